```python
import jax
import jax.numpy as jnp
from jax import lax
import numpy as np

D_MODEL = 1024
BATCH = 8
SEQ = 4096
DEPTH = 4

CTX_LEN = 256
GRID_W = 64
EPS = 1e-6

D_CONV = 512
CONV_WIDTH = 31
GLA_HEADS = 4
GLA_DK = 64
GLA_DV = 128
GLA_QK = GLA_HEADS * GLA_DK
GLA_V = GLA_HEADS * GLA_DV
GLA_RANK = 16
GLA_GATE_NORM = 16.0
GLA_CHUNK = 64
ATT_HEADS = 8
ATT_KV_HEADS = 2
ATT_GROUP = ATT_HEADS // ATT_KV_HEADS
ATT_HD = 64
ATT_Q = ATT_HEADS * ATT_HD
ATT_KV = ATT_KV_HEADS * ATT_HD
ATT_BLOCK = 128
ROPE_AXIS_DIM = ATT_HD // 2
ROPE_THETA = 10000.0

IN_SPLITS = (D_CONV, D_CONV, D_CONV,
             GLA_QK, GLA_QK, GLA_V, GLA_V, GLA_RANK, GLA_RANK,
             ATT_Q, ATT_KV, ATT_KV, ATT_Q,
             D_MODEL, D_MODEL, D_MODEL)
N_IN = sum(IN_SPLITS)

kernel_name = 'hybrid_flow_block'


def rms_norm(x, g):
    xf = x.astype(jnp.float32)
    y = xf * lax.rsqrt(jnp.mean(xf * xf, axis=-1, keepdims=True) + EPS)
    return (y * g.astype(jnp.float32)).astype(x.dtype)


def layer_norm(x, g, b):
    xf = x.astype(jnp.float32)
    mu = jnp.mean(xf, axis=-1, keepdims=True)
    var = jnp.mean(jnp.square(xf - mu), axis=-1, keepdims=True)
    y = (xf - mu) * lax.rsqrt(var + EPS)
    return (y * g.astype(jnp.float32) + b.astype(jnp.float32)).astype(x.dtype)


def split_cols(p):
    idx = np.cumsum(np.array(IN_SPLITS))[:-1].tolist()
    return jnp.split(p, idx, axis=-1)


def conv_module(val, glu, gate, dw_w, dw_b, ln_g, ln_b, w_o):
    u = val * jax.nn.sigmoid(glu)
    pad = CONV_WIDTH // 2
    u = lax.conv_general_dilated(u, dw_w[:, None, :], window_strides=(1,),
                                 padding=[(pad, pad)],
                                 dimension_numbers=('NWC', 'WIO', 'NWC'),
                                 feature_group_count=D_CONV) + dw_b
    u = jax.nn.silu(layer_norm(u, ln_g, ln_b)) * jax.nn.silu(gate)
    return u @ w_o


def gla_scan(q, k, v, log_a, s0):
    b_, t_, h_, _ = q.shape
    dv = v.shape[-1]
    n = t_ // GLA_CHUNK

    def chunks(z):
        return z.astype(jnp.float32).reshape(b_, n, GLA_CHUNK, h_, z.shape[-1])

    q, k, v, log_a = chunks(q), chunks(k), chunks(v), chunks(log_a)
    cum = jnp.cumsum(log_a, axis=2)
    last = cum[:, :, -1]
    q_in = q * jnp.exp(cum)
    k_in = k * jnp.exp(-cum)
    mask = jnp.tril(jnp.ones((GLA_CHUNK, GLA_CHUNK), dtype=bool))
    att = jnp.where(mask, jnp.einsum('bnihd,bnjhd->bnhij', q_in, k_in), 0.0)
    o_intra = jnp.einsum('bnhij,bnjhe->bnihe', att, v)
    k_state = k * jnp.exp(last[:, :, None] - cum)
    contrib = jnp.einsum('bnjhd,bnjhe->bnhde', k_state, v)
    decay = jnp.exp(last)

    def step(s, inp):
        dec, con = inp
        return dec[..., None] * s + con, s

    s_fin, s_prev = lax.scan(step, s0, (jnp.swapaxes(decay, 0, 1), jnp.swapaxes(contrib, 0, 1)))
    o_inter = jnp.einsum('bnihd,bnhde->bnihe', q_in, jnp.swapaxes(s_prev, 0, 1))
    return (o_intra + o_inter).reshape(b_, t_, h_, dv), s_fin


def gla_heads(q, k, v):
    b_, t_, _ = q.shape
    return (q.reshape(b_, t_, GLA_HEADS, GLA_DK) * (GLA_DK ** -0.5),
            k.reshape(b_, t_, GLA_HEADS, GLA_DK),
            v.reshape(b_, t_, GLA_HEADS, GLA_DV))


def gla_log_gate(lr, w_up, b_up):
    b_, t_, _ = lr.shape
    z = (lr @ w_up + b_up).astype(jnp.float32)
    return (jax.nn.log_sigmoid(z) / GLA_GATE_NORM).reshape(b_, t_, GLA_HEADS, GLA_DK)


def gla_output(o, gate, norm_g, w_o):
    b_, t_ = o.shape[:2]
    o = rms_norm(o, norm_g).reshape(b_, t_, GLA_V).astype(gate.dtype)
    return (o * jax.nn.silu(gate)) @ w_o


def gla_branch(xp, cp, w_up, b_up, norm_g, w_o, need_ctx):
    xq, xk, xv, xg, xlf, xlb = xp
    cq, ck, cv, cg, clf, clb = cp
    xq, xk, xv = gla_heads(xq, xk, xv)
    cq, ck, cv = gla_heads(cq, ck, cv)
    s0 = jnp.zeros((xq.shape[0], GLA_HEADS, GLA_DK, GLA_DV), jnp.float32)

    def flip(z):
        return jnp.flip(z, axis=1)

    oc_f, sc_f = gla_scan(cq, ck, cv, gla_log_gate(clf, w_up[0], b_up[0]), s0)
    ox_f, _ = gla_scan(xq, xk, xv, gla_log_gate(xlf, w_up[0], b_up[0]), sc_f)
    oc_b, sc_b = gla_scan(flip(cq), flip(ck), flip(cv),
                          flip(gla_log_gate(clb, w_up[1], b_up[1])), s0)
    ox_b, _ = gla_scan(flip(xq), flip(xk), flip(xv),
                       flip(gla_log_gate(xlb, w_up[1], b_up[1])), sc_b)
    y_x = gla_output(ox_f + flip(ox_b), xg, norm_g, w_o)
    y_c = gla_output(oc_f + flip(oc_b), cg, norm_g, w_o) if need_ctx else None
    return y_x, y_c


def rope_tables(t_):
    n_rows = t_ // GRID_W
    row = jnp.repeat(jnp.arange(n_rows, dtype=jnp.float32), GRID_W)
    col = jnp.tile(jnp.arange(GRID_W, dtype=jnp.float32), n_rows)
    n_freq = ROPE_AXIS_DIM // 2
    freqs = ROPE_THETA ** (-jnp.arange(n_freq, dtype=jnp.float32) / n_freq)
    ang_r = row[:, None] * freqs
    ang_c = col[:, None] * freqs
    return jnp.cos(ang_r), jnp.sin(ang_r), jnp.cos(ang_c), jnp.sin(ang_c)


def rotate_half(x, cos, sin):
    x1, x2 = jnp.split(x, 2, axis=-1)
    cos = cos[:, None, :]
    sin = sin[:, None, :]
    return jnp.concatenate([x1 * cos - x2 * sin, x2 * cos + x1 * sin], axis=-1)


def rope_2d(x, tabs):
    cr, sr, cc, sc = tabs
    xr, xcol = jnp.split(x.astype(jnp.float32), 2, axis=-1)
    return jnp.concatenate([rotate_half(xr, cr, sr), rotate_half(xcol, cc, sc)], axis=-1).astype(x.dtype)


def attn_heads(q, k, v, qn_g, kn_g, tabs):
    b_, t_, _ = q.shape
    q = rms_norm(q.reshape(b_, t_, ATT_HEADS, ATT_HD), qn_g)
    k = rms_norm(k.reshape(b_, t_, ATT_KV_HEADS, ATT_HD), kn_g)
    if tabs is not None:
        q, k = rope_2d(q, tabs), rope_2d(k, tabs)
    return (q.reshape(b_, t_, ATT_KV_HEADS, ATT_GROUP, ATT_HD), k,
            v.reshape(b_, t_, ATT_KV_HEADS, ATT_HD))


def gqa_attend(q, k, v):
    s = jnp.einsum('bqkgd,bskd->bkgqs', q, k).astype(jnp.float32) * (ATT_HD ** -0.5)
    p = jax.nn.softmax(s, axis=-1).astype(v.dtype)
    return jnp.einsum('bkgqs,bskd->bqkgd', p, v)


def attn_branch(xp, cp, qn_g, kn_g, w_o, need_ctx):
    xq, xk, xv, xg = xp
    cq, ck, cv, cg = cp
    b_, t_, _ = xq.shape
    xq, xk, xv = attn_heads(xq, xk, xv, qn_g, kn_g, rope_tables(t_))
    cq, ck, cv = attn_heads(cq, ck, cv, qn_g, kn_g, None)
    k_all = jnp.concatenate([ck, xk], axis=1)
    v_all = jnp.concatenate([cv, xv], axis=1)
    n_blk = t_ // ATT_BLOCK
    q_blk = jnp.swapaxes(xq.reshape(b_, n_blk, ATT_BLOCK, ATT_KV_HEADS, ATT_GROUP, ATT_HD), 0, 1)
    o = lax.map(lambda qb: gqa_attend(qb, k_all, v_all), q_blk)
    o = jnp.swapaxes(o, 0, 1).reshape(b_, t_, ATT_Q)
    y_x = (o * jax.nn.silu(xg)) @ w_o
    y_c = None
    if need_ctx:
        oc = gqa_attend(cq, ck, cv).reshape(b_, cq.shape[1], ATT_Q)
        y_c = (oc * jax.nn.silu(cg)) @ w_o
    return y_x, y_c


def hybrid_layer(xs, cs, c, c_ctx, norm_g, w_mod, b_mod, w_in, b_in, dw_w, dw_b, ln_g, ln_b,
                 w_conv_out, gla_w_gate, gla_b_gate, gla_norm_g, w_gla_out, q_norm_g, k_norm_g,
                 w_attn_out, w_out, need_ctx):
    shift_x, scale_x, gate_x = jnp.split(jax.nn.silu(c) @ w_mod + b_mod, 3, axis=-1)
    shift_c, scale_c, gate_c = jnp.split(jax.nn.silu(c_ctx) @ w_mod + b_mod, 3, axis=-1)
    hx = rms_norm(xs, norm_g) * (1 + scale_x[:, None]) + shift_x[:, None]
    hc = rms_norm(cs, norm_g) * (1 + scale_c) + shift_c
    px = split_cols(hx @ w_in + b_in)
    pc = split_cols(hc @ w_in + b_in)

    ya_x = conv_module(*px[0:3], dw_w, dw_b, ln_g, ln_b, w_conv_out)
    yb_x, yb_c = gla_branch(px[3:9], pc[3:9], gla_w_gate, gla_b_gate, gla_norm_g, w_gla_out, need_ctx)
    yc_x, yc_c = attn_branch(px[9:13], pc[9:13], q_norm_g, k_norm_g, w_attn_out, need_ctx)

    def merge(p, ya, yb, yc):
        return (jax.nn.sigmoid(p[13]) * ya + jax.nn.sigmoid(p[14]) * yb
                + jax.nn.sigmoid(p[15]) * yc) @ w_out

    xs = xs + gate_x[:, None] * merge(px, ya_x, yb_x, yc_x)
    if need_ctx:
        ya_c = conv_module(*pc[0:3], dw_w, dw_b, ln_g, ln_b, w_conv_out)
        cs = cs + gate_c * merge(pc, ya_c, yb_c, yc_c)
    return xs, cs


def setup_inputs(seed: int = 0) -> dict:
    key = jax.random.key(seed)
    ks = jax.random.split(key, 22)

    def nrm(k, shape, s):
        return jax.random.normal(k, shape, jnp.float32) * s

    return {
        'x': nrm(ks[0], (BATCH, SEQ, D_MODEL), 1.0),
        'c': nrm(ks[1], (BATCH, D_MODEL), 1.0),
        'ctx': nrm(ks[2], (BATCH, CTX_LEN, D_MODEL), 1.0),
        'c_ctx': nrm(ks[3], (D_MODEL,), 1.0),
        'norm_g': 1.0 + nrm(ks[4], (DEPTH, D_MODEL), 0.02),
        'w_mod': nrm(ks[5], (DEPTH, D_MODEL, 3 * D_MODEL), 0.5 * D_MODEL ** -0.5),
        'b_mod': nrm(ks[6], (DEPTH, 3 * D_MODEL), 0.01),
        'w_in': nrm(ks[7], (DEPTH, D_MODEL, N_IN), D_MODEL ** -0.5),
        'b_in': nrm(ks[8], (DEPTH, N_IN), 0.01),
        'conv_dw_w': nrm(ks[9], (DEPTH, CONV_WIDTH, D_CONV), CONV_WIDTH ** -0.5),
        'conv_dw_b': nrm(ks[10], (DEPTH, D_CONV), 0.01),
        'conv_ln_g': 1.0 + nrm(ks[11], (DEPTH, D_CONV), 0.02),
        'conv_ln_b': nrm(ks[12], (DEPTH, D_CONV), 0.01),
        'w_conv_out': nrm(ks[13], (DEPTH, D_CONV, D_MODEL), D_CONV ** -0.5),
        'gla_w_gate': nrm(ks[14], (DEPTH, 2, GLA_RANK, GLA_QK), GLA_RANK ** -0.5),
        'gla_b_gate': nrm(ks[15], (DEPTH, 2, GLA_QK), 0.1),
        'gla_norm_g': 1.0 + nrm(ks[16], (DEPTH, GLA_DV), 0.02),
        'w_gla_out': nrm(ks[17], (DEPTH, GLA_V, D_MODEL), GLA_V ** -0.5),
        'q_norm_g': 1.0 + nrm(ks[18], (DEPTH, ATT_HD), 0.02),
        'k_norm_g': 1.0 + nrm(ks[19], (DEPTH, ATT_HD), 0.02),
        'w_attn_out': nrm(ks[20], (DEPTH, ATT_Q, D_MODEL), ATT_Q ** -0.5),
        'w_out': nrm(ks[21], (DEPTH, D_MODEL, D_MODEL), D_MODEL ** -0.5),
    }


def reference(x, c, ctx, c_ctx, norm_g, w_mod, b_mod, w_in, b_in, conv_dw_w, conv_dw_b,
              conv_ln_g, conv_ln_b, w_conv_out, gla_w_gate, gla_b_gate, gla_norm_g, w_gla_out,
              q_norm_g, k_norm_g, w_attn_out, w_out):
    xs, cs = x, ctx
    for l in range(DEPTH):
        xs, cs = hybrid_layer(xs, cs, c, c_ctx, norm_g[l], w_mod[l], b_mod[l], w_in[l], b_in[l],
                              conv_dw_w[l], conv_dw_b[l], conv_ln_g[l], conv_ln_b[l], w_conv_out[l],
                              gla_w_gate[l], gla_b_gate[l], gla_norm_g[l], w_gla_out[l],
                              q_norm_g[l], k_norm_g[l], w_attn_out[l], w_out[l],
                              need_ctx=(l < DEPTH - 1))
    return xs
```

```python
import functools
import math

import numpy as np
import jax
import jax.numpy as jnp
from jax import lax
from jax.experimental import pallas as pl
from jax.experimental.pallas import tpu as pltpu

F32 = jnp.float32
BF16 = jnp.bfloat16
HIGHEST = lax.Precision.HIGHEST

D_MODEL = 1024
BATCH = 8
SEQ = 4096
DEPTH = 4
CTX_LEN = 256
GRID_W = 64
EPS = 1e-6
D_CONV = 512
CONV_WIDTH = 31
CONV_PAD = CONV_WIDTH // 2
GLA_HEADS = 4
GLA_DK = 64
GLA_DV = 128
GLA_QK = GLA_HEADS * GLA_DK
GLA_V = GLA_HEADS * GLA_DV
GLA_RANK = 16
GLA_GATE_NORM = 16.0
GLA_CHUNK = 64
ATT_HEADS = 8
ATT_KV_HEADS = 2
ATT_GROUP = ATT_HEADS // ATT_KV_HEADS
ATT_HD = 64
ATT_Q = ATT_HEADS * ATT_HD
ATT_KV = ATT_KV_HEADS * ATT_HD
ROPE_AXIS_DIM = ATT_HD // 2
ROPE_THETA = 10000.0

LANES = 128
MX = BATCH * SEQ
MC = BATCH * CTX_LEN
M_ALL = MX + MC

O_VAL, O_GLU, O_CGATE = 0, 512, 1024
O_GQ, O_GK, O_GV, O_GG = 1536, 1792, 2048, 2560
O_LR = 3072
O_AQ, O_AK, O_AV, O_AG = 3104, 3616, 3744, 3872
O_MA = 4384
N_IN = 7456

W_M = 0
W_CONV = 3072
W_GLA = 4608
W_AQ = 6144
W_AG = 6656
W_AK = 7168
W_AV = 7296
W_LR = 7424
NW = 7552

P_SIG = 0
P_U = 3072
P_CG = 3584
P_GQK = 4096
P_GV = 4608
P_GG = 5120
P_AQ = 5632
P_AG = 6144
P_AKV = 6656
P_LR = 6912
NP = 7040

TM = 512
N_XT = MX // TM
N_CT = MC // TM
TILES_PER_SEQ = SEQ // TM
TC = 256
XB = MX // TC
BLK_PER_SEQ = SEQ // TC
HALO = 16
CONV_RC = 32
KB = 512
Q_PRESCALE = (ATT_HD ** -0.5) * math.log2(math.e)

VMEM_LIMIT = 56 * 1024 * 1024


def _cparams(n_axes, vmem=VMEM_LIMIT):
    return pltpu.CompilerParams(dimension_semantics=("arbitrary",) * n_axes,
                                vmem_limit_bytes=vmem)


def _silu(x):
    return x * jax.nn.sigmoid(x)


def _mod_kernel(c_ref, w_ref, b_ref, o_ref):
    s = _silu(c_ref[...])
    o_ref[...] = jnp.dot(s, w_ref[...], preferred_element_type=F32, precision=HIGHEST) + b_ref[...]


def _modulation(cc, w_mod, b_mod):
    nt = 3 * D_MODEL // 1024
    return pl.pallas_call(
        _mod_kernel,
        grid=(DEPTH, nt),
        in_specs=[pl.BlockSpec((16, D_MODEL), lambda l, n: (0, 0)),
                  pl.BlockSpec((None, D_MODEL, 1024), lambda l, n: (l, 0, n)),
                  pl.BlockSpec((None, 1, 1024), lambda l, n: (l, 0, n))],
        out_specs=pl.BlockSpec((None, 16, 1024), lambda l, n: (l, 0, n)),
        out_shape=jax.ShapeDtypeStruct((DEPTH, 16, 3 * D_MODEL), F32),
        compiler_params=_cparams(2),
        name="modulation",
    )(cc, w_mod, b_mod.reshape(DEPTH, 1, 3 * D_MODEL))


def _head_norm(xv, gain, bd):
    sq = xv * xv
    hi = sq.astype(BF16)
    lo = (sq - hi.astype(F32)).astype(BF16)
    ss = (jnp.dot(hi, bd, preferred_element_type=F32)
          + jnp.dot(lo, bd, preferred_element_type=F32))
    return xv * lax.rsqrt(ss * (1.0 / ATT_HD) + EPS) * gain


def _rope(xv, cosv, sinv):
    parts = []
    for s in range(xv.shape[1] // LANES):
        sl = slice(s * LANES, (s + 1) * LANES)
        xs = xv[:, sl]
        up = pltpu.roll(xs, LANES - 16, axis=1)
        dn = pltpu.roll(xs, 16, axis=1)
        lane = lax.broadcasted_iota(jnp.int32, xs.shape, 1)
        partner = jnp.where((lane & 16) == 0, up, dn)
        parts.append(xs * cosv[:, sl] + partner * sinv[:, sl])
    return jnp.concatenate(parts, axis=1) if len(parts) > 1 else parts[0]


def _inproj_kernel(x_ref, mod_ref, g_ref, w_ref, b_ref, cos_ref, sin_ref, qg_ref, kg_ref,
                   bd_ref, o_ref):
    i = pl.program_id(0)
    is_ctx = i >= N_XT
    row = jnp.where(is_ctx, BATCH, i // TILES_PER_SEQ)
    x = x_ref[...]
    ms = jnp.mean(x * x, axis=-1, keepdims=True)
    y = x * lax.rsqrt(ms + EPS) * g_ref[...]
    m = mod_ref[pl.ds(row, 1), :]
    shift = m[:, 0:D_MODEL]
    scale = m[:, D_MODEL:2 * D_MODEL]
    h = (y * (1.0 + scale) + shift).astype(BF16)

    def proj(a, n):
        return jnp.dot(h, w_ref[:, a:a + n], preferred_element_type=F32) + b_ref[:, a:a + n]

    def put(a, val):
        o_ref[:, a:a + val.shape[1]] = val.astype(BF16)

    for k in range(3):
        put(P_SIG + 1024 * k, jax.nn.sigmoid(proj(W_M + 1024 * k, 1024)))
    put(P_U, proj(W_CONV, 512) * jax.nn.sigmoid(proj(W_CONV + 512, 512)))
    put(P_CG, _silu(proj(W_CONV + 1024, 512)))
    qk = proj(W_GLA, 512)
    lane = lax.broadcasted_iota(jnp.int32, qk.shape, 1)
    put(P_GQK, jnp.where(lane < GLA_QK, qk * (GLA_DK ** -0.5), qk))
    put(P_GV, proj(W_GLA + 512, 512))
    put(P_GG, _silu(proj(W_GLA + 1024, 512)))

    cosv = jnp.where(is_ctx, 1.0, cos_ref[...])
    sinv = jnp.where(is_ctx, 0.0, sin_ref[...])
    bd = bd_ref[...]
    aq = _head_norm(proj(W_AQ, ATT_Q), qg_ref[...], bd)
    put(P_AQ, _rope(aq, cosv, sinv) * Q_PRESCALE)
    put(P_AG, _silu(proj(W_AG, ATT_Q)))
    akv = proj(W_AK, 2 * ATT_KV)
    ak = _head_norm(akv[:, :ATT_KV], kg_ref[...], bd[:ATT_KV, :ATT_KV])
    put(P_AKV, _rope(ak, cosv[:, :ATT_KV], sinv[:, :ATT_KV]))
    put(P_AKV + ATT_KV, akv[:, ATT_KV:])
    put(P_LR, proj(W_LR, LANES))


def _inproj(l, s_all, mod, norm_g, w_p, b_p, cos_t, sin_t, qg, kg, bd):
    const = lambda i: (0, 0)
    pos = lambda i: (jnp.where(i >= N_XT, 0, i % TILES_PER_SEQ), 0)
    return pl.pallas_call(
        _inproj_kernel,
        grid=(M_ALL // TM,),
        in_specs=[pl.BlockSpec((TM, D_MODEL), lambda i: (i, 0)),
                  pl.BlockSpec((None, 16, 3 * D_MODEL), lambda i: (l, 0, 0)),
                  pl.BlockSpec((None, 1, D_MODEL), lambda i: (l, 0, 0)),
                  pl.BlockSpec((None, D_MODEL, NW), lambda i: (l, 0, 0),
                               pipeline_mode=pl.Buffered(1)),
                  pl.BlockSpec((None, 1, NW), lambda i: (l, 0, 0)),
                  pl.BlockSpec((TM, ATT_Q), pos),
                  pl.BlockSpec((TM, ATT_Q), pos),
                  pl.BlockSpec((None, 1, ATT_Q), lambda i: (l, 0, 0)),
                  pl.BlockSpec((None, 1, ATT_KV), lambda i: (l, 0, 0)),
                  pl.BlockSpec((ATT_Q, ATT_Q), const)],
        out_specs=pl.BlockSpec((TM, NP), lambda i: (i, 0)),
        out_shape=jax.ShapeDtypeStruct((M_ALL, NP), BF16),
        compiler_params=_cparams(1),
        name="inproj",
    )(s_all, mod, norm_g, w_p, b_p, cos_t, sin_t, qg, kg, bd)


def _conv_kernel(u_ref, ul_ref, ur_ref, sg_ref, dw_ref, dwb_ref, lng_ref, lnb_ref, o_ref, ext_ref):
    i = pl.program_id(0)
    j = i % BLK_PER_SEQ
    is_x = i < XB
    left_ok = jnp.logical_and(is_x, j != 0)
    right_ok = jnp.logical_and(is_x, j != BLK_PER_SEQ - 1)
    ext_ref[0:HALO, :] = jnp.where(left_ok, ul_ref[...].astype(F32), 0.0)
    ext_ref[HALO:HALO + TC, :] = u_ref[...].astype(F32)
    ext_ref[HALO + TC:, :] = jnp.where(right_ok, ur_ref[...].astype(F32), 0.0)
    bias = dwb_ref[...]
    lng = lng_ref[...]
    lnb = lnb_ref[...]
    for c in range(TC // CONV_RC):
        r0 = c * CONV_RC
        acc = jnp.broadcast_to(bias, (CONV_RC, D_CONV))
        for t in range(CONV_WIDTH):
            off = r0 + t + HALO - CONV_PAD
            acc = acc + ext_ref[off:off + CONV_RC, :] * dw_ref[t:t + 1, :]
        mu = jnp.mean(acc, axis=-1, keepdims=True)
        d = acc - mu
        var = jnp.mean(d * d, axis=-1, keepdims=True)
        yn = d * lax.rsqrt(var + EPS) * lng + lnb
        o_ref[r0:r0 + CONV_RC, :] = (_silu(yn) * sg_ref[r0:r0 + CONV_RC, :].astype(F32)).astype(BF16)


def _conv(l, p, dw_w, dw_b, ln_g, ln_b):
    nhb = M_ALL // HALO
    per = TC // HALO
    cu = P_U // D_CONV
    cg = P_CG // D_CONV
    vec = lambda i: (l, 0, 0)
    return pl.pallas_call(
        _conv_kernel,
        grid=(M_ALL // TC,),
        in_specs=[pl.BlockSpec((TC, D_CONV), lambda i: (i, cu)),
                  pl.BlockSpec((HALO, D_CONV), lambda i: (jnp.maximum(i * per - 1, 0), cu)),
                  pl.BlockSpec((HALO, D_CONV), lambda i: (jnp.minimum((i + 1) * per, nhb - 1), cu)),
                  pl.BlockSpec((TC, D_CONV), lambda i: (i, cg)),
                  pl.BlockSpec((None, 32, D_CONV), vec),
                  pl.BlockSpec((None, 1, D_CONV), vec),
                  pl.BlockSpec((None, 1, D_CONV), vec),
                  pl.BlockSpec((None, 1, D_CONV), vec)],
        out_specs=pl.BlockSpec((TC, D_CONV), lambda i: (i, 0)),
        out_shape=jax.ShapeDtypeStruct((M_ALL, D_CONV), BF16),
        scratch_shapes=[pltpu.VMEM((TC + 2 * HALO, D_CONV), F32)],
        compiler_params=_cparams(1),
        name="conv",
    )(p, p, p, p, dw_w, dw_b, ln_g, ln_b)


def _gla_kernel(qk_ref, v_ref, lr_ref, wup_ref, bup_ref, tri_ref, o_ref, s_ref):
    d = pl.program_id(1)
    step = pl.program_id(2)

    @pl.when(step == 0)
    def _():
        s_ref[...] = jnp.zeros_like(s_ref)

    tri = tri_ref[...]
    tri4 = jnp.concatenate([tri] * GLA_HEADS, axis=0) > 0.5
    lane_h = lax.broadcasted_iota(jnp.int32, (GLA_CHUNK, GLA_QK), 1) // GLA_DK
    srow_h = lax.broadcasted_iota(jnp.int32, (GLA_QK, GLA_V), 0) // GLA_DK
    scol_h = lax.broadcasted_iota(jnp.int32, (GLA_QK, GLA_V), 1) // GLA_DV
    smask = srow_h == scol_h
    ones_cl = jnp.ones((GLA_CHUNK, LANES), F32)
    wup = wup_ref[...]
    bup = bup_ref[...]
    n_chunks = TC // GLA_CHUNK
    for n in range(n_chunks):
        c = jnp.where(d == 0, n, n_chunks - 1 - n)
        rows = pl.ds(pl.multiple_of(c * GLA_CHUNK, GLA_CHUNK), GLA_CHUNK)
        qk = qk_ref[rows, :].astype(F32)
        q = qk[:, :GLA_QK]
        k = qk[:, GLA_QK:]
        v = v_ref[rows, :]
        z = jnp.dot(lr_ref[rows, :].astype(F32), wup, preferred_element_type=F32,
                    precision=HIGHEST) + bup
        la = (jnp.minimum(z, 0.0) - jnp.log(1.0 + jnp.exp(-jnp.abs(z)))) * (1.0 / GLA_GATE_NORM)
        cum = jnp.dot(tri, la, preferred_element_type=F32, precision=HIGHEST)
        total = jnp.where(d == 0, cum[GLA_CHUNK - 1:GLA_CHUNK, :], cum[0:1, :])
        q_in = q * jnp.exp(cum)
        k_in = (k * jnp.exp(-cum)).astype(BF16)
        k_st = (k * jnp.exp(total - cum)).astype(BF16)
        q_in_b = q_in.astype(BF16)
        tot_col = lax.dot_general(la, ones_cl, (((0,), (0,)), ((), ())),
                                  preferred_element_type=F32, precision=HIGHEST)
        dcol = jnp.exp(tot_col)
        dcol = jnp.concatenate([dcol] * (GLA_V // LANES), axis=1)
        q_stack = jnp.concatenate(
            [jnp.where(lane_h == h, q_in, 0.0).astype(BF16) for h in range(GLA_HEADS)], axis=0)
        att = lax.dot_general(q_stack, k_in, (((1,), (1,)), ((), ())),
                              preferred_element_type=F32)
        att = jnp.where(tri4, att, 0.0).astype(BF16)
        s_old = s_ref[...]
        o_inter = jnp.dot(q_in_b, s_old.astype(BF16), preferred_element_type=F32)
        o_intra = jnp.concatenate(
            [jnp.dot(att[h * GLA_CHUNK:(h + 1) * GLA_CHUNK, :],
                     v[:, h * GLA_DV:(h + 1) * GLA_DV], preferred_element_type=F32)
             for h in range(GLA_HEADS)], axis=1)
        o_ref[rows, :] = (o_intra + o_inter).astype(BF16)
        kv = lax.dot_general(k_st, v, (((0,), (0,)), ((), ())), preferred_element_type=F32)
        s_ref[...] = dcol * s_old + jnp.where(smask, kv, 0.0)


def _gla(l, p, wup, bup, tri):
    def rb(b, d, s):
        xblk = b * BLK_PER_SEQ + jnp.where(d == 0, s - 1, BLK_PER_SEQ - s)
        return jnp.where(s == 0, XB + b, xblk)

    cqk = P_GQK // 512
    cv = P_GV // 512
    clr = P_LR // LANES
    return pl.pallas_call(
        _gla_kernel,
        grid=(BATCH, 2, BLK_PER_SEQ + 1),
        in_specs=[pl.BlockSpec((TC, 512), lambda b, d, s: (rb(b, d, s), cqk)),
                  pl.BlockSpec((TC, GLA_V), lambda b, d, s: (rb(b, d, s), cv)),
                  pl.BlockSpec((TC, LANES), lambda b, d, s: (rb(b, d, s), clr)),
                  pl.BlockSpec((None, None, LANES, GLA_QK), lambda b, d, s: (l, d, 0, 0)),
                  pl.BlockSpec((None, None, 1, GLA_QK), lambda b, d, s: (l, d, 0, 0)),
                  pl.BlockSpec((None, GLA_CHUNK, GLA_CHUNK), lambda b, d, s: (d, 0, 0))],
        out_specs=pl.BlockSpec((None, TC, GLA_V), lambda b, d, s: (d, rb(b, d, s), 0)),
        out_shape=jax.ShapeDtypeStruct((2, M_ALL, GLA_V), BF16),
        scratch_shapes=[pltpu.VMEM((GLA_QK, GLA_V), F32)],
        compiler_params=_cparams(3),
        name="gla",
    )(p, p, p, wup, bup, tri)


NKEYS = CTX_LEN + SEQ


def _attn_kernel(q_ref, g_ref, ckv_ref, xkv_ref, selk_ref, selv_ref, one_ref, o_ref,
                 kz_ref, vx_ref, m_ref, acc_ref):
    qi = pl.program_id(1)

    @pl.when(qi == 0)
    def _():
        for h in range(ATT_KV_HEADS):
            for e in range(2):
                sk = selk_ref[h, e]
                sv = selv_ref[h, e]
                ones_row = one_ref[e]
                for (src, r0, n) in ((ckv_ref, 0, CTX_LEN), (xkv_ref, CTX_LEN, SEQ)):
                    for r in range(0, n, 1024):
                        nr = min(1024, n - r)
                        blk = src[r:r + nr, :]
                        kz_ref[h, e, r0 + r:r0 + r + nr, :] = jnp.dot(
                            blk, sk, preferred_element_type=F32).astype(BF16)
                        vx_ref[h, e, r0 + r:r0 + r + nr, :] = (jnp.dot(
                            blk, sv, preferred_element_type=F32) + ones_row).astype(BF16)

    m_ref[...] = jnp.full_like(m_ref, -jnp.inf)
    acc_ref[...] = jnp.zeros_like(acc_ref)

    def block(k0, nk):
        for pr in range(ATT_HEADS // 2):
            h = pr // (ATT_GROUP // 2)
            qp = q_ref[:, pr * LANES:(pr + 1) * LANES]
            for e in range(2):
                hd = 2 * pr + e
                kz = kz_ref[h, e, pl.ds(k0, nk), :]
                s = lax.dot_general(qp, kz, (((1,), (1,)), ((), ())),
                                    preferred_element_type=F32)
                m_prev = m_ref[hd]
                m_next = jnp.maximum(m_prev, jnp.max(s, axis=-1, keepdims=True))
                p = jnp.exp2(s - m_next[:, 0:1]).astype(BF16)
                alpha = jnp.exp2(m_prev - m_next)
                vx = vx_ref[h, e, pl.ds(k0, nk), :]
                acc_ref[hd] = alpha * acc_ref[hd] + jnp.dot(p, vx, preferred_element_type=F32)
                m_ref[hd] = m_next

    block(0, CTX_LEN)

    @pl.when(qi > 0)
    def _():
        def body(t, carry):
            block(pl.multiple_of(CTX_LEN + t * KB, KB // 2), KB)
            return carry
        lax.fori_loop(0, SEQ // KB, body, 0)

    lane = lax.broadcasted_iota(jnp.int32, (TC, LANES), 1)
    for pr in range(ATT_HEADS // 2):
        a0 = acc_ref[2 * pr]
        a1 = acc_ref[2 * pr + 1]
        o0 = a0 / a0[:, ATT_HD:ATT_HD + 1]
        o1 = a1 / a1[:, 0:1]
        slab = jnp.where(lane < ATT_HD, o0, o1)
        sl = slice(pr * LANES, (pr + 1) * LANES)
        o_ref[:, sl] = (slab * g_ref[:, sl].astype(F32)).astype(BF16)


def _attn(p, selk, selv, onerow):
    cq = P_AQ // ATT_Q
    cg = P_AG // ATT_Q
    ckv = P_AKV // (2 * ATT_KV)

    def qrow(b, qi):
        return jnp.where(qi == 0, XB + b, b * BLK_PER_SEQ + qi - 1)

    return pl.pallas_call(
        _attn_kernel,
        grid=(BATCH, BLK_PER_SEQ + 1),
        in_specs=[pl.BlockSpec((TC, ATT_Q), lambda b, qi: (qrow(b, qi), cq)),
                  pl.BlockSpec((TC, ATT_Q), lambda b, qi: (qrow(b, qi), cg)),
                  pl.BlockSpec((CTX_LEN, 2 * ATT_KV), lambda b, qi: (XB + b, ckv)),
                  pl.BlockSpec((SEQ, 2 * ATT_KV), lambda b, qi: (b, ckv)),
                  pl.BlockSpec((ATT_KV_HEADS, 2, 2 * ATT_KV, LANES), lambda b, qi: (0, 0, 0, 0)),
                  pl.BlockSpec((ATT_KV_HEADS, 2, 2 * ATT_KV, LANES), lambda b, qi: (0, 0, 0, 0)),
                  pl.BlockSpec((2, 1, LANES), lambda b, qi: (0, 0, 0))],
        out_specs=pl.BlockSpec((TC, ATT_Q), lambda b, qi: (qrow(b, qi), 0)),
        out_shape=jax.ShapeDtypeStruct((M_ALL, ATT_Q), BF16),
        scratch_shapes=[pltpu.VMEM((ATT_KV_HEADS, 2, NKEYS, LANES), BF16),
                        pltpu.VMEM((ATT_KV_HEADS, 2, NKEYS, LANES), BF16),
                        pltpu.VMEM((ATT_HEADS, TC, LANES), F32),
                        pltpu.VMEM((ATT_HEADS, TC, LANES), F32)],
        compiler_params=_cparams(2),
        name="attention",
    )(p, p, p, p, selk, selv, onerow)


def _final_kernel(s_ref, sig_ref, gg_ref, ua_ref, of_ref, ob_ref, oc_ref, mod_ref,
                  wc_ref, wg_ref, wa_ref, wo_ref, gn_ref, o_ref):
    i = pl.program_id(0)
    row = jnp.where(i >= N_XT, BATCH, i // TILES_PER_SEQ)
    ya = jnp.dot(ua_ref[...], wc_ref[...], preferred_element_type=F32)
    og = of_ref[...].astype(F32) + ob_ref[...].astype(F32)
    gn = gn_ref[...]
    parts = []
    for h in range(GLA_HEADS):
        oh = og[:, h * GLA_DV:(h + 1) * GLA_DV]
        ms = jnp.mean(oh * oh, axis=-1, keepdims=True)
        parts.append(oh * lax.rsqrt(ms + EPS) * gn)
    on = jnp.concatenate(parts, axis=1) * gg_ref[...].astype(F32)
    yb = jnp.dot(on.astype(BF16), wg_ref[...], preferred_element_type=F32)
    yc = jnp.dot(oc_ref[...], wa_ref[...], preferred_element_type=F32)
    merged = (sig_ref[:, 0:D_MODEL].astype(F32) * ya
              + sig_ref[:, D_MODEL:2 * D_MODEL].astype(F32) * yb
              + sig_ref[:, 2 * D_MODEL:3 * D_MODEL].astype(F32) * yc)
    out = jnp.dot(merged.astype(BF16), wo_ref[...], preferred_element_type=F32)
    gate = mod_ref[pl.ds(row, 1), 2 * D_MODEL:3 * D_MODEL]
    o_ref[...] = s_ref[...] + gate * out


def _final(l, n_tiles, s_all, p, ua, o_gla, oc, mod, wc, wg, wa, wo, gn):
    rowblk = lambda i: (i, 0)
    wspec = lambda k: pl.BlockSpec((None, k, D_MODEL), lambda i: (l, 0, 0))
    return pl.pallas_call(
        _final_kernel,
        grid=(n_tiles,),
        in_specs=[pl.BlockSpec((TM, D_MODEL), rowblk),
                  pl.BlockSpec((TM, 3 * D_MODEL), lambda i: (i, P_SIG // (3 * D_MODEL))),
                  pl.BlockSpec((TM, GLA_V), lambda i: (i, P_GG // GLA_V)),
                  pl.BlockSpec((TM, D_CONV), rowblk),
                  pl.BlockSpec((None, TM, GLA_V), lambda i: (0, i, 0)),
                  pl.BlockSpec((None, TM, GLA_V), lambda i: (1, i, 0)),
                  pl.BlockSpec((TM, ATT_Q), rowblk),
                  pl.BlockSpec((None, 16, 3 * D_MODEL), lambda i: (l, 0, 0)),
                  wspec(D_CONV), wspec(GLA_V), wspec(ATT_Q), wspec(D_MODEL),
                  pl.BlockSpec((None, 1, GLA_DV), lambda i: (l, 0, 0))],
        out_specs=pl.BlockSpec((TM, D_MODEL), rowblk),
        out_shape=jax.ShapeDtypeStruct((n_tiles * TM, D_MODEL), F32),
        compiler_params=_cparams(1),
        name="merge_out",
    )(s_all, p, p, ua, o_gla, o_gla, oc, mod, wc, wg, wa, wo, gn)


def _rope_tables():
    t = np.arange(SEQ)
    row = (t // GRID_W).astype(np.float32)
    col = (t % GRID_W).astype(np.float32)
    n_freq = ROPE_AXIS_DIM // 2
    freqs = (np.float32(ROPE_THETA) ** (-np.arange(n_freq, dtype=np.float32) / n_freq)).astype(np.float32)
    ar = row[:, None] * freqs
    ac = col[:, None] * freqs
    cos64 = np.concatenate([np.cos(ar), np.cos(ar), np.cos(ac), np.cos(ac)], axis=1)
    sin64 = np.concatenate([-np.sin(ar), np.sin(ar), -np.sin(ac), np.sin(ac)], axis=1)
    return (np.tile(cos64, (1, ATT_HEADS)).astype(np.float32),
            np.tile(sin64, (1, ATT_HEADS)).astype(np.float32))


def _static_tables():
    bd = np.kron(np.eye(ATT_HEADS, dtype=np.float32), np.ones((ATT_HD, ATT_HD), np.float32))
    idx = np.arange(GLA_CHUNK)
    tri = np.stack([(idx[None, :] <= idx[:, None]), (idx[None, :] >= idx[:, None])]).astype(np.float32)
    selk = np.zeros((ATT_KV_HEADS, 2, 2 * ATT_KV, LANES), np.float32)
    selv = np.zeros((ATT_KV_HEADS, 2, 2 * ATT_KV, LANES), np.float32)
    onerow = np.zeros((2, 1, LANES), np.float32)
    for h in range(ATT_KV_HEADS):
        for e in range(2):
            for dd in range(ATT_HD):
                selk[h, e, h * ATT_HD + dd, e * ATT_HD + dd] = 1.0
                selv[h, e, ATT_KV + h * ATT_HD + dd, e * ATT_HD + dd] = 1.0
    onerow[0, 0, ATT_HD] = 1.0
    onerow[1, 0, 0] = 1.0
    return bd, tri, selk, selv, onerow


def _permute_cols(a):
    pieces = [a[..., O_MA:N_IN], a[..., O_VAL:O_GQ], a[..., O_GQ:O_LR], a[..., O_AQ:O_AK],
              a[..., O_AG:O_MA], a[..., O_AK:O_AG], a[..., O_LR:O_AQ]]
    pad = jnp.zeros(a.shape[:-1] + (NW - N_IN,), a.dtype)
    return jnp.concatenate(pieces + [pad], axis=-1)


def kernel(x, c, ctx, c_ctx, norm_g, w_mod, b_mod, w_in, b_in, conv_dw_w, conv_dw_b, conv_ln_g,
           conv_ln_b, w_conv_out, gla_w_gate, gla_b_gate, gla_norm_g, w_gla_out, q_norm_g,
           k_norm_g, w_attn_out, w_out):
    cos_np, sin_np = _rope_tables()
    bd_np, tri_np, selk_np, selv_np, one_np = _static_tables()
    cos_t, sin_t = jnp.asarray(cos_np), jnp.asarray(sin_np)
    bd = jnp.asarray(bd_np, BF16)
    tri = jnp.asarray(tri_np)
    selk = jnp.asarray(selk_np, BF16)
    selv = jnp.asarray(selv_np, BF16)
    onerow = jnp.asarray(one_np)

    w_p = _permute_cols(w_in).astype(BF16)
    b_p = _permute_cols(b_in).reshape(DEPTH, 1, NW)
    cc = jnp.concatenate([c, c_ctx[None, :], jnp.zeros((16 - BATCH - 1, D_MODEL), F32)], axis=0)
    dw_w = jnp.concatenate([conv_dw_w, jnp.zeros((DEPTH, 32 - CONV_WIDTH, D_CONV), F32)], axis=1)
    r3 = lambda a: a.reshape(DEPTH, 1, a.shape[-1])
    wup = jnp.zeros((DEPTH, 2, LANES, GLA_QK), F32)
    wup = wup.at[:, 0, 0:GLA_RANK].set(gla_w_gate[:, 0])
    wup = wup.at[:, 1, GLA_RANK:2 * GLA_RANK].set(gla_w_gate[:, 1])
    bup = gla_b_gate.reshape(DEPTH, 2, 1, GLA_QK)
    qg = jnp.tile(q_norm_g, (1, ATT_HEADS)).reshape(DEPTH, 1, ATT_Q)
    kg = jnp.tile(k_norm_g, (1, ATT_KV_HEADS)).reshape(DEPTH, 1, ATT_KV)
    wc, wg, wa, wo = (w.astype(BF16) for w in (w_conv_out, w_gla_out, w_attn_out, w_out))

    mod = _modulation(cc, w_mod, b_mod)
    s_all = jnp.concatenate([x.reshape(MX, D_MODEL), ctx.reshape(MC, D_MODEL)], axis=0)
    for l in range(DEPTH):
        p = _inproj(l, s_all, mod, r3(norm_g), w_p, b_p, cos_t, sin_t, qg, kg, bd)
        ua = _conv(l, p, dw_w, r3(conv_dw_b), r3(conv_ln_g), r3(conv_ln_b))
        o_gla = _gla(l, p, wup, bup, tri)
        oc = _attn(p, selk, selv, onerow)
        n_tiles = M_ALL // TM if l < DEPTH - 1 else N_XT
        s_all = _final(l, n_tiles, s_all, p, ua, o_gla, oc, mod, wc, wg, wa, wo, r3(gla_norm_g))
    return s_all.reshape(BATCH, SEQ, D_MODEL)
```

```python
import functools
import math

import numpy as np
import jax
import jax.numpy as jnp
from jax import lax
from jax.experimental import pallas as pl
from jax.experimental.pallas import tpu as pltpu

F32 = jnp.float32
BF16 = jnp.bfloat16
HIGHEST = lax.Precision.HIGHEST

D_MODEL = 1024
BATCH = 8
SEQ = 4096
DEPTH = 4
CTX_LEN = 256
GRID_W = 64
EPS = 1e-6
D_CONV = 512
CONV_WIDTH = 31
CONV_PAD = CONV_WIDTH // 2
GLA_HEADS = 4
GLA_DK = 64
GLA_DV = 128
GLA_QK = GLA_HEADS * GLA_DK
GLA_V = GLA_HEADS * GLA_DV
GLA_RANK = 16
GLA_GATE_NORM = 16.0
GLA_CHUNK = 64
ATT_HEADS = 8
ATT_KV_HEADS = 2
ATT_GROUP = ATT_HEADS // ATT_KV_HEADS
ATT_HD = 64
ATT_Q = ATT_HEADS * ATT_HD
ATT_KV = ATT_KV_HEADS * ATT_HD
ROPE_AXIS_DIM = ATT_HD // 2
ROPE_THETA = 10000.0

LANES = 128
MX = BATCH * SEQ
MC = BATCH * CTX_LEN
M_ALL = MX + MC

O_VAL, O_GLU, O_CGATE = 0, 512, 1024
O_GQ, O_GK, O_GV, O_GG = 1536, 1792, 2048, 2560
O_LR = 3072
O_AQ, O_AK, O_AV, O_AG = 3104, 3616, 3744, 3872
O_MA = 4384
N_IN = 7456

W_M = 0
W_CONV = 3072
W_GLA = 4608
W_AQ = 6144
W_AG = 6656
W_AK = 7168
W_AV = 7296
W_LR = 7424
NW = 7552

P_SIG = 0
P_U = 3072
P_CG = 3584
P_GQK = 4096
P_GV = 4608
P_GG = 5120
P_AQ = 5632
P_AG = 6144
P_AKV = 6656
P_LR = 6912
NP = 7040

TM = 512
N_XT = MX // TM
N_CT = MC // TM
TILES_PER_SEQ = SEQ // TM
TC = 256
XB = MX // TC
BLK_PER_SEQ = SEQ // TC
HALO = 16
CONV_RC = 32
KB = 512
Q_PRESCALE = (ATT_HD ** -0.5) * math.log2(math.e)

VMEM_LIMIT = 56 * 1024 * 1024


def _cparams(n_axes, vmem=VMEM_LIMIT):
    return pltpu.CompilerParams(dimension_semantics=("arbitrary",) * n_axes,
                                vmem_limit_bytes=vmem)


def _silu(x):
    return x * jax.nn.sigmoid(x)


def _mod_kernel(c_ref, w_ref, b_ref, o_ref):
    s = _silu(c_ref[...])
    o_ref[...] = jnp.dot(s, w_ref[...], preferred_element_type=F32, precision=HIGHEST) + b_ref[...]


def _modulation(cc, w_mod, b_mod):
    nt = 3 * D_MODEL // 1024
    return pl.pallas_call(
        _mod_kernel,
        grid=(DEPTH, nt),
        in_specs=[pl.BlockSpec((16, D_MODEL), lambda l, n: (0, 0)),
                  pl.BlockSpec((None, D_MODEL, 1024), lambda l, n: (l, 0, n)),
                  pl.BlockSpec((None, 1, 1024), lambda l, n: (l, 0, n))],
        out_specs=pl.BlockSpec((None, 16, 1024), lambda l, n: (l, 0, n)),
        out_shape=jax.ShapeDtypeStruct((DEPTH, 16, 3 * D_MODEL), F32),
        compiler_params=_cparams(2),
        name="modulation",
    )(cc, w_mod, b_mod.reshape(DEPTH, 1, 3 * D_MODEL))


def _head_norm(xv, gain, bd):
    sq = xv * xv
    hi = sq.astype(BF16)
    lo = (sq - hi.astype(F32)).astype(BF16)
    ss = (jnp.dot(hi, bd, preferred_element_type=F32)
          + jnp.dot(lo, bd, preferred_element_type=F32))
    return xv * lax.rsqrt(ss * (1.0 / ATT_HD) + EPS) * gain


def _rope(xv, cosv, sinv):
    parts = []
    for s in range(xv.shape[1] // LANES):
        sl = slice(s * LANES, (s + 1) * LANES)
        xs = xv[:, sl]
        up = pltpu.roll(xs, LANES - 16, axis=1)
        dn = pltpu.roll(xs, 16, axis=1)
        lane = lax.broadcasted_iota(jnp.int32, xs.shape, 1)
        partner = jnp.where((lane & 16) == 0, up, dn)
        parts.append(xs * cosv[:, sl] + partner * sinv[:, sl])
    return jnp.concatenate(parts, axis=1) if len(parts) > 1 else parts[0]


def _inproj_kernel(x_ref, mod_ref, g_ref, w_ref, b_ref, cos_ref, sin_ref, qg_ref, kg_ref,
                   bd_ref, o_ref):
    i = pl.program_id(0)
    is_ctx = i >= N_XT
    row = jnp.where(is_ctx, BATCH, i // TILES_PER_SEQ)
    x = x_ref[...]
    ms = jnp.mean(x * x, axis=-1, keepdims=True)
    y = x * lax.rsqrt(ms + EPS) * g_ref[...]
    m = mod_ref[pl.ds(row, 1), :]
    shift = m[:, 0:D_MODEL]
    scale = m[:, D_MODEL:2 * D_MODEL]
    h = (y * (1.0 + scale) + shift).astype(BF16)

    def proj(a, n):
        return jnp.dot(h, w_ref[:, a:a + n], preferred_element_type=F32) + b_ref[:, a:a + n]

    def put(a, val):
        o_ref[:, a:a + val.shape[1]] = val.astype(BF16)

    for k in range(3):
        put(P_SIG + 1024 * k, jax.nn.sigmoid(proj(W_M + 1024 * k, 1024)))
    put(P_U, proj(W_CONV, 512) * jax.nn.sigmoid(proj(W_CONV + 512, 512)))
    put(P_CG, _silu(proj(W_CONV + 1024, 512)))
    qk = proj(W_GLA, 512)
    lane = lax.broadcasted_iota(jnp.int32, qk.shape, 1)
    put(P_GQK, jnp.where(lane < GLA_QK, qk * (GLA_DK ** -0.5), qk))
    put(P_GV, proj(W_GLA + 512, 512))
    put(P_GG, _silu(proj(W_GLA + 1024, 512)))

    cosv = jnp.where(is_ctx, 1.0, cos_ref[...])
    sinv = jnp.where(is_ctx, 0.0, sin_ref[...])
    bd = bd_ref[...]
    aq = _head_norm(proj(W_AQ, ATT_Q), qg_ref[...], bd)
    put(P_AQ, _rope(aq, cosv, sinv) * Q_PRESCALE)
    put(P_AG, _silu(proj(W_AG, ATT_Q)))
    akv = proj(W_AK, 2 * ATT_KV)
    ak = _head_norm(akv[:, :ATT_KV], kg_ref[...], bd[:ATT_KV, :ATT_KV])
    put(P_AKV, _rope(ak, cosv[:, :ATT_KV], sinv[:, :ATT_KV]))
    put(P_AKV + ATT_KV, akv[:, ATT_KV:])
    put(P_LR, proj(W_LR, LANES))


def _inproj(l, s_all, mod, norm_g, w_p, b_p, cos_t, sin_t, qg, kg, bd):
    const = lambda i: (0, 0)
    pos = lambda i: (jnp.where(i >= N_XT, 0, i % TILES_PER_SEQ), 0)
    return pl.pallas_call(
        _inproj_kernel,
        grid=(M_ALL // TM,),
        in_specs=[pl.BlockSpec((TM, D_MODEL), lambda i: (i, 0)),
                  pl.BlockSpec((None, 16, 3 * D_MODEL), lambda i: (l, 0, 0)),
                  pl.BlockSpec((None, 1, D_MODEL), lambda i: (l, 0, 0)),
                  pl.BlockSpec((None, D_MODEL, NW), lambda i: (l, 0, 0),
                               pipeline_mode=pl.Buffered(1)),
                  pl.BlockSpec((None, 1, NW), lambda i: (l, 0, 0)),
                  pl.BlockSpec((TM, ATT_Q), pos),
                  pl.BlockSpec((TM, ATT_Q), pos),
                  pl.BlockSpec((None, 1, ATT_Q), lambda i: (l, 0, 0)),
                  pl.BlockSpec((None, 1, ATT_KV), lambda i: (l, 0, 0)),
                  pl.BlockSpec((ATT_Q, ATT_Q), const)],
        out_specs=pl.BlockSpec((TM, NP), lambda i: (i, 0)),
        out_shape=jax.ShapeDtypeStruct((M_ALL, NP), BF16),
        compiler_params=_cparams(1),
        name="inproj",
    )(s_all, mod, norm_g, w_p, b_p, cos_t, sin_t, qg, kg, bd)


def _conv_kernel(u_ref, ul_ref, ur_ref, sg_ref, dw_ref, dwb_ref, lng_ref, lnb_ref, o_ref, ext_ref):
    i = pl.program_id(0)
    j = i % BLK_PER_SEQ
    is_x = i < XB
    left_ok = jnp.logical_and(is_x, j != 0)
    right_ok = jnp.logical_and(is_x, j != BLK_PER_SEQ - 1)
    ext_ref[0:HALO, :] = jnp.where(left_ok, ul_ref[...].astype(F32), 0.0)
    ext_ref[HALO:HALO + TC, :] = u_ref[...].astype(F32)
    ext_ref[HALO + TC:, :] = jnp.where(right_ok, ur_ref[...].astype(F32), 0.0)
    bias = dwb_ref[...]
    lng = lng_ref[...]
    lnb = lnb_ref[...]
    for c in range(TC // CONV_RC):
        r0 = c * CONV_RC
        acc = jnp.broadcast_to(bias, (CONV_RC, D_CONV))
        for t in range(CONV_WIDTH):
            off = r0 + t + HALO - CONV_PAD
            acc = acc + ext_ref[off:off + CONV_RC, :] * dw_ref[t:t + 1, :]
        mu = jnp.mean(acc, axis=-1, keepdims=True)
        d = acc - mu
        var = jnp.mean(d * d, axis=-1, keepdims=True)
        yn = d * lax.rsqrt(var + EPS) * lng + lnb
        o_ref[r0:r0 + CONV_RC, :] = (_silu(yn) * sg_ref[r0:r0 + CONV_RC, :].astype(F32)).astype(BF16)


def _conv(l, p, dw_w, dw_b, ln_g, ln_b):
    nhb = M_ALL // HALO
    per = TC // HALO
    cu = P_U // D_CONV
    cg = P_CG // D_CONV
    vec = lambda i: (l, 0, 0)
    return pl.pallas_call(
        _conv_kernel,
        grid=(M_ALL // TC,),
        in_specs=[pl.BlockSpec((TC, D_CONV), lambda i: (i, cu)),
                  pl.BlockSpec((HALO, D_CONV), lambda i: (jnp.maximum(i * per - 1, 0), cu)),
                  pl.BlockSpec((HALO, D_CONV), lambda i: (jnp.minimum((i + 1) * per, nhb - 1), cu)),
                  pl.BlockSpec((TC, D_CONV), lambda i: (i, cg)),
                  pl.BlockSpec((None, 32, D_CONV), vec),
                  pl.BlockSpec((None, 1, D_CONV), vec),
                  pl.BlockSpec((None, 1, D_CONV), vec),
                  pl.BlockSpec((None, 1, D_CONV), vec)],
        out_specs=pl.BlockSpec((TC, D_CONV), lambda i: (i, 0)),
        out_shape=jax.ShapeDtypeStruct((M_ALL, D_CONV), BF16),
        scratch_shapes=[pltpu.VMEM((TC + 2 * HALO, D_CONV), F32)],
        compiler_params=_cparams(1),
        name="conv",
    )(p, p, p, p, dw_w, dw_b, ln_g, ln_b)


def _gla_kernel(qk_ref, v_ref, lr_ref, wup_ref, bup_ref, tri_ref, o_ref, s_ref):
    d = pl.program_id(1)
    step = pl.program_id(2)

    @pl.when(step == 0)
    def _():
        s_ref[...] = jnp.zeros_like(s_ref)

    tri = tri_ref[...]
    tri4 = jnp.concatenate([tri] * GLA_HEADS, axis=0) > 0.5
    lane_h = lax.broadcasted_iota(jnp.int32, (GLA_CHUNK, GLA_QK), 1) // GLA_DK
    srow_h = lax.broadcasted_iota(jnp.int32, (GLA_QK, GLA_V), 0) // GLA_DK
    scol_h = lax.broadcasted_iota(jnp.int32, (GLA_QK, GLA_V), 1) // GLA_DV
    smask = srow_h == scol_h
    ones_cl = jnp.ones((GLA_CHUNK, LANES), F32)
    wup = wup_ref[...]
    bup = bup_ref[...]
    n_chunks = TC // GLA_CHUNK
    for n in range(n_chunks):
        c = jnp.where(d == 0, n, n_chunks - 1 - n)
        rows = pl.ds(pl.multiple_of(c * GLA_CHUNK, GLA_CHUNK), GLA_CHUNK)
        qk = qk_ref[rows, :].astype(F32)
        q = qk[:, :GLA_QK]
        k = qk[:, GLA_QK:]
        v = v_ref[rows, :]
        z = jnp.dot(lr_ref[rows, :].astype(F32), wup, preferred_element_type=F32,
                    precision=HIGHEST) + bup
        la = (jnp.minimum(z, 0.0) - jnp.log(1.0 + jnp.exp(-jnp.abs(z)))) * (1.0 / GLA_GATE_NORM)
        cum = jnp.dot(tri, la, preferred_element_type=F32, precision=HIGHEST)
        total = jnp.where(d == 0, cum[GLA_CHUNK - 1:GLA_CHUNK, :], cum[0:1, :])
        q_in = q * jnp.exp(cum)
        k_in = (k * jnp.exp(-cum)).astype(BF16)
        k_st = (k * jnp.exp(total - cum)).astype(BF16)
        q_in_b = q_in.astype(BF16)
        tot_col = lax.dot_general(la, ones_cl, (((0,), (0,)), ((), ())),
                                  preferred_element_type=F32, precision=HIGHEST)
        dcol = jnp.exp(tot_col)
        dcol = jnp.concatenate([dcol] * (GLA_V // LANES), axis=1)
        q_stack = jnp.concatenate(
            [jnp.where(lane_h == h, q_in, 0.0).astype(BF16) for h in range(GLA_HEADS)], axis=0)
        att = lax.dot_general(q_stack, k_in, (((1,), (1,)), ((), ())),
                              preferred_element_type=F32)
        att = jnp.where(tri4, att, 0.0).astype(BF16)
        s_old = s_ref[...]
        o_inter = jnp.dot(q_in_b, s_old.astype(BF16), preferred_element_type=F32)
        o_intra = jnp.concatenate(
            [jnp.dot(att[h * GLA_CHUNK:(h + 1) * GLA_CHUNK, :],
                     v[:, h * GLA_DV:(h + 1) * GLA_DV], preferred_element_type=F32)
             for h in range(GLA_HEADS)], axis=1)
        o_ref[rows, :] = (o_intra + o_inter).astype(BF16)
        kv = lax.dot_general(k_st, v, (((0,), (0,)), ((), ())), preferred_element_type=F32)
        s_ref[...] = dcol * s_old + jnp.where(smask, kv, 0.0)


def _gla(l, p, wup, bup, tri):
    def rb(b, d, s):
        xblk = b * BLK_PER_SEQ + jnp.where(d == 0, s - 1, BLK_PER_SEQ - s)
        return jnp.where(s == 0, XB + b, xblk)

    cqk = P_GQK // 512
    cv = P_GV // 512
    clr = P_LR // LANES
    return pl.pallas_call(
        _gla_kernel,
        grid=(BATCH, 2, BLK_PER_SEQ + 1),
        in_specs=[pl.BlockSpec((TC, 512), lambda b, d, s: (rb(b, d, s), cqk)),
                  pl.BlockSpec((TC, GLA_V), lambda b, d, s: (rb(b, d, s), cv)),
                  pl.BlockSpec((TC, LANES), lambda b, d, s: (rb(b, d, s), clr)),
                  pl.BlockSpec((None, None, LANES, GLA_QK), lambda b, d, s: (l, d, 0, 0)),
                  pl.BlockSpec((None, None, 1, GLA_QK), lambda b, d, s: (l, d, 0, 0)),
                  pl.BlockSpec((None, GLA_CHUNK, GLA_CHUNK), lambda b, d, s: (d, 0, 0))],
        out_specs=pl.BlockSpec((None, TC, GLA_V), lambda b, d, s: (d, rb(b, d, s), 0)),
        out_shape=jax.ShapeDtypeStruct((2, M_ALL, GLA_V), BF16),
        scratch_shapes=[pltpu.VMEM((GLA_QK, GLA_V), F32)],
        compiler_params=_cparams(3),
        name="gla",
    )(p, p, p, wup, bup, tri)


NKEYS = CTX_LEN + SEQ


def _attn_kernel(q_ref, g_ref, ckv_ref, xkv_ref, selk_ref, selv_ref, one_ref, o_ref,
                 kz_ref, vx_ref, m_ref, acc_ref, al_ref, s_ref):
    qi = pl.program_id(1)

    @pl.when(qi == 0)
    def _():
        for h in range(ATT_KV_HEADS):
            for e in range(2):
                sk = selk_ref[h, e]
                sv = selv_ref[h, e]
                ones_row = one_ref[e]
                for (src, r0, n) in ((ckv_ref, 0, CTX_LEN), (xkv_ref, CTX_LEN, SEQ)):
                    for r in range(0, n, 1024):
                        nr = min(1024, n - r)
                        blk = src[r:r + nr, :]
                        kz_ref[h, e, r0 + r:r0 + r + nr, :] = jnp.dot(
                            blk, sk, preferred_element_type=F32).astype(BF16)
                        vx_ref[h, e, r0 + r:r0 + r + nr, :] = (jnp.dot(
                            blk, sv, preferred_element_type=F32) + ones_row).astype(BF16)

    m_ref[...] = jnp.full_like(m_ref, -jnp.inf)
    acc_ref[...] = jnp.zeros_like(acc_ref)

    def block(k0, nk):
        for pr in range(ATT_HEADS // 2):
            h = pr // (ATT_GROUP // 2)
            qp = q_ref[:, pr * LANES:(pr + 1) * LANES]
            for e in range(2):
                kz = kz_ref[h, e, pl.ds(k0, nk), :]
                s_ref[2 * pr + e, :, 0:nk] = lax.dot_general(
                    qp, kz, (((1,), (1,)), ((), ())), preferred_element_type=F32)
        for hd in range(ATT_HEADS):
            m_prev = m_ref[hd]
            m_next = jnp.maximum(m_prev, jnp.max(s_ref[hd, :, 0:nk], axis=-1, keepdims=True))
            al_ref[hd] = jnp.exp2(m_prev - m_next)
            m_ref[hd] = m_next
        for hd in range(ATT_HEADS):
            h, e = hd // ATT_GROUP, hd % 2
            m_rep = jnp.concatenate([m_ref[hd]] * (nk // LANES), axis=1)
            p = jnp.exp2(s_ref[hd, :, 0:nk] - m_rep).astype(BF16)
            vx = vx_ref[h, e, pl.ds(k0, nk), :]
            acc_ref[hd] = al_ref[hd] * acc_ref[hd] + jnp.dot(p, vx, preferred_element_type=F32)

    block(0, CTX_LEN)

    @pl.when(qi > 0)
    def _():
        def body(t, carry):
            block(pl.multiple_of(CTX_LEN + t * KB, KB // 2), KB)
            return carry
        lax.fori_loop(0, SEQ // KB, body, 0)

    lane = lax.broadcasted_iota(jnp.int32, (TC, LANES), 1)
    for pr in range(ATT_HEADS // 2):
        a0 = acc_ref[2 * pr]
        a1 = acc_ref[2 * pr + 1]
        o0 = a0 / a0[:, ATT_HD:ATT_HD + 1]
        o1 = a1 / a1[:, 0:1]
        slab = jnp.where(lane < ATT_HD, o0, o1)
        sl = slice(pr * LANES, (pr + 1) * LANES)
        o_ref[:, sl] = (slab * g_ref[:, sl].astype(F32)).astype(BF16)


def _attn(p, selk, selv, onerow):
    cq = P_AQ // ATT_Q
    cg = P_AG // ATT_Q
    ckv = P_AKV // (2 * ATT_KV)

    def qrow(b, qi):
        return jnp.where(qi == 0, XB + b, b * BLK_PER_SEQ + qi - 1)

    return pl.pallas_call(
        _attn_kernel,
        grid=(BATCH, BLK_PER_SEQ + 1),
        in_specs=[pl.BlockSpec((TC, ATT_Q), lambda b, qi: (qrow(b, qi), cq)),
                  pl.BlockSpec((TC, ATT_Q), lambda b, qi: (qrow(b, qi), cg)),
                  pl.BlockSpec((CTX_LEN, 2 * ATT_KV), lambda b, qi: (XB + b, ckv)),
                  pl.BlockSpec((SEQ, 2 * ATT_KV), lambda b, qi: (b, ckv)),
                  pl.BlockSpec((ATT_KV_HEADS, 2, 2 * ATT_KV, LANES), lambda b, qi: (0, 0, 0, 0)),
                  pl.BlockSpec((ATT_KV_HEADS, 2, 2 * ATT_KV, LANES), lambda b, qi: (0, 0, 0, 0)),
                  pl.BlockSpec((2, 1, LANES), lambda b, qi: (0, 0, 0))],
        out_specs=pl.BlockSpec((TC, ATT_Q), lambda b, qi: (qrow(b, qi), 0)),
        out_shape=jax.ShapeDtypeStruct((M_ALL, ATT_Q), BF16),
        scratch_shapes=[pltpu.VMEM((ATT_KV_HEADS, 2, NKEYS, LANES), BF16),
                        pltpu.VMEM((ATT_KV_HEADS, 2, NKEYS, LANES), BF16),
                        pltpu.VMEM((ATT_HEADS, TC, LANES), F32),
                        pltpu.VMEM((ATT_HEADS, TC, LANES), F32),
                        pltpu.VMEM((ATT_HEADS, TC, LANES), F32),
                        pltpu.VMEM((ATT_HEADS, TC, KB), F32)],
        compiler_params=_cparams(2),
        name="attention",
    )(p, p, p, p, selk, selv, onerow)


def _final_kernel(s_ref, sig_ref, gg_ref, ua_ref, of_ref, ob_ref, oc_ref, mod_ref,
                  wc_ref, wg_ref, wa_ref, wo_ref, gn_ref, o_ref):
    i = pl.program_id(0)
    row = jnp.where(i >= N_XT, BATCH, i // TILES_PER_SEQ)
    ya = jnp.dot(ua_ref[...], wc_ref[...], preferred_element_type=F32)
    og = of_ref[...].astype(F32) + ob_ref[...].astype(F32)
    gn = gn_ref[...]
    parts = []
    for h in range(GLA_HEADS):
        oh = og[:, h * GLA_DV:(h + 1) * GLA_DV]
        ms = jnp.mean(oh * oh, axis=-1, keepdims=True)
        parts.append(oh * lax.rsqrt(ms + EPS) * gn)
    on = jnp.concatenate(parts, axis=1) * gg_ref[...].astype(F32)
    yb = jnp.dot(on.astype(BF16), wg_ref[...], preferred_element_type=F32)
    yc = jnp.dot(oc_ref[...], wa_ref[...], preferred_element_type=F32)
    merged = (sig_ref[:, 0:D_MODEL].astype(F32) * ya
              + sig_ref[:, D_MODEL:2 * D_MODEL].astype(F32) * yb
              + sig_ref[:, 2 * D_MODEL:3 * D_MODEL].astype(F32) * yc)
    out = jnp.dot(merged.astype(BF16), wo_ref[...], preferred_element_type=F32)
    gate = mod_ref[pl.ds(row, 1), 2 * D_MODEL:3 * D_MODEL]
    o_ref[...] = s_ref[...] + gate * out


def _final(l, n_tiles, s_all, p, ua, o_gla, oc, mod, wc, wg, wa, wo, gn):
    rowblk = lambda i: (i, 0)
    wspec = lambda k: pl.BlockSpec((None, k, D_MODEL), lambda i: (l, 0, 0))
    return pl.pallas_call(
        _final_kernel,
        grid=(n_tiles,),
        in_specs=[pl.BlockSpec((TM, D_MODEL), rowblk),
                  pl.BlockSpec((TM, 3 * D_MODEL), lambda i: (i, P_SIG // (3 * D_MODEL))),
                  pl.BlockSpec((TM, GLA_V), lambda i: (i, P_GG // GLA_V)),
                  pl.BlockSpec((TM, D_CONV), rowblk),
                  pl.BlockSpec((None, TM, GLA_V), lambda i: (0, i, 0)),
                  pl.BlockSpec((None, TM, GLA_V), lambda i: (1, i, 0)),
                  pl.BlockSpec((TM, ATT_Q), rowblk),
                  pl.BlockSpec((None, 16, 3 * D_MODEL), lambda i: (l, 0, 0)),
                  wspec(D_CONV), wspec(GLA_V), wspec(ATT_Q), wspec(D_MODEL),
                  pl.BlockSpec((None, 1, GLA_DV), lambda i: (l, 0, 0))],
        out_specs=pl.BlockSpec((TM, D_MODEL), rowblk),
        out_shape=jax.ShapeDtypeStruct((n_tiles * TM, D_MODEL), F32),
        compiler_params=_cparams(1),
        name="merge_out",
    )(s_all, p, p, ua, o_gla, o_gla, oc, mod, wc, wg, wa, wo, gn)


def _rope_tables():
    t = np.arange(SEQ)
    row = (t // GRID_W).astype(np.float32)
    col = (t % GRID_W).astype(np.float32)
    n_freq = ROPE_AXIS_DIM // 2
    freqs = (np.float32(ROPE_THETA) ** (-np.arange(n_freq, dtype=np.float32) / n_freq)).astype(np.float32)
    ar = row[:, None] * freqs
    ac = col[:, None] * freqs
    cos64 = np.concatenate([np.cos(ar), np.cos(ar), np.cos(ac), np.cos(ac)], axis=1)
    sin64 = np.concatenate([-np.sin(ar), np.sin(ar), -np.sin(ac), np.sin(ac)], axis=1)
    return (np.tile(cos64, (1, ATT_HEADS)).astype(np.float32),
            np.tile(sin64, (1, ATT_HEADS)).astype(np.float32))


def _static_tables():
    bd = np.kron(np.eye(ATT_HEADS, dtype=np.float32), np.ones((ATT_HD, ATT_HD), np.float32))
    idx = np.arange(GLA_CHUNK)
    tri = np.stack([(idx[None, :] <= idx[:, None]), (idx[None, :] >= idx[:, None])]).astype(np.float32)
    selk = np.zeros((ATT_KV_HEADS, 2, 2 * ATT_KV, LANES), np.float32)
    selv = np.zeros((ATT_KV_HEADS, 2, 2 * ATT_KV, LANES), np.float32)
    onerow = np.zeros((2, 1, LANES), np.float32)
    for h in range(ATT_KV_HEADS):
        for e in range(2):
            for dd in range(ATT_HD):
                selk[h, e, h * ATT_HD + dd, e * ATT_HD + dd] = 1.0
                selv[h, e, ATT_KV + h * ATT_HD + dd, e * ATT_HD + dd] = 1.0
    onerow[0, 0, ATT_HD] = 1.0
    onerow[1, 0, 0] = 1.0
    return bd, tri, selk, selv, onerow


def _permute_cols(a):
    pieces = [a[..., O_MA:N_IN], a[..., O_VAL:O_GQ], a[..., O_GQ:O_LR], a[..., O_AQ:O_AK],
              a[..., O_AG:O_MA], a[..., O_AK:O_AG], a[..., O_LR:O_AQ]]
    pad = jnp.zeros(a.shape[:-1] + (NW - N_IN,), a.dtype)
    return jnp.concatenate(pieces + [pad], axis=-1)


def kernel(x, c, ctx, c_ctx, norm_g, w_mod, b_mod, w_in, b_in, conv_dw_w, conv_dw_b, conv_ln_g,
           conv_ln_b, w_conv_out, gla_w_gate, gla_b_gate, gla_norm_g, w_gla_out, q_norm_g,
           k_norm_g, w_attn_out, w_out):
    cos_np, sin_np = _rope_tables()
    bd_np, tri_np, selk_np, selv_np, one_np = _static_tables()
    cos_t, sin_t = jnp.asarray(cos_np), jnp.asarray(sin_np)
    bd = jnp.asarray(bd_np, BF16)
    tri = jnp.asarray(tri_np)
    selk = jnp.asarray(selk_np, BF16)
    selv = jnp.asarray(selv_np, BF16)
    onerow = jnp.asarray(one_np)

    w_p = _permute_cols(w_in).astype(BF16)
    b_p = _permute_cols(b_in).reshape(DEPTH, 1, NW)
    cc = jnp.concatenate([c, c_ctx[None, :], jnp.zeros((16 - BATCH - 1, D_MODEL), F32)], axis=0)
    dw_w = jnp.concatenate([conv_dw_w, jnp.zeros((DEPTH, 32 - CONV_WIDTH, D_CONV), F32)], axis=1)
    r3 = lambda a: a.reshape(DEPTH, 1, a.shape[-1])
    wup = jnp.zeros((DEPTH, 2, LANES, GLA_QK), F32)
    wup = wup.at[:, 0, 0:GLA_RANK].set(gla_w_gate[:, 0])
    wup = wup.at[:, 1, GLA_RANK:2 * GLA_RANK].set(gla_w_gate[:, 1])
    bup = gla_b_gate.reshape(DEPTH, 2, 1, GLA_QK)
    qg = jnp.tile(q_norm_g, (1, ATT_HEADS)).reshape(DEPTH, 1, ATT_Q)
    kg = jnp.tile(k_norm_g, (1, ATT_KV_HEADS)).reshape(DEPTH, 1, ATT_KV)
    wc, wg, wa, wo = (w.astype(BF16) for w in (w_conv_out, w_gla_out, w_attn_out, w_out))

    mod = _modulation(cc, w_mod, b_mod)
    s_all = jnp.concatenate([x.reshape(MX, D_MODEL), ctx.reshape(MC, D_MODEL)], axis=0)
    for l in range(DEPTH):
        p = _inproj(l, s_all, mod, r3(norm_g), w_p, b_p, cos_t, sin_t, qg, kg, bd)
        ua = _conv(l, p, dw_w, r3(conv_dw_b), r3(conv_ln_g), r3(conv_ln_b))
        o_gla = _gla(l, p, wup, bup, tri)
        oc = _attn(p, selk, selv, onerow)
        n_tiles = M_ALL // TM if l < DEPTH - 1 else N_XT
        s_all = _final(l, n_tiles, s_all, p, ua, o_gla, oc, mod, wc, wg, wa, wo, r3(gla_norm_g))
    return s_all.reshape(BATCH, SEQ, D_MODEL)
```

```python
import functools
import math

import numpy as np
import jax
import jax.numpy as jnp
from jax import lax
from jax.experimental import pallas as pl
from jax.experimental.pallas import tpu as pltpu

F32 = jnp.float32
BF16 = jnp.bfloat16
HIGHEST = lax.Precision.HIGHEST

D_MODEL = 1024
BATCH = 8
SEQ = 4096
DEPTH = 4
CTX_LEN = 256
GRID_W = 64
EPS = 1e-6
D_CONV = 512
CONV_WIDTH = 31
CONV_PAD = CONV_WIDTH // 2
GLA_HEADS = 4
GLA_DK = 64
GLA_DV = 128
GLA_QK = GLA_HEADS * GLA_DK
GLA_V = GLA_HEADS * GLA_DV
GLA_RANK = 16
GLA_GATE_NORM = 16.0
GLA_CHUNK = 64
ATT_HEADS = 8
ATT_KV_HEADS = 2
ATT_GROUP = ATT_HEADS // ATT_KV_HEADS
ATT_HD = 64
ATT_Q = ATT_HEADS * ATT_HD
ATT_KV = ATT_KV_HEADS * ATT_HD
ROPE_AXIS_DIM = ATT_HD // 2
ROPE_THETA = 10000.0

LANES = 128
SUBLANES = 8
MX = BATCH * SEQ
MC = BATCH * CTX_LEN
M_ALL = MX + MC

O_VAL, O_GLU, O_CGATE = 0, 512, 1024
O_GQ, O_GK, O_GV, O_GG = 1536, 1792, 2048, 2560
O_LR = 3072
O_AQ, O_AK, O_AV, O_AG = 3104, 3616, 3744, 3872
O_MA = 4384
N_IN = 7456

W_M = 0
W_CONV = 3072
W_GLA = 4608
W_AQ = 6144
W_AG = 6656
W_AK = 7168
W_AV = 7296
W_LR = 7424
NW = 7552

P_SIG = 0
P_U = 3072
P_CG = 3584
P_GV = 4096
P_GD = 4608
GD_W = 3 * GLA_QK
P_GG = 6144
P_AQ = 6656
P_AG = 7168
P_AKV = 7680
NP = 7936

TM = 512
N_XT = MX // TM
N_CT = MC // TM
TILES_PER_SEQ = SEQ // TM
TC = 256
XB = MX // TC
BLK_PER_SEQ = SEQ // TC
HALO = 16
CONV_RC = 32
CONV_SH = TC // 2 + 2 * HALO
KB = 512
Q_PRESCALE = (ATT_HD ** -0.5) * math.log2(math.e)

VMEM_LIMIT = 56 * 1024 * 1024


def _cparams(n_axes, vmem=VMEM_LIMIT):
    return pltpu.CompilerParams(dimension_semantics=("arbitrary",) * n_axes,
                                vmem_limit_bytes=vmem)


def _silu(x):
    return x * jax.nn.sigmoid(x)


def _mod_kernel(c_ref, w_ref, b_ref, o_ref):
    s = _silu(c_ref[...])
    o_ref[...] = jnp.dot(s, w_ref[...], preferred_element_type=F32, precision=HIGHEST) + b_ref[...]


def _modulation(cc, w_mod, b_mod):
    nt = 3 * D_MODEL // 1024
    return pl.pallas_call(
        _mod_kernel,
        grid=(DEPTH, nt),
        in_specs=[pl.BlockSpec((16, D_MODEL), lambda l, n: (0, 0)),
                  pl.BlockSpec((None, D_MODEL, 1024), lambda l, n: (l, 0, n)),
                  pl.BlockSpec((None, 1, 1024), lambda l, n: (l, 0, n))],
        out_specs=pl.BlockSpec((None, 16, 1024), lambda l, n: (l, 0, n)),
        out_shape=jax.ShapeDtypeStruct((DEPTH, 16, 3 * D_MODEL), F32),
        compiler_params=_cparams(2),
        name="modulation",
    )(cc, w_mod, b_mod.reshape(DEPTH, 1, 3 * D_MODEL))


def _head_norm(xv, gain, bd):
    sq = xv * xv
    hi = sq.astype(BF16)
    lo = (sq - hi.astype(F32)).astype(BF16)
    ss = (jnp.dot(hi, bd, preferred_element_type=F32)
          + jnp.dot(lo, bd, preferred_element_type=F32))
    return xv * lax.rsqrt(ss * (1.0 / ATT_HD) + EPS) * gain


def _rope(xv, cosv, sinv):
    parts = []
    for s in range(xv.shape[1] // LANES):
        sl = slice(s * LANES, (s + 1) * LANES)
        xs = xv[:, sl]
        up = pltpu.roll(xs, LANES - 16, axis=1)
        dn = pltpu.roll(xs, 16, axis=1)
        lane = lax.broadcasted_iota(jnp.int32, xs.shape, 1)
        partner = jnp.where((lane & 16) == 0, up, dn)
        parts.append(xs * cosv[:, sl] + partner * sinv[:, sl])
    return jnp.concatenate(parts, axis=1) if len(parts) > 1 else parts[0]


def _inproj_kernel(x_ref, mod_ref, g_ref, w_ref, b_ref, cos_ref, sin_ref, qg_ref, kg_ref,
                   bd_ref, wuh_ref, wul_ref, bup_ref, o_ref, dec_ref):
    i = pl.program_id(0)
    is_ctx = i >= N_XT
    row = jnp.where(is_ctx, BATCH, i // TILES_PER_SEQ)
    x = x_ref[...]
    ms = jnp.mean(x * x, axis=-1, keepdims=True)
    y = x * lax.rsqrt(ms + EPS) * g_ref[...]
    m = mod_ref[pl.ds(row, 1), :]
    shift = m[:, 0:D_MODEL]
    scale = m[:, D_MODEL:2 * D_MODEL]
    h = (y * (1.0 + scale) + shift).astype(BF16)

    def proj(a, n):
        return jnp.dot(h, w_ref[:, a:a + n], preferred_element_type=F32) + b_ref[:, a:a + n]

    def put(a, val):
        o_ref[:, a:a + val.shape[1]] = val.astype(BF16)

    for k in range(3):
        put(P_SIG + 1024 * k, jax.nn.sigmoid(proj(W_M + 1024 * k, 1024)))
    put(P_U, proj(W_CONV, 512) * jax.nn.sigmoid(proj(W_CONV + 512, 512)))
    put(P_CG, _silu(proj(W_CONV + 1024, 512)))
    put(P_GV, proj(W_GLA + 512, 512))
    put(P_GG, _silu(proj(W_GLA + 1024, 512)))

    lr = proj(W_LR, LANES)
    lr_hi = lr.astype(BF16)
    lr_lo = (lr - lr_hi.astype(F32)).astype(BF16)
    wuh = wuh_ref[...]
    z = (jnp.dot(lr_hi, wuh, preferred_element_type=F32)
         + jnp.dot(lr_lo, wuh, preferred_element_type=F32)
         + jnp.dot(lr_hi, wul_ref[...], preferred_element_type=F32)) + bup_ref[...]
    la = (jnp.minimum(z, 0.0) - jnp.log(1.0 + jnp.exp(-jnp.abs(z)))) * (1.0 / GLA_GATE_NORM)
    rowc = lax.broadcasted_iota(jnp.int32, (TM, GLA_QK), 0) & (GLA_CHUNK - 1)
    cf = la[:, :GLA_QK]
    cb = la[:, GLA_QK:]
    sh = 1
    while sh < GLA_CHUNK:
        cf = cf + jnp.where(rowc >= sh, pltpu.roll(cf, sh, axis=0), 0.0)
        cb = cb + jnp.where(rowc < GLA_CHUNK - sh, pltpu.roll(cb, TM - sh, axis=0), 0.0)
        sh *= 2
    n_ch = TM // GLA_CHUNK
    last_f = [cf[c * GLA_CHUNK + GLA_CHUNK - 1:(c + 1) * GLA_CHUNK, :] for c in range(n_ch)]
    last_b = [cb[c * GLA_CHUNK:c * GLA_CHUNK + 1, :] for c in range(n_ch)]
    dec_ref[0] = jnp.concatenate(last_f, axis=0)
    dec_ref[1] = jnp.concatenate(last_b, axis=0)
    qk = proj(W_GLA, 512)
    gq = qk[:, :GLA_QK] * (GLA_DK ** -0.5)
    gk = qk[:, GLA_QK:]
    for dd, (cum, last) in enumerate(((cf, last_f), (cb, last_b))):
        tot = jnp.concatenate([jnp.broadcast_to(t, (GLA_CHUNK, GLA_QK)) for t in last], axis=0)
        base = P_GD + dd * GD_W
        put(base, gq * jnp.exp(cum))
        put(base + GLA_QK, gk * jnp.exp(-cum))
        put(base + 2 * GLA_QK, gk * jnp.exp(tot - cum))

    cosv = jnp.where(is_ctx, 1.0, cos_ref[...])
    sinv = jnp.where(is_ctx, 0.0, sin_ref[...])
    bd = bd_ref[...]
    aq = _head_norm(proj(W_AQ, ATT_Q), qg_ref[...], bd)
    put(P_AQ, _rope(aq, cosv, sinv) * Q_PRESCALE)
    put(P_AG, _silu(proj(W_AG, ATT_Q)))
    akv = proj(W_AK, 2 * ATT_KV)
    ak = _head_norm(akv[:, :ATT_KV], kg_ref[...], bd[:ATT_KV, :ATT_KV])
    put(P_AKV, _rope(ak, cosv[:, :ATT_KV], sinv[:, :ATT_KV]))
    put(P_AKV + ATT_KV, akv[:, ATT_KV:])


def _inproj(l, s_all, mod, norm_g, w_p, b_p, cos_t, sin_t, qg, kg, bd, wuh, wul, bup):
    const = lambda i: (0, 0)
    pos = lambda i: (jnp.where(i >= N_XT, 0, i % TILES_PER_SEQ), 0)
    return pl.pallas_call(
        _inproj_kernel,
        grid=(M_ALL // TM,),
        in_specs=[pl.BlockSpec((TM, D_MODEL), lambda i: (i, 0)),
                  pl.BlockSpec((None, 16, 3 * D_MODEL), lambda i: (l, 0, 0)),
                  pl.BlockSpec((None, 1, D_MODEL), lambda i: (l, 0, 0)),
                  pl.BlockSpec((None, D_MODEL, NW), lambda i: (l, 0, 0),
                               pipeline_mode=pl.Buffered(1)),
                  pl.BlockSpec((None, 1, NW), lambda i: (l, 0, 0)),
                  pl.BlockSpec((TM, ATT_Q), pos),
                  pl.BlockSpec((TM, ATT_Q), pos),
                  pl.BlockSpec((None, 1, ATT_Q), lambda i: (l, 0, 0)),
                  pl.BlockSpec((None, 1, ATT_KV), lambda i: (l, 0, 0)),
                  pl.BlockSpec((ATT_Q, ATT_Q), const),
                  pl.BlockSpec((None, LANES, 2 * GLA_QK), lambda i: (l, 0, 0)),
                  pl.BlockSpec((None, LANES, 2 * GLA_QK), lambda i: (l, 0, 0)),
                  pl.BlockSpec((None, 1, 2 * GLA_QK), lambda i: (l, 0, 0))],
        out_specs=[pl.BlockSpec((TM, NP), lambda i: (i, 0)),
                   pl.BlockSpec((2, TM // GLA_CHUNK, GLA_QK), lambda i: (0, i, 0))],
        out_shape=[jax.ShapeDtypeStruct((M_ALL, NP), BF16),
                   jax.ShapeDtypeStruct((2, M_ALL // GLA_CHUNK, GLA_QK), F32)],
        compiler_params=_cparams(1),
        name="inproj",
    )(s_all, mod, norm_g, w_p, b_p, cos_t, sin_t, qg, kg, bd, wuh, wul, bup)


def _conv_kernel(u_ref, ul_ref, ur_ref, sg_ref, dw_ref, dwb_ref, lng_ref, lnb_ref, sh_ref,
                 o_ref, win_ref, ext_ref):
    i = pl.program_id(0)
    j = i % BLK_PER_SEQ
    is_x = i < XB
    left_ok = jnp.logical_and(is_x, j != 0)
    right_ok = jnp.logical_and(is_x, j != BLK_PER_SEQ - 1)
    zero_h = jnp.zeros((HALO, D_CONV), BF16)
    win_ref[0:HALO, :] = jnp.where(left_ok, ul_ref[...], zero_h)
    win_ref[HALO:HALO + TC, :] = u_ref[...]
    win_ref[HALO + TC:, :] = jnp.where(right_ok, ur_ref[...], zero_h)
    ext_ref[0] = win_ref[...].astype(F32)
    half = TC // 2
    for r in range(1, SUBLANES):
        for a in range(2):
            ext_ref[r, a * half:a * half + CONV_SH, :] = jnp.dot(
                sh_ref[r], win_ref[a * half:a * half + CONV_SH, :], preferred_element_type=F32)
    bias = dwb_ref[...]
    lng = lng_ref[...]
    lnb = lnb_ref[...]
    for c in range(TC // CONV_RC):
        r0 = c * CONV_RC
        acc = jnp.broadcast_to(bias, (CONV_RC, D_CONV))
        for t in range(CONV_WIDTH):
            off = t + HALO - CONV_PAD
            a0 = r0 + (off // SUBLANES) * SUBLANES
            w_t = jnp.concatenate([dw_ref[t]] * (CONV_RC // SUBLANES), axis=0)
            acc = acc + ext_ref[off % SUBLANES, a0:a0 + CONV_RC, :] * w_t
        mu = jnp.mean(acc, axis=-1, keepdims=True)
        d = acc - mu
        var = jnp.mean(d * d, axis=-1, keepdims=True)
        yn = d * lax.rsqrt(var + EPS) * lng + lnb
        o_ref[r0:r0 + CONV_RC, :] = (_silu(yn) * sg_ref[r0:r0 + CONV_RC, :].astype(F32)).astype(BF16)


def _conv(l, p, dw_w, dw_b, ln_g, ln_b, shifts):
    nhb = M_ALL // HALO
    per = TC // HALO
    cu = P_U // D_CONV
    cg = P_CG // D_CONV
    vec = lambda i: (l, 0, 0)
    return pl.pallas_call(
        _conv_kernel,
        grid=(M_ALL // TC,),
        in_specs=[pl.BlockSpec((TC, D_CONV), lambda i: (i, cu)),
                  pl.BlockSpec((HALO, D_CONV), lambda i: (jnp.maximum(i * per - 1, 0), cu)),
                  pl.BlockSpec((HALO, D_CONV), lambda i: (jnp.minimum((i + 1) * per, nhb - 1), cu)),
                  pl.BlockSpec((TC, D_CONV), lambda i: (i, cg)),
                  pl.BlockSpec((None, CONV_WIDTH, SUBLANES, D_CONV), lambda i: (l, 0, 0, 0)),
                  pl.BlockSpec((None, 1, D_CONV), vec),
                  pl.BlockSpec((None, 1, D_CONV), vec),
                  pl.BlockSpec((None, 1, D_CONV), vec),
                  pl.BlockSpec((SUBLANES, CONV_SH, CONV_SH), lambda i: (0, 0, 0))],
        out_specs=pl.BlockSpec((TC, D_CONV), lambda i: (i, 0)),
        out_shape=jax.ShapeDtypeStruct((M_ALL, D_CONV), BF16),
        scratch_shapes=[pltpu.VMEM((TC + 2 * HALO, D_CONV), BF16),
                        pltpu.VMEM((SUBLANES, TC + 2 * HALO, D_CONV), F32)],
        compiler_params=_cparams(1),
        name="conv",
    )(p, p, p, p, dw_w, dw_b, ln_g, ln_b, shifts)


def _gla_kernel(qkk_ref, v_ref, dec_ref, tri_ref, o_ref, st_ref):
    d = pl.program_id(1)
    step = pl.program_id(2)

    @pl.when(step == 0)
    def _():
        st_ref[...] = jnp.zeros_like(st_ref)

    tri4 = jnp.concatenate([tri_ref[...]] * GLA_HEADS, axis=0) > 0.5
    lane_h = lax.broadcasted_iota(jnp.int32, (GLA_CHUNK, GLA_QK), 1) // GLA_DK
    srow_h = lax.broadcasted_iota(jnp.int32, (GLA_V, GLA_QK), 0) // GLA_DV
    scol_h = lax.broadcasted_iota(jnp.int32, (GLA_V, GLA_QK), 1) // GLA_DK
    smask = srow_h == scol_h
    nt = (((1,), (1,)), ((), ()))
    n_chunks = TC // GLA_CHUNK
    for n in range(n_chunks):
        c = jnp.where(d == 0, n, n_chunks - 1 - n)
        rows = pl.ds(pl.multiple_of(c * GLA_CHUNK, GLA_CHUNK), GLA_CHUNK)
        q_in = qkk_ref[rows, 0:GLA_QK]
        k_in = qkk_ref[rows, GLA_QK:2 * GLA_QK]
        k_st = qkk_ref[rows, 2 * GLA_QK:3 * GLA_QK]
        v = v_ref[rows, :]
        decay = jnp.exp(dec_ref[pl.ds(c, 1), :])
        q_stack = jnp.concatenate(
            [jnp.where(lane_h == h, q_in, jnp.zeros_like(q_in)) for h in range(GLA_HEADS)], axis=0)
        att = lax.dot_general(q_stack, k_in, nt, preferred_element_type=F32)
        att = jnp.where(tri4, att, 0.0).astype(BF16)
        st_old = st_ref[...]
        o_inter = lax.dot_general(q_in, st_old.astype(BF16), nt, preferred_element_type=F32)
        o_intra = jnp.concatenate(
            [jnp.dot(att[h * GLA_CHUNK:(h + 1) * GLA_CHUNK, :],
                     v[:, h * GLA_DV:(h + 1) * GLA_DV], preferred_element_type=F32)
             for h in range(GLA_HEADS)], axis=1)
        o_ref[rows, :] = (o_intra + o_inter).astype(BF16)
        kvt = lax.dot_general(v, k_st, (((0,), (0,)), ((), ())), preferred_element_type=F32)
        st_ref[...] = st_old * decay + jnp.where(smask, kvt, 0.0)


def _gla(p, dec, tri):
    def rb(b, d, s):
        xblk = b * BLK_PER_SEQ + jnp.where(d == 0, s - 1, BLK_PER_SEQ - s)
        return jnp.where(s == 0, XB + b, xblk)

    cgd = P_GD // GD_W
    cv = P_GV // GLA_V
    n_ch = TC // GLA_CHUNK
    dec4 = dec.reshape(2, M_ALL // TC, n_ch, GLA_QK)
    return pl.pallas_call(
        _gla_kernel,
        grid=(BATCH, 2, BLK_PER_SEQ + 1),
        in_specs=[pl.BlockSpec((TC, GD_W), lambda b, d, s: (rb(b, d, s), cgd + d)),
                  pl.BlockSpec((TC, GLA_V), lambda b, d, s: (rb(b, d, s), cv)),
                  pl.BlockSpec((None, None, n_ch, GLA_QK), lambda b, d, s: (d, rb(b, d, s), 0, 0)),
                  pl.BlockSpec((None, GLA_CHUNK, GLA_CHUNK), lambda b, d, s: (d, 0, 0))],
        out_specs=pl.BlockSpec((None, TC, GLA_V), lambda b, d, s: (d, rb(b, d, s), 0)),
        out_shape=jax.ShapeDtypeStruct((2, M_ALL, GLA_V), BF16),
        scratch_shapes=[pltpu.VMEM((GLA_V, GLA_QK), F32)],
        compiler_params=_cparams(3),
        name="gla",
    )(p, p, dec4, tri)


NKEYS = CTX_LEN + SEQ


def _attn_kernel(q_ref, g_ref, ckv_ref, xkv_ref, selk_ref, selv_ref, one_ref, o_ref,
                 kz_ref, vx_ref, m_ref, acc_ref, al_ref, s_ref):
    qi = pl.program_id(1)

    @pl.when(qi == 0)
    def _():
        for h in range(ATT_KV_HEADS):
            for e in range(2):
                sk = selk_ref[h, e]
                sv = selv_ref[h, e]
                ones_row = one_ref[e]
                for (src, r0, n) in ((ckv_ref, 0, CTX_LEN), (xkv_ref, CTX_LEN, SEQ)):
                    for r in range(0, n, 1024):
                        nr = min(1024, n - r)
                        blk = src[r:r + nr, :]
                        kz_ref[h, e, r0 + r:r0 + r + nr, :] = jnp.dot(
                            blk, sk, preferred_element_type=F32).astype(BF16)
                        vx_ref[h, e, r0 + r:r0 + r + nr, :] = (jnp.dot(
                            blk, sv, preferred_element_type=F32) + ones_row).astype(BF16)

    m_ref[...] = jnp.full_like(m_ref, -jnp.inf)
    acc_ref[...] = jnp.zeros_like(acc_ref)

    def block(k0, nk):
        for pr in range(ATT_HEADS // 2):
            h = pr // (ATT_GROUP // 2)
            qp = q_ref[:, pr * LANES:(pr + 1) * LANES]
            for e in range(2):
                kz = kz_ref[h, e, pl.ds(k0, nk), :]
                s_ref[2 * pr + e, :, 0:nk] = lax.dot_general(
                    qp, kz, (((1,), (1,)), ((), ())), preferred_element_type=F32)
        for hd in range(ATT_HEADS):
            m_prev = m_ref[hd]
            m_next = jnp.maximum(m_prev, jnp.max(s_ref[hd, :, 0:nk], axis=-1, keepdims=True))
            al_ref[hd] = jnp.exp2(m_prev - m_next)
            m_ref[hd] = m_next
        for hd in range(ATT_HEADS):
            h, e = hd // ATT_GROUP, hd % 2
            m_rep = jnp.concatenate([m_ref[hd]] * (nk // LANES), axis=1)
            p = jnp.exp2(s_ref[hd, :, 0:nk] - m_rep).astype(BF16)
            vx = vx_ref[h, e, pl.ds(k0, nk), :]
            acc_ref[hd] = al_ref[hd] * acc_ref[hd] + jnp.dot(p, vx, preferred_element_type=F32)

    block(0, CTX_LEN)

    @pl.when(qi > 0)
    def _():
        def body(t, carry):
            block(pl.multiple_of(CTX_LEN + t * KB, KB // 2), KB)
            return carry
        lax.fori_loop(0, SEQ // KB, body, 0)

    lane = lax.broadcasted_iota(jnp.int32, (TC, LANES), 1)
    for pr in range(ATT_HEADS // 2):
        a0 = acc_ref[2 * pr]
        a1 = acc_ref[2 * pr + 1]
        o0 = a0 / a0[:, ATT_HD:ATT_HD + 1]
        o1 = a1 / a1[:, 0:1]
        slab = jnp.where(lane < ATT_HD, o0, o1)
        sl = slice(pr * LANES, (pr + 1) * LANES)
        o_ref[:, sl] = (slab * g_ref[:, sl].astype(F32)).astype(BF16)


def _attn(p, selk, selv, onerow):
    cq = P_AQ // ATT_Q
    cg = P_AG // ATT_Q
    ckv = P_AKV // (2 * ATT_KV)

    def qrow(b, qi):
        return jnp.where(qi == 0, XB + b, b * BLK_PER_SEQ + qi - 1)

    return pl.pallas_call(
        _attn_kernel,
        grid=(BATCH, BLK_PER_SEQ + 1),
        in_specs=[pl.BlockSpec((TC, ATT_Q), lambda b, qi: (qrow(b, qi), cq)),
                  pl.BlockSpec((TC, ATT_Q), lambda b, qi: (qrow(b, qi), cg)),
                  pl.BlockSpec((CTX_LEN, 2 * ATT_KV), lambda b, qi: (XB + b, ckv)),
                  pl.BlockSpec((SEQ, 2 * ATT_KV), lambda b, qi: (b, ckv)),
                  pl.BlockSpec((ATT_KV_HEADS, 2, 2 * ATT_KV, LANES), lambda b, qi: (0, 0, 0, 0)),
                  pl.BlockSpec((ATT_KV_HEADS, 2, 2 * ATT_KV, LANES), lambda b, qi: (0, 0, 0, 0)),
                  pl.BlockSpec((2, 1, LANES), lambda b, qi: (0, 0, 0))],
        out_specs=pl.BlockSpec((TC, ATT_Q), lambda b, qi: (qrow(b, qi), 0)),
        out_shape=jax.ShapeDtypeStruct((M_ALL, ATT_Q), BF16),
        scratch_shapes=[pltpu.VMEM((ATT_KV_HEADS, 2, NKEYS, LANES), BF16),
                        pltpu.VMEM((ATT_KV_HEADS, 2, NKEYS, LANES), BF16),
                        pltpu.VMEM((ATT_HEADS, TC, LANES), F32),
                        pltpu.VMEM((ATT_HEADS, TC, LANES), F32),
                        pltpu.VMEM((ATT_HEADS, TC, LANES), F32),
                        pltpu.VMEM((ATT_HEADS, TC, KB), F32)],
        compiler_params=_cparams(2),
        name="attention",
    )(p, p, p, p, selk, selv, onerow)


def _final_kernel(s_ref, sig_ref, gg_ref, ua_ref, of_ref, ob_ref, oc_ref, mod_ref,
                  wc_ref, wg_ref, wa_ref, wo_ref, gn_ref, o_ref):
    i = pl.program_id(0)
    row = jnp.where(i >= N_XT, BATCH, i // TILES_PER_SEQ)
    ya = jnp.dot(ua_ref[...], wc_ref[...], preferred_element_type=F32)
    og = of_ref[...].astype(F32) + ob_ref[...].astype(F32)
    gn = gn_ref[...]
    parts = []
    for h in range(GLA_HEADS):
        oh = og[:, h * GLA_DV:(h + 1) * GLA_DV]
        ms = jnp.mean(oh * oh, axis=-1, keepdims=True)
        parts.append(oh * lax.rsqrt(ms + EPS) * gn)
    on = jnp.concatenate(parts, axis=1) * gg_ref[...].astype(F32)
    yb = jnp.dot(on.astype(BF16), wg_ref[...], preferred_element_type=F32)
    yc = jnp.dot(oc_ref[...], wa_ref[...], preferred_element_type=F32)
    merged = (sig_ref[:, 0:D_MODEL].astype(F32) * ya
              + sig_ref[:, D_MODEL:2 * D_MODEL].astype(F32) * yb
              + sig_ref[:, 2 * D_MODEL:3 * D_MODEL].astype(F32) * yc)
    out = jnp.dot(merged.astype(BF16), wo_ref[...], preferred_element_type=F32)
    gate = mod_ref[pl.ds(row, 1), 2 * D_MODEL:3 * D_MODEL]
    o_ref[...] = s_ref[...] + gate * out


def _final(l, n_tiles, s_all, p, ua, o_gla, oc, mod, wc, wg, wa, wo, gn):
    rowblk = lambda i: (i, 0)
    wspec = lambda k: pl.BlockSpec((None, k, D_MODEL), lambda i: (l, 0, 0))
    return pl.pallas_call(
        _final_kernel,
        grid=(n_tiles,),
        in_specs=[pl.BlockSpec((TM, D_MODEL), rowblk),
                  pl.BlockSpec((TM, 3 * D_MODEL), lambda i: (i, P_SIG // (3 * D_MODEL))),
                  pl.BlockSpec((TM, GLA_V), lambda i: (i, P_GG // GLA_V)),
                  pl.BlockSpec((TM, D_CONV), rowblk),
                  pl.BlockSpec((None, TM, GLA_V), lambda i: (0, i, 0)),
                  pl.BlockSpec((None, TM, GLA_V), lambda i: (1, i, 0)),
                  pl.BlockSpec((TM, ATT_Q), rowblk),
                  pl.BlockSpec((None, 16, 3 * D_MODEL), lambda i: (l, 0, 0)),
                  wspec(D_CONV), wspec(GLA_V), wspec(ATT_Q), wspec(D_MODEL),
                  pl.BlockSpec((None, 1, GLA_DV), lambda i: (l, 0, 0))],
        out_specs=pl.BlockSpec((TM, D_MODEL), rowblk),
        out_shape=jax.ShapeDtypeStruct((n_tiles * TM, D_MODEL), F32),
        compiler_params=_cparams(1),
        name="merge_out",
    )(s_all, p, p, ua, o_gla, o_gla, oc, mod, wc, wg, wa, wo, gn)


def _rope_tables():
    t = np.arange(SEQ)
    row = (t // GRID_W).astype(np.float32)
    col = (t % GRID_W).astype(np.float32)
    n_freq = ROPE_AXIS_DIM // 2
    freqs = (np.float32(ROPE_THETA) ** (-np.arange(n_freq, dtype=np.float32) / n_freq)).astype(np.float32)
    ar = row[:, None] * freqs
    ac = col[:, None] * freqs
    cos64 = np.concatenate([np.cos(ar), np.cos(ar), np.cos(ac), np.cos(ac)], axis=1)
    sin64 = np.concatenate([-np.sin(ar), np.sin(ar), -np.sin(ac), np.sin(ac)], axis=1)
    return (np.tile(cos64, (1, ATT_HEADS)).astype(np.float32),
            np.tile(sin64, (1, ATT_HEADS)).astype(np.float32))


def _static_tables():
    bd = np.kron(np.eye(ATT_HEADS, dtype=np.float32), np.ones((ATT_HD, ATT_HD), np.float32))
    idx = np.arange(GLA_CHUNK)
    tri = np.stack([(idx[None, :] <= idx[:, None]), (idx[None, :] >= idx[:, None])]).astype(np.float32)
    selk = np.zeros((ATT_KV_HEADS, 2, 2 * ATT_KV, LANES), np.float32)
    selv = np.zeros((ATT_KV_HEADS, 2, 2 * ATT_KV, LANES), np.float32)
    onerow = np.zeros((2, 1, LANES), np.float32)
    for h in range(ATT_KV_HEADS):
        for e in range(2):
            for dd in range(ATT_HD):
                selk[h, e, h * ATT_HD + dd, e * ATT_HD + dd] = 1.0
                selv[h, e, ATT_KV + h * ATT_HD + dd, e * ATT_HD + dd] = 1.0
    onerow[0, 0, ATT_HD] = 1.0
    onerow[1, 0, 0] = 1.0
    shifts = np.stack([np.eye(CONV_SH, k=r, dtype=np.float32) for r in range(SUBLANES)])
    return bd, tri, selk, selv, onerow, shifts


def _permute_cols(a):
    pieces = [a[..., O_MA:N_IN], a[..., O_VAL:O_GQ], a[..., O_GQ:O_LR], a[..., O_AQ:O_AK],
              a[..., O_AG:O_MA], a[..., O_AK:O_AG], a[..., O_LR:O_AQ]]
    pad = jnp.zeros(a.shape[:-1] + (NW - N_IN,), a.dtype)
    return jnp.concatenate(pieces + [pad], axis=-1)


def kernel(x, c, ctx, c_ctx, norm_g, w_mod, b_mod, w_in, b_in, conv_dw_w, conv_dw_b, conv_ln_g,
           conv_ln_b, w_conv_out, gla_w_gate, gla_b_gate, gla_norm_g, w_gla_out, q_norm_g,
           k_norm_g, w_attn_out, w_out):
    cos_np, sin_np = _rope_tables()
    bd_np, tri_np, selk_np, selv_np, one_np, shifts_np = _static_tables()
    shifts = jnp.asarray(shifts_np, BF16)
    cos_t, sin_t = jnp.asarray(cos_np), jnp.asarray(sin_np)
    bd = jnp.asarray(bd_np, BF16)
    tri = jnp.asarray(tri_np)
    selk = jnp.asarray(selk_np, BF16)
    selv = jnp.asarray(selv_np, BF16)
    onerow = jnp.asarray(one_np)

    w_p = _permute_cols(w_in).astype(BF16)
    b_p = _permute_cols(b_in).reshape(DEPTH, 1, NW)
    cc = jnp.concatenate([c, c_ctx[None, :], jnp.zeros((16 - BATCH - 1, D_MODEL), F32)], axis=0)
    dw_w = jnp.broadcast_to(conv_dw_w[:, :, None, :], (DEPTH, CONV_WIDTH, SUBLANES, D_CONV))
    r3 = lambda a: a.reshape(DEPTH, 1, a.shape[-1])
    wup = jnp.zeros((DEPTH, LANES, 2 * GLA_QK), F32)
    wup = wup.at[:, 0:GLA_RANK, 0:GLA_QK].set(gla_w_gate[:, 0])
    wup = wup.at[:, GLA_RANK:2 * GLA_RANK, GLA_QK:].set(gla_w_gate[:, 1])
    wuh = wup.astype(BF16)
    wul = (wup - wuh.astype(F32)).astype(BF16)
    bup = gla_b_gate.reshape(DEPTH, 1, 2 * GLA_QK)
    qg = jnp.tile(q_norm_g, (1, ATT_HEADS)).reshape(DEPTH, 1, ATT_Q)
    kg = jnp.tile(k_norm_g, (1, ATT_KV_HEADS)).reshape(DEPTH, 1, ATT_KV)
    wc, wg, wa, wo = (w.astype(BF16) for w in (w_conv_out, w_gla_out, w_attn_out, w_out))

    mod = _modulation(cc, w_mod, b_mod)
    s_all = jnp.concatenate([x.reshape(MX, D_MODEL), ctx.reshape(MC, D_MODEL)], axis=0)
    for l in range(DEPTH):
        p, dec = _inproj(l, s_all, mod, r3(norm_g), w_p, b_p, cos_t, sin_t, qg, kg, bd, wuh, wul, bup)
        ua = _conv(l, p, dw_w, r3(conv_dw_b), r3(conv_ln_g), r3(conv_ln_b), shifts)
        o_gla = _gla(p, dec, tri)
        oc = _attn(p, selk, selv, onerow)
        n_tiles = M_ALL // TM if l < DEPTH - 1 else N_XT
        s_all = _final(l, n_tiles, s_all, p, ua, o_gla, oc, mod, wc, wg, wa, wo, r3(gla_norm_g))
    return s_all.reshape(BATCH, SEQ, D_MODEL)
```

```python
import functools
import math

import numpy as np
import jax
import jax.numpy as jnp
from jax import lax
from jax.experimental import pallas as pl
from jax.experimental.pallas import tpu as pltpu

F32 = jnp.float32
BF16 = jnp.bfloat16
HIGHEST = lax.Precision.HIGHEST

D_MODEL = 1024
BATCH = 8
SEQ = 4096
DEPTH = 4
CTX_LEN = 256
GRID_W = 64
EPS = 1e-6
D_CONV = 512
CONV_WIDTH = 31
CONV_PAD = CONV_WIDTH // 2
GLA_HEADS = 4
GLA_DK = 64
GLA_DV = 128
GLA_QK = GLA_HEADS * GLA_DK
GLA_V = GLA_HEADS * GLA_DV
GLA_RANK = 16
GLA_GATE_NORM = 16.0
GLA_CHUNK = 64
ATT_HEADS = 8
ATT_KV_HEADS = 2
ATT_GROUP = ATT_HEADS // ATT_KV_HEADS
ATT_HD = 64
ATT_Q = ATT_HEADS * ATT_HD
ATT_KV = ATT_KV_HEADS * ATT_HD
ROPE_AXIS_DIM = ATT_HD // 2
ROPE_THETA = 10000.0

LANES = 128
SUBLANES = 8
MX = BATCH * SEQ
MC = BATCH * CTX_LEN
M_ALL = MX + MC

O_VAL, O_GLU, O_CGATE = 0, 512, 1024
O_GQ, O_GK, O_GV, O_GG = 1536, 1792, 2048, 2560
O_LR = 3072
O_AQ, O_AK, O_AV, O_AG = 3104, 3616, 3744, 3872
O_MA = 4384
N_IN = 7456

W_M = 0
W_CONV = 3072
W_GLA = 4608
W_AQ = 6144
W_AG = 6656
W_AK = 7168
W_AV = 7296
W_LR = 7424
NW = 7552

P_SIG = 0
P_U = 3072
P_CG = 3584
P_GV = 4096
P_GD = 4608
GD_W = 3 * GLA_QK
P_GG = 6144
P_AQ = 6656
P_AG = 7168
P_AKV = 7680
NP = 7936

TM = 512
N_XT = MX // TM
N_CT = MC // TM
TILES_PER_SEQ = SEQ // TM
TC = 256
XB = MX // TC
BLK_PER_SEQ = SEQ // TC
HALO = 16
CONV_RC = 32
CONV_SH = TC // 2 + 2 * HALO
KB = 512
Q_PRESCALE = (ATT_HD ** -0.5) * math.log2(math.e)

VMEM_LIMIT = 56 * 1024 * 1024


def _cparams(n_axes, vmem=VMEM_LIMIT):
    return pltpu.CompilerParams(dimension_semantics=("arbitrary",) * n_axes,
                                vmem_limit_bytes=vmem)


def _silu(x):
    return x * jax.nn.sigmoid(x)


def _mod_kernel(c_ref, w_ref, b_ref, o_ref):
    s = _silu(c_ref[...])
    o_ref[...] = jnp.dot(s, w_ref[...], preferred_element_type=F32, precision=HIGHEST) + b_ref[...]


def _modulation(cc, w_mod, b_mod):
    nt = 3 * D_MODEL // 1024
    return pl.pallas_call(
        _mod_kernel,
        grid=(DEPTH, nt),
        in_specs=[pl.BlockSpec((16, D_MODEL), lambda l, n: (0, 0)),
                  pl.BlockSpec((None, D_MODEL, 1024), lambda l, n: (l, 0, n)),
                  pl.BlockSpec((None, 1, 1024), lambda l, n: (l, 0, n))],
        out_specs=pl.BlockSpec((None, 16, 1024), lambda l, n: (l, 0, n)),
        out_shape=jax.ShapeDtypeStruct((DEPTH, 16, 3 * D_MODEL), F32),
        compiler_params=_cparams(2),
        name="modulation",
    )(cc, w_mod, b_mod.reshape(DEPTH, 1, 3 * D_MODEL))


def _head_norm(xv, gain, bd):
    ss = jnp.dot((xv * xv).astype(BF16), bd, preferred_element_type=F32)
    return xv * lax.rsqrt(ss * (1.0 / ATT_HD) + EPS) * gain


def _rope(xv, cosv, sinv):
    parts = []
    for s in range(xv.shape[1] // LANES):
        sl = slice(s * LANES, (s + 1) * LANES)
        xs = xv[:, sl]
        up = pltpu.roll(xs, LANES - 16, axis=1)
        dn = pltpu.roll(xs, 16, axis=1)
        lane = lax.broadcasted_iota(jnp.int32, xs.shape, 1)
        partner = jnp.where((lane & 16) == 0, up, dn)
        parts.append(xs * cosv[:, sl] + partner * sinv[:, sl])
    return jnp.concatenate(parts, axis=1) if len(parts) > 1 else parts[0]


def _inproj_kernel(x_ref, mod_ref, g_ref, w_ref, b_ref, cos_ref, sin_ref, qg_ref, kg_ref,
                   bd_ref, wuh_ref, wul_ref, bup_ref, o_ref, dec_ref):
    i = pl.program_id(0)
    is_ctx = i >= N_XT
    row = jnp.where(is_ctx, BATCH, i // TILES_PER_SEQ)
    x = x_ref[...]
    ms = jnp.mean(x * x, axis=-1, keepdims=True)
    y = x * lax.rsqrt(ms + EPS) * g_ref[...]
    m = mod_ref[pl.ds(row, 1), :]
    shift = m[:, 0:D_MODEL]
    scale = m[:, D_MODEL:2 * D_MODEL]
    h = (y * (1.0 + scale) + shift).astype(BF16)

    def proj(a, n):
        return jnp.dot(h, w_ref[:, a:a + n], preferred_element_type=F32) + b_ref[:, a:a + n]

    def put(a, val):
        o_ref[:, a:a + val.shape[1]] = val.astype(BF16)

    for k in range(3):
        put(P_SIG + 1024 * k, jax.nn.sigmoid(proj(W_M + 1024 * k, 1024)))
    put(P_U, proj(W_CONV, 512) * jax.nn.sigmoid(proj(W_CONV + 512, 512)))
    put(P_CG, _silu(proj(W_CONV + 1024, 512)))
    put(P_GV, proj(W_GLA + 512, 512))
    put(P_GG, _silu(proj(W_GLA + 1024, 512)))

    lr = proj(W_LR, LANES)
    lr_hi = lr.astype(BF16)
    lr_lo = (lr - lr_hi.astype(F32)).astype(BF16)
    wuh = wuh_ref[...]
    z = (jnp.dot(lr_hi, wuh, preferred_element_type=F32)
         + jnp.dot(lr_lo, wuh, preferred_element_type=F32)
         + jnp.dot(lr_hi, wul_ref[...], preferred_element_type=F32)) + bup_ref[...]
    la = (jnp.minimum(z, 0.0) - jnp.log(1.0 + jnp.exp(-jnp.abs(z)))) * (1.0 / GLA_GATE_NORM)
    rowc = lax.broadcasted_iota(jnp.int32, (TM, GLA_QK), 0) & (GLA_CHUNK - 1)
    cf = la[:, :GLA_QK]
    cb = la[:, GLA_QK:]
    sh = 1
    while sh < GLA_CHUNK:
        cf = cf + jnp.where(rowc >= sh, pltpu.roll(cf, sh, axis=0), 0.0)
        cb = cb + jnp.where(rowc < GLA_CHUNK - sh, pltpu.roll(cb, TM - sh, axis=0), 0.0)
        sh *= 2
    n_ch = TM // GLA_CHUNK
    last_f = [cf[c * GLA_CHUNK + GLA_CHUNK - 1:(c + 1) * GLA_CHUNK, :] for c in range(n_ch)]
    last_b = [cb[c * GLA_CHUNK:c * GLA_CHUNK + 1, :] for c in range(n_ch)]
    dec_ref[0] = jnp.concatenate(last_f, axis=0)
    dec_ref[1] = jnp.concatenate(last_b, axis=0)
    qk = proj(W_GLA, 512)
    gq = qk[:, :GLA_QK] * (GLA_DK ** -0.5)
    gk = qk[:, GLA_QK:]
    for dd, (cum, last) in enumerate(((cf, last_f), (cb, last_b))):
        tot = jnp.concatenate([jnp.broadcast_to(t, (GLA_CHUNK, GLA_QK)) for t in last], axis=0)
        base = P_GD + dd * GD_W
        put(base, gq * jnp.exp(cum))
        put(base + GLA_QK, gk * jnp.exp(-cum))
        put(base + 2 * GLA_QK, gk * jnp.exp(tot - cum))

    cosv = jnp.where(is_ctx, 1.0, cos_ref[...])
    sinv = jnp.where(is_ctx, 0.0, sin_ref[...])
    bd = bd_ref[...]
    aq = _head_norm(proj(W_AQ, ATT_Q), qg_ref[...], bd)
    put(P_AQ, _rope(aq, cosv, sinv) * Q_PRESCALE)
    put(P_AG, _silu(proj(W_AG, ATT_Q)))
    akv = proj(W_AK, 2 * ATT_KV)
    ak = _head_norm(akv[:, :ATT_KV], kg_ref[...], bd[:ATT_KV, :ATT_KV])
    put(P_AKV, _rope(ak, cosv[:, :ATT_KV], sinv[:, :ATT_KV]))
    put(P_AKV + ATT_KV, akv[:, ATT_KV:])


def _inproj(l, s_all, mod, norm_g, w_p, b_p, cos_t, sin_t, qg, kg, bd, wuh, wul, bup):
    const = lambda i: (0, 0)
    pos = lambda i: (jnp.where(i >= N_XT, 0, i % TILES_PER_SEQ), 0)
    return pl.pallas_call(
        _inproj_kernel,
        grid=(M_ALL // TM,),
        in_specs=[pl.BlockSpec((TM, D_MODEL), lambda i: (i, 0)),
                  pl.BlockSpec((None, 16, 3 * D_MODEL), lambda i: (l, 0, 0)),
                  pl.BlockSpec((None, 1, D_MODEL), lambda i: (l, 0, 0)),
                  pl.BlockSpec((None, D_MODEL, NW), lambda i: (l, 0, 0),
                               pipeline_mode=pl.Buffered(1)),
                  pl.BlockSpec((None, 1, NW), lambda i: (l, 0, 0)),
                  pl.BlockSpec((TM, ATT_Q), pos),
                  pl.BlockSpec((TM, ATT_Q), pos),
                  pl.BlockSpec((None, 1, ATT_Q), lambda i: (l, 0, 0)),
                  pl.BlockSpec((None, 1, ATT_KV), lambda i: (l, 0, 0)),
                  pl.BlockSpec((ATT_Q, ATT_Q), const),
                  pl.BlockSpec((None, LANES, 2 * GLA_QK), lambda i: (l, 0, 0)),
                  pl.BlockSpec((None, LANES, 2 * GLA_QK), lambda i: (l, 0, 0)),
                  pl.BlockSpec((None, 1, 2 * GLA_QK), lambda i: (l, 0, 0))],
        out_specs=[pl.BlockSpec((TM, NP), lambda i: (i, 0)),
                   pl.BlockSpec((2, TM // GLA_CHUNK, GLA_QK), lambda i: (0, i, 0))],
        out_shape=[jax.ShapeDtypeStruct((M_ALL, NP), BF16),
                   jax.ShapeDtypeStruct((2, M_ALL // GLA_CHUNK, GLA_QK), F32)],
        compiler_params=_cparams(1),
        name="inproj",
    )(s_all, mod, norm_g, w_p, b_p, cos_t, sin_t, qg, kg, bd, wuh, wul, bup)


def _conv_kernel(u_ref, ul_ref, ur_ref, sg_ref, dw_ref, dwb_ref, lng_ref, lnb_ref, sh_ref,
                 o_ref, win_ref, ext_ref):
    i = pl.program_id(0)
    j = i % BLK_PER_SEQ
    is_x = i < XB
    left_ok = jnp.logical_and(is_x, j != 0)
    right_ok = jnp.logical_and(is_x, j != BLK_PER_SEQ - 1)
    zero_h = jnp.zeros((HALO, D_CONV), BF16)
    win_ref[0:HALO, :] = jnp.where(left_ok, ul_ref[...], zero_h)
    win_ref[HALO:HALO + TC, :] = u_ref[...]
    win_ref[HALO + TC:, :] = jnp.where(right_ok, ur_ref[...], zero_h)
    ext_ref[0] = win_ref[...].astype(F32)
    half = TC // 2
    for r in range(1, SUBLANES):
        for a in range(2):
            ext_ref[r, a * half:a * half + CONV_SH, :] = jnp.dot(
                sh_ref[r], win_ref[a * half:a * half + CONV_SH, :], preferred_element_type=F32)
    bias = dwb_ref[...]
    lng = lng_ref[...]
    lnb = lnb_ref[...]
    for c in range(TC // CONV_RC):
        r0 = c * CONV_RC
        acc = jnp.broadcast_to(bias, (CONV_RC, D_CONV))
        for t in range(CONV_WIDTH):
            off = t + HALO - CONV_PAD
            a0 = r0 + (off // SUBLANES) * SUBLANES
            w_t = jnp.concatenate([dw_ref[t]] * (CONV_RC // SUBLANES), axis=0)
            acc = acc + ext_ref[off % SUBLANES, a0:a0 + CONV_RC, :] * w_t
        mu = jnp.mean(acc, axis=-1, keepdims=True)
        d = acc - mu
        var = jnp.mean(d * d, axis=-1, keepdims=True)
        yn = d * lax.rsqrt(var + EPS) * lng + lnb
        o_ref[r0:r0 + CONV_RC, :] = (_silu(yn) * sg_ref[r0:r0 + CONV_RC, :].astype(F32)).astype(BF16)


def _conv(l, p, dw_w, dw_b, ln_g, ln_b, shifts):
    nhb = M_ALL // HALO
    per = TC // HALO
    cu = P_U // D_CONV
    cg = P_CG // D_CONV
    vec = lambda i: (l, 0, 0)
    return pl.pallas_call(
        _conv_kernel,
        grid=(M_ALL // TC,),
        in_specs=[pl.BlockSpec((TC, D_CONV), lambda i: (i, cu)),
                  pl.BlockSpec((HALO, D_CONV), lambda i: (jnp.maximum(i * per - 1, 0), cu)),
                  pl.BlockSpec((HALO, D_CONV), lambda i: (jnp.minimum((i + 1) * per, nhb - 1), cu)),
                  pl.BlockSpec((TC, D_CONV), lambda i: (i, cg)),
                  pl.BlockSpec((None, CONV_WIDTH, SUBLANES, D_CONV), lambda i: (l, 0, 0, 0)),
                  pl.BlockSpec((None, 1, D_CONV), vec),
                  pl.BlockSpec((None, 1, D_CONV), vec),
                  pl.BlockSpec((None, 1, D_CONV), vec),
                  pl.BlockSpec((SUBLANES, CONV_SH, CONV_SH), lambda i: (0, 0, 0))],
        out_specs=pl.BlockSpec((TC, D_CONV), lambda i: (i, 0)),
        out_shape=jax.ShapeDtypeStruct((M_ALL, D_CONV), BF16),
        scratch_shapes=[pltpu.VMEM((TC + 2 * HALO, D_CONV), BF16),
                        pltpu.VMEM((SUBLANES, TC + 2 * HALO, D_CONV), F32)],
        compiler_params=_cparams(1),
        name="conv",
    )(p, p, p, p, dw_w, dw_b, ln_g, ln_b, shifts)


def _gla_kernel(qkk_ref, v_ref, dec_ref, tri_ref, o_ref, st_ref):
    d = pl.program_id(1)
    step = pl.program_id(2)

    @pl.when(step == 0)
    def _():
        st_ref[...] = jnp.zeros_like(st_ref)

    tri4 = jnp.concatenate([tri_ref[...]] * GLA_HEADS, axis=0) > 0.5
    lane_h = lax.broadcasted_iota(jnp.int32, (GLA_CHUNK, GLA_QK), 1) // GLA_DK
    srow_h = lax.broadcasted_iota(jnp.int32, (GLA_V, GLA_QK), 0) // GLA_DV
    scol_h = lax.broadcasted_iota(jnp.int32, (GLA_V, GLA_QK), 1) // GLA_DK
    smask = srow_h == scol_h
    nt = (((1,), (1,)), ((), ()))
    n_chunks = TC // GLA_CHUNK
    for n in range(n_chunks):
        c = jnp.where(d == 0, n, n_chunks - 1 - n)
        rows = pl.ds(pl.multiple_of(c * GLA_CHUNK, GLA_CHUNK), GLA_CHUNK)
        q_in = qkk_ref[rows, 0:GLA_QK]
        k_in = qkk_ref[rows, GLA_QK:2 * GLA_QK]
        k_st = qkk_ref[rows, 2 * GLA_QK:3 * GLA_QK]
        v = v_ref[rows, :]
        decay = jnp.exp(dec_ref[pl.ds(c, 1), :])
        q_stack = jnp.concatenate(
            [jnp.where(lane_h == h, q_in, jnp.zeros_like(q_in)) for h in range(GLA_HEADS)], axis=0)
        att = lax.dot_general(q_stack, k_in, nt, preferred_element_type=F32)
        att = jnp.where(tri4, att, 0.0).astype(BF16)
        st_old = st_ref[...]
        o_inter = lax.dot_general(q_in, st_old.astype(BF16), nt, preferred_element_type=F32)
        o_intra = jnp.concatenate(
            [jnp.dot(att[h * GLA_CHUNK:(h + 1) * GLA_CHUNK, :],
                     v[:, h * GLA_DV:(h + 1) * GLA_DV], preferred_element_type=F32)
             for h in range(GLA_HEADS)], axis=1)
        o_ref[rows, :] = (o_intra + o_inter).astype(BF16)
        kvt = lax.dot_general(v, k_st, (((0,), (0,)), ((), ())), preferred_element_type=F32)
        st_ref[...] = st_old * decay + jnp.where(smask, kvt, 0.0)


def _gla(p, dec, tri):
    def rb(b, d, s):
        xblk = b * BLK_PER_SEQ + jnp.where(d == 0, s - 1, BLK_PER_SEQ - s)
        return jnp.where(s == 0, XB + b, xblk)

    cgd = P_GD // GD_W
    cv = P_GV // GLA_V
    n_ch = TC // GLA_CHUNK
    dec4 = dec.reshape(2, M_ALL // TC, n_ch, GLA_QK)
    return pl.pallas_call(
        _gla_kernel,
        grid=(BATCH, 2, BLK_PER_SEQ + 1),
        in_specs=[pl.BlockSpec((TC, GD_W), lambda b, d, s: (rb(b, d, s), cgd + d)),
                  pl.BlockSpec((TC, GLA_V), lambda b, d, s: (rb(b, d, s), cv)),
                  pl.BlockSpec((None, None, n_ch, GLA_QK), lambda b, d, s: (d, rb(b, d, s), 0, 0)),
                  pl.BlockSpec((None, GLA_CHUNK, GLA_CHUNK), lambda b, d, s: (d, 0, 0))],
        out_specs=pl.BlockSpec((None, TC, GLA_V), lambda b, d, s: (d, rb(b, d, s), 0)),
        out_shape=jax.ShapeDtypeStruct((2, M_ALL, GLA_V), BF16),
        scratch_shapes=[pltpu.VMEM((GLA_V, GLA_QK), F32)],
        compiler_params=_cparams(3),
        name="gla",
    )(p, p, dec4, tri)


NKEYS = CTX_LEN + SEQ
VROWS = 80
NT_DIMS = (((1,), (1,)), ((), ()))


def _attn_kernel(q_ref, g_ref, ckv_ref, xkv_ref, selk_ref, selvt_ref, eye_ref, o_ref,
                 kz_ref, vtc_ref, vtx_ref, qt_ref, m_ref, al_ref, acc_ref, s_ref):
    qi = pl.program_id(1)
    row_v = lax.broadcasted_iota(jnp.int32, (VROWS, 1), 0)

    def ext_values(svt, blk):
        vt = lax.dot_general(svt, blk, NT_DIMS, preferred_element_type=F32)
        return jnp.where(row_v == ATT_HD, 1.0, vt).astype(BF16)

    @pl.when(qi == 0)
    def _():
        for h in range(ATT_KV_HEADS):
            for e in range(2):
                sk = selk_ref[h, e]
                for (src, r0, n) in ((ckv_ref, 0, CTX_LEN), (xkv_ref, CTX_LEN, SEQ)):
                    for r in range(0, n, 1024):
                        nr = min(1024, n - r)
                        kz_ref[h, e, r0 + r:r0 + r + nr, :] = jnp.dot(
                            src[r:r + nr, :], sk, preferred_element_type=F32).astype(BF16)
            svt = selvt_ref[h]
            vtc_ref[h] = ext_values(svt, ckv_ref[...])
            for t in range(SEQ // KB):
                vtx_ref[h, t] = ext_values(svt, xkv_ref[t * KB:(t + 1) * KB, :])

    eye = eye_ref[...]
    for pr in range(ATT_HEADS // 2):
        qt_ref[pr] = lax.dot_general(eye, q_ref[:, pr * LANES:(pr + 1) * LANES], NT_DIMS,
                                     preferred_element_type=F32).astype(BF16)
    m_ref[...] = jnp.full_like(m_ref, -jnp.inf)
    acc_ref[...] = jnp.zeros_like(acc_ref)

    def scores(nxt, hd):
        slot, k0, nk, _ = nxt
        h, pr, e = hd // ATT_GROUP, hd // 2, hd % 2
        s_ref[slot, hd, 0:nk, :] = jnp.dot(kz_ref[h, e, pl.ds(k0, nk), :], qt_ref[pr],
                                           preferred_element_type=F32)

    def stage(cur, nxt):
        if cur is not None:
            slot, _, nk, vt_of = cur
            for hd in range(ATT_HEADS):
                s3 = s_ref[slot, hd, 0:nk, :].reshape(nk // SUBLANES, SUBLANES, TC)
                m_col = jnp.max(jnp.max(s3, axis=0), axis=0, keepdims=True)
                m_prev = m_ref[hd]
                m_next = jnp.maximum(m_prev, m_col)
                al_ref[hd] = jnp.exp2(m_prev - m_next)
                m_ref[hd] = m_next
        for hd in range(ATT_HEADS):
            if nxt is not None:
                scores(nxt, hd)
            if cur is not None:
                s3 = s_ref[slot, hd, 0:nk, :].reshape(nk // SUBLANES, SUBLANES, TC)
                p = jnp.exp2(s3 - m_ref[hd][None]).reshape(nk, TC).astype(BF16)
                pv = jnp.dot(vt_of(hd // ATT_GROUP), p, preferred_element_type=F32)
                acc3 = acc_ref[hd].reshape(VROWS // SUBLANES, SUBLANES, TC) * al_ref[hd][None]
                acc_ref[hd] = acc3.reshape(VROWS, TC) + pv

    n_xb = SEQ // KB
    ctx_stage = (1, 0, CTX_LEN, lambda h: vtc_ref[h])

    def x_stage(t, slot):
        k0 = CTX_LEN + t * KB
        if not isinstance(t, int):
            k0 = pl.multiple_of(k0, CTX_LEN)
        return (slot, k0, KB, lambda h: vtx_ref[h, t])

    stage(None, ctx_stage)

    @pl.when(qi == 0)
    def _():
        stage(ctx_stage, None)

    @pl.when(qi > 0)
    def _():
        stage(ctx_stage, x_stage(0, 0))

        def body(j, carry):
            t = 2 * j
            stage(x_stage(t, 0), x_stage(t + 1, 1))
            stage(x_stage(t + 1, 1), x_stage(t + 2, 0))
            return carry
        lax.fori_loop(0, (n_xb - 2) // 2, body, 0)
        stage(x_stage(n_xb - 2, 0), x_stage(n_xb - 1, 1))
        stage(x_stage(n_xb - 1, 1), None)

    outs = []
    for hd in range(ATT_HEADS):
        a = acc_ref[hd]
        outs.append(a[0:ATT_HD, :] / a[ATT_HD:ATT_HD + 1, :])
    o_nat = jnp.concatenate(outs, axis=0).T
    o_ref[...] = (o_nat * g_ref[...].astype(F32)).astype(BF16)


def _attn(p, selk, selvt, eye):
    cq = P_AQ // ATT_Q
    cg = P_AG // ATT_Q
    ckv = P_AKV // (2 * ATT_KV)

    def qrow(b, qi):
        return jnp.where(qi == 0, XB + b, b * BLK_PER_SEQ + qi - 1)

    return pl.pallas_call(
        _attn_kernel,
        grid=(BATCH, BLK_PER_SEQ + 1),
        in_specs=[pl.BlockSpec((TC, ATT_Q), lambda b, qi: (qrow(b, qi), cq)),
                  pl.BlockSpec((TC, ATT_Q), lambda b, qi: (qrow(b, qi), cg)),
                  pl.BlockSpec((CTX_LEN, 2 * ATT_KV), lambda b, qi: (XB + b, ckv)),
                  pl.BlockSpec((SEQ, 2 * ATT_KV), lambda b, qi: (b, ckv)),
                  pl.BlockSpec((ATT_KV_HEADS, 2, 2 * ATT_KV, LANES), lambda b, qi: (0, 0, 0, 0)),
                  pl.BlockSpec((ATT_KV_HEADS, VROWS, 2 * ATT_KV), lambda b, qi: (0, 0, 0)),
                  pl.BlockSpec((LANES, LANES), lambda b, qi: (0, 0))],
        out_specs=pl.BlockSpec((TC, ATT_Q), lambda b, qi: (qrow(b, qi), 0)),
        out_shape=jax.ShapeDtypeStruct((M_ALL, ATT_Q), BF16),
        scratch_shapes=[pltpu.VMEM((ATT_KV_HEADS, 2, NKEYS, LANES), BF16),
                        pltpu.VMEM((ATT_KV_HEADS, VROWS, CTX_LEN), BF16),
                        pltpu.VMEM((ATT_KV_HEADS, SEQ // KB, VROWS, KB), BF16),
                        pltpu.VMEM((ATT_HEADS // 2, LANES, TC), BF16),
                        pltpu.VMEM((ATT_HEADS, SUBLANES, TC), F32),
                        pltpu.VMEM((ATT_HEADS, SUBLANES, TC), F32),
                        pltpu.VMEM((ATT_HEADS, VROWS, TC), F32),
                        pltpu.VMEM((2, ATT_HEADS, KB, TC), F32)],
        compiler_params=_cparams(2),
        name="attention",
    )(p, p, p, p, selk, selvt, eye)


def _final_kernel(s_ref, sig_ref, gg_ref, ua_ref, of_ref, ob_ref, oc_ref, mod_ref,
                  wc_ref, wg_ref, wa_ref, wo_ref, gn_ref, o_ref):
    i = pl.program_id(0)
    row = jnp.where(i >= N_XT, BATCH, i // TILES_PER_SEQ)
    ya = jnp.dot(ua_ref[...], wc_ref[...], preferred_element_type=F32)
    og = of_ref[...].astype(F32) + ob_ref[...].astype(F32)
    gn = gn_ref[...]
    parts = []
    for h in range(GLA_HEADS):
        oh = og[:, h * GLA_DV:(h + 1) * GLA_DV]
        ms = jnp.mean(oh * oh, axis=-1, keepdims=True)
        parts.append(oh * lax.rsqrt(ms + EPS) * gn)
    on = jnp.concatenate(parts, axis=1) * gg_ref[...].astype(F32)
    yb = jnp.dot(on.astype(BF16), wg_ref[...], preferred_element_type=F32)
    yc = jnp.dot(oc_ref[...], wa_ref[...], preferred_element_type=F32)
    merged = (sig_ref[:, 0:D_MODEL].astype(F32) * ya
              + sig_ref[:, D_MODEL:2 * D_MODEL].astype(F32) * yb
              + sig_ref[:, 2 * D_MODEL:3 * D_MODEL].astype(F32) * yc)
    out = jnp.dot(merged.astype(BF16), wo_ref[...], preferred_element_type=F32)
    gate = mod_ref[pl.ds(row, 1), 2 * D_MODEL:3 * D_MODEL]
    o_ref[...] = s_ref[...] + gate * out


def _final(l, n_tiles, s_all, p, ua, o_gla, oc, mod, wc, wg, wa, wo, gn):
    rowblk = lambda i: (i, 0)
    wspec = lambda k: pl.BlockSpec((None, k, D_MODEL), lambda i: (l, 0, 0))
    return pl.pallas_call(
        _final_kernel,
        grid=(n_tiles,),
        in_specs=[pl.BlockSpec((TM, D_MODEL), rowblk),
                  pl.BlockSpec((TM, 3 * D_MODEL), lambda i: (i, P_SIG // (3 * D_MODEL))),
                  pl.BlockSpec((TM, GLA_V), lambda i: (i, P_GG // GLA_V)),
                  pl.BlockSpec((TM, D_CONV), rowblk),
                  pl.BlockSpec((None, TM, GLA_V), lambda i: (0, i, 0)),
                  pl.BlockSpec((None, TM, GLA_V), lambda i: (1, i, 0)),
                  pl.BlockSpec((TM, ATT_Q), rowblk),
                  pl.BlockSpec((None, 16, 3 * D_MODEL), lambda i: (l, 0, 0)),
                  wspec(D_CONV), wspec(GLA_V), wspec(ATT_Q), wspec(D_MODEL),
                  pl.BlockSpec((None, 1, GLA_DV), lambda i: (l, 0, 0))],
        out_specs=pl.BlockSpec((TM, D_MODEL), rowblk),
        out_shape=jax.ShapeDtypeStruct((n_tiles * TM, D_MODEL), F32),
        compiler_params=_cparams(1),
        name="merge_out",
    )(s_all, p, p, ua, o_gla, o_gla, oc, mod, wc, wg, wa, wo, gn)


def _rope_tables():
    t = np.arange(SEQ)
    row = (t // GRID_W).astype(np.float32)
    col = (t % GRID_W).astype(np.float32)
    n_freq = ROPE_AXIS_DIM // 2
    freqs = (np.float32(ROPE_THETA) ** (-np.arange(n_freq, dtype=np.float32) / n_freq)).astype(np.float32)
    ar = row[:, None] * freqs
    ac = col[:, None] * freqs
    cos64 = np.concatenate([np.cos(ar), np.cos(ar), np.cos(ac), np.cos(ac)], axis=1)
    sin64 = np.concatenate([-np.sin(ar), np.sin(ar), -np.sin(ac), np.sin(ac)], axis=1)
    return (np.tile(cos64, (1, ATT_HEADS)).astype(np.float32),
            np.tile(sin64, (1, ATT_HEADS)).astype(np.float32))


def _static_tables():
    bd = np.kron(np.eye(ATT_HEADS, dtype=np.float32), np.ones((ATT_HD, ATT_HD), np.float32))
    idx = np.arange(GLA_CHUNK)
    tri = np.stack([(idx[None, :] <= idx[:, None]), (idx[None, :] >= idx[:, None])]).astype(np.float32)
    selk = np.zeros((ATT_KV_HEADS, 2, 2 * ATT_KV, LANES), np.float32)
    selvt = np.zeros((ATT_KV_HEADS, VROWS, 2 * ATT_KV), np.float32)
    for h in range(ATT_KV_HEADS):
        for dd in range(ATT_HD):
            selvt[h, dd, ATT_KV + h * ATT_HD + dd] = 1.0
            for e in range(2):
                selk[h, e, h * ATT_HD + dd, e * ATT_HD + dd] = 1.0
    shifts = np.stack([np.eye(CONV_SH, k=r, dtype=np.float32) for r in range(SUBLANES)])
    return bd, tri, selk, selvt, shifts


def _permute_cols(a):
    pieces = [a[..., O_MA:N_IN], a[..., O_VAL:O_GQ], a[..., O_GQ:O_LR], a[..., O_AQ:O_AK],
              a[..., O_AG:O_MA], a[..., O_AK:O_AG], a[..., O_LR:O_AQ]]
    pad = jnp.zeros(a.shape[:-1] + (NW - N_IN,), a.dtype)
    return jnp.concatenate(pieces + [pad], axis=-1)


def kernel(x, c, ctx, c_ctx, norm_g, w_mod, b_mod, w_in, b_in, conv_dw_w, conv_dw_b, conv_ln_g,
           conv_ln_b, w_conv_out, gla_w_gate, gla_b_gate, gla_norm_g, w_gla_out, q_norm_g,
           k_norm_g, w_attn_out, w_out):
    cos_np, sin_np = _rope_tables()
    bd_np, tri_np, selk_np, selvt_np, shifts_np = _static_tables()
    shifts = jnp.asarray(shifts_np, BF16)
    cos_t, sin_t = jnp.asarray(cos_np), jnp.asarray(sin_np)
    bd = jnp.asarray(bd_np, BF16)
    tri = jnp.asarray(tri_np)
    selk = jnp.asarray(selk_np, BF16)
    selvt = jnp.asarray(selvt_np, BF16)
    eye = jnp.eye(LANES, dtype=BF16)

    w_p = _permute_cols(w_in.astype(BF16))
    b_p = _permute_cols(b_in).reshape(DEPTH, 1, NW)
    cc = jnp.concatenate([c, c_ctx[None, :], jnp.zeros((16 - BATCH - 1, D_MODEL), F32)], axis=0)
    dw_w = jnp.broadcast_to(conv_dw_w[:, :, None, :], (DEPTH, CONV_WIDTH, SUBLANES, D_CONV))
    r3 = lambda a: a.reshape(DEPTH, 1, a.shape[-1])
    wup = jnp.zeros((DEPTH, LANES, 2 * GLA_QK), F32)
    wup = wup.at[:, 0:GLA_RANK, 0:GLA_QK].set(gla_w_gate[:, 0])
    wup = wup.at[:, GLA_RANK:2 * GLA_RANK, GLA_QK:].set(gla_w_gate[:, 1])
    wuh = wup.astype(BF16)
    wul = (wup - wuh.astype(F32)).astype(BF16)
    bup = gla_b_gate.reshape(DEPTH, 1, 2 * GLA_QK)
    qg = jnp.tile(q_norm_g, (1, ATT_HEADS)).reshape(DEPTH, 1, ATT_Q)
    kg = jnp.tile(k_norm_g, (1, ATT_KV_HEADS)).reshape(DEPTH, 1, ATT_KV)
    wc, wg, wa, wo = (w.astype(BF16) for w in (w_conv_out, w_gla_out, w_attn_out, w_out))

    mod = _modulation(cc, w_mod, b_mod)
    s_all = jnp.concatenate([x.reshape(MX, D_MODEL), ctx.reshape(MC, D_MODEL)], axis=0)
    for l in range(DEPTH):
        p, dec = _inproj(l, s_all, mod, r3(norm_g), w_p, b_p, cos_t, sin_t, qg, kg, bd, wuh, wul, bup)
        ua = _conv(l, p, dw_w, r3(conv_dw_b), r3(conv_ln_g), r3(conv_ln_b), shifts)
        o_gla = _gla(p, dec, tri)
        oc = _attn(p, selk, selvt, eye)
        n_tiles = M_ALL // TM if l < DEPTH - 1 else N_XT
        s_all = _final(l, n_tiles, s_all, p, ua, o_gla, oc, mod, wc, wg, wa, wo, r3(gla_norm_g))
    return s_all.reshape(BATCH, SEQ, D_MODEL)
```

```python
import functools
import math

import numpy as np
import jax
import jax.numpy as jnp
from jax import lax
from jax.experimental import pallas as pl
from jax.experimental.pallas import tpu as pltpu

F32 = jnp.float32
BF16 = jnp.bfloat16
HIGHEST = lax.Precision.HIGHEST

D_MODEL = 1024
BATCH = 8
SEQ = 4096
DEPTH = 4
CTX_LEN = 256
GRID_W = 64
EPS = 1e-6
D_CONV = 512
CONV_WIDTH = 31
CONV_PAD = CONV_WIDTH // 2
GLA_HEADS = 4
GLA_DK = 64
GLA_DV = 128
GLA_QK = GLA_HEADS * GLA_DK
GLA_V = GLA_HEADS * GLA_DV
GLA_RANK = 16
GLA_GATE_NORM = 16.0
GLA_CHUNK = 64
ATT_HEADS = 8
ATT_KV_HEADS = 2
ATT_GROUP = ATT_HEADS // ATT_KV_HEADS
ATT_HD = 64
ATT_Q = ATT_HEADS * ATT_HD
ATT_KV = ATT_KV_HEADS * ATT_HD
ROPE_AXIS_DIM = ATT_HD // 2
ROPE_THETA = 10000.0

LANES = 128
SUBLANES = 8
MX = BATCH * SEQ
MC = BATCH * CTX_LEN
M_ALL = MX + MC

O_VAL, O_GLU, O_CGATE = 0, 512, 1024
O_GQ, O_GK, O_GV, O_GG = 1536, 1792, 2048, 2560
O_LR = 3072
O_AQ, O_AK, O_AV, O_AG = 3104, 3616, 3744, 3872
O_MA = 4384
N_IN = 7456

W_M = 0
W_CONV = 3072
W_GLA = 4608
W_AQ = 6144
W_AG = 6656
W_AK = 7168
W_AV = 7296
W_LR = 7424
NW = 7552

P_SIG = 0
P_U = 3072
P_CG = 3584
P_GV = 4096
P_GD = 4608
GD_W = 3 * GLA_QK
P_GG = 6144
P_AQ = 6656
P_AG = 7168
P_AKV = 7680
NP = 7936

TM = 512
N_XT = MX // TM
N_CT = MC // TM
TILES_PER_SEQ = SEQ // TM
TC = 256
XB = MX // TC
BLK_PER_SEQ = SEQ // TC
HALO = 16
CONV_RC = 32
CONV_SH = TC // 2 + 2 * HALO
KB = 512
Q_PRESCALE = (ATT_HD ** -0.5) * math.log2(math.e)

VMEM_LIMIT = 56 * 1024 * 1024


def _cparams(n_axes, vmem=VMEM_LIMIT):
    return pltpu.CompilerParams(dimension_semantics=("arbitrary",) * n_axes,
                                vmem_limit_bytes=vmem)


def _silu(x):
    return x * jax.nn.sigmoid(x)


def _mod_kernel(c_ref, w_ref, b_ref, o_ref):
    s = _silu(c_ref[...])
    o_ref[...] = jnp.dot(s, w_ref[...], preferred_element_type=F32, precision=HIGHEST) + b_ref[...]


def _modulation(cc, w_mod, b_mod):
    nt = 3 * D_MODEL // 1024
    return pl.pallas_call(
        _mod_kernel,
        grid=(DEPTH, nt),
        in_specs=[pl.BlockSpec((16, D_MODEL), lambda l, n: (0, 0)),
                  pl.BlockSpec((None, D_MODEL, 1024), lambda l, n: (l, 0, n)),
                  pl.BlockSpec((None, 1, 1024), lambda l, n: (l, 0, n))],
        out_specs=pl.BlockSpec((None, 16, 1024), lambda l, n: (l, 0, n)),
        out_shape=jax.ShapeDtypeStruct((DEPTH, 16, 3 * D_MODEL), F32),
        compiler_params=_cparams(2),
        name="modulation",
    )(cc, w_mod, b_mod.reshape(DEPTH, 1, 3 * D_MODEL))


def _head_norm(xv, gain, bd):
    ss = jnp.dot((xv * xv).astype(BF16), bd, preferred_element_type=F32)
    return xv * lax.rsqrt(ss * (1.0 / ATT_HD) + EPS) * gain


def _rope(xv, cosv, sinv):
    parts = []
    for s in range(xv.shape[1] // LANES):
        sl = slice(s * LANES, (s + 1) * LANES)
        xs = xv[:, sl]
        up = pltpu.roll(xs, LANES - 16, axis=1)
        dn = pltpu.roll(xs, 16, axis=1)
        lane = lax.broadcasted_iota(jnp.int32, xs.shape, 1)
        partner = jnp.where((lane & 16) == 0, up, dn)
        parts.append(xs * cosv[:, sl] + partner * sinv[:, sl])
    return jnp.concatenate(parts, axis=1) if len(parts) > 1 else parts[0]


def _inproj_kernel(x_ref, c_ref, mod_ref, g_ref, w_ref, b_ref, cos_ref, sin_ref, qg_ref, kg_ref,
                   bd_ref, wuh_ref, wul_ref, bup_ref, o_ref, dec_ref):
    i = pl.program_id(0)
    is_ctx = i >= N_XT
    row = jnp.where(is_ctx, BATCH, i // TILES_PER_SEQ)
    x = jnp.where(is_ctx, c_ref[...], x_ref[...])
    ms = jnp.mean(x * x, axis=-1, keepdims=True)
    y = x * lax.rsqrt(ms + EPS) * g_ref[...]
    m = mod_ref[pl.ds(row, 1), :]
    shift = m[:, 0:D_MODEL]
    scale = m[:, D_MODEL:2 * D_MODEL]
    h = (y * (1.0 + scale) + shift).astype(BF16)

    def proj(a, n):
        return jnp.dot(h, w_ref[:, a:a + n], preferred_element_type=F32) + b_ref[:, a:a + n]

    def put(a, val):
        o_ref[:, a:a + val.shape[1]] = val.astype(BF16)

    lr = proj(W_LR, LANES)
    aq_raw = proj(W_AQ, ATT_Q)
    akv = proj(W_AK, 2 * ATT_KV)
    put(P_SIG, jax.nn.sigmoid(proj(W_M, 1024)))

    lr_hi = lr.astype(BF16)
    lr_lo = (lr - lr_hi.astype(F32)).astype(BF16)
    wuh = wuh_ref[...]
    z = (jnp.dot(lr_hi, wuh, preferred_element_type=F32)
         + jnp.dot(lr_lo, wuh, preferred_element_type=F32)
         + jnp.dot(lr_hi, wul_ref[...], preferred_element_type=F32)) + bup_ref[...]
    put(P_SIG + 1024, jax.nn.sigmoid(proj(W_M + 1024, 1024)))

    cosv = jnp.where(is_ctx, 1.0, cos_ref[...])
    sinv = jnp.where(is_ctx, 0.0, sin_ref[...])
    bd = bd_ref[...]
    aq = _head_norm(aq_raw, qg_ref[...], bd)
    put(P_AQ, _rope(aq, cosv, sinv) * Q_PRESCALE)
    ak = _head_norm(akv[:, :ATT_KV], kg_ref[...], bd[:ATT_KV, :ATT_KV])
    put(P_AKV, _rope(ak, cosv[:, :ATT_KV], sinv[:, :ATT_KV]))
    put(P_AKV + ATT_KV, akv[:, ATT_KV:])
    put(P_SIG + 2048, jax.nn.sigmoid(proj(W_M + 2048, 1024)))

    la = (jnp.minimum(z, 0.0) - jnp.log(1.0 + jnp.exp(-jnp.abs(z)))) * (1.0 / GLA_GATE_NORM)
    rowc = lax.broadcasted_iota(jnp.int32, (TM, GLA_QK), 0) & (GLA_CHUNK - 1)
    cf = la[:, :GLA_QK]
    cb = la[:, GLA_QK:]
    sh = 1
    while sh < GLA_CHUNK:
        cf = cf + jnp.where(rowc >= sh, pltpu.roll(cf, sh, axis=0), 0.0)
        cb = cb + jnp.where(rowc < GLA_CHUNK - sh, pltpu.roll(cb, TM - sh, axis=0), 0.0)
        sh *= 2
    n_ch = TM // GLA_CHUNK
    last_f = [cf[c * GLA_CHUNK + GLA_CHUNK - 1:(c + 1) * GLA_CHUNK, :] for c in range(n_ch)]
    last_b = [cb[c * GLA_CHUNK:c * GLA_CHUNK + 1, :] for c in range(n_ch)]
    dec_ref[0] = jnp.concatenate(last_f, axis=0)
    dec_ref[1] = jnp.concatenate(last_b, axis=0)
    qk = proj(W_GLA, 512)
    gq = qk[:, :GLA_QK] * (GLA_DK ** -0.5)
    gk = qk[:, GLA_QK:]
    for dd, (cum, last) in enumerate(((cf, last_f), (cb, last_b))):
        tot = jnp.concatenate([jnp.broadcast_to(t, (GLA_CHUNK, GLA_QK)) for t in last], axis=0)
        base = P_GD + dd * GD_W
        put(base, gq * jnp.exp(cum))
        put(base + GLA_QK, gk * jnp.exp(-cum))
        put(base + 2 * GLA_QK, gk * jnp.exp(tot - cum))

    put(P_U, proj(W_CONV, 512) * jax.nn.sigmoid(proj(W_CONV + 512, 512)))
    put(P_CG, _silu(proj(W_CONV + 1024, 512)))
    put(P_GG, _silu(proj(W_GLA + 1024, 512)))
    put(P_AG, _silu(proj(W_AG, ATT_Q)))
    put(P_GV, proj(W_GLA + 512, 512))


def _stream_specs(ctx_off):
    return [pl.BlockSpec((TM, D_MODEL), lambda i: (jnp.minimum(i, N_XT - 1), 0)),
            pl.BlockSpec((TM, D_MODEL), lambda i: (jnp.maximum(i - N_XT, 0) + ctx_off, 0))]


def _inproj(l, xs, cs, ctx_off, mod, norm_g, w_p, b_p, cos_t, sin_t, qg, kg, bd, wuh, wul, bup):
    const = lambda i: (0, 0)
    pos = lambda i: (jnp.where(i >= N_XT, 0, i % TILES_PER_SEQ), 0)
    return pl.pallas_call(
        _inproj_kernel,
        grid=(M_ALL // TM,),
        in_specs=_stream_specs(ctx_off) + [
                  pl.BlockSpec((None, 16, 3 * D_MODEL), lambda i: (l, 0, 0)),
                  pl.BlockSpec((None, 1, D_MODEL), lambda i: (l, 0, 0)),
                  pl.BlockSpec((None, D_MODEL, NW), lambda i: (l, 0, 0),
                               pipeline_mode=pl.Buffered(1)),
                  pl.BlockSpec((None, 1, NW), lambda i: (l, 0, 0)),
                  pl.BlockSpec((TM, ATT_Q), pos),
                  pl.BlockSpec((TM, ATT_Q), pos),
                  pl.BlockSpec((None, 1, ATT_Q), lambda i: (l, 0, 0)),
                  pl.BlockSpec((None, 1, ATT_KV), lambda i: (l, 0, 0)),
                  pl.BlockSpec((ATT_Q, ATT_Q), const),
                  pl.BlockSpec((None, LANES, 2 * GLA_QK), lambda i: (l, 0, 0)),
                  pl.BlockSpec((None, LANES, 2 * GLA_QK), lambda i: (l, 0, 0)),
                  pl.BlockSpec((None, 1, 2 * GLA_QK), lambda i: (l, 0, 0))],
        out_specs=[pl.BlockSpec((TM, NP), lambda i: (i, 0)),
                   pl.BlockSpec((2, TM // GLA_CHUNK, GLA_QK), lambda i: (0, i, 0))],
        out_shape=[jax.ShapeDtypeStruct((M_ALL, NP), BF16),
                   jax.ShapeDtypeStruct((2, M_ALL // GLA_CHUNK, GLA_QK), F32)],
        compiler_params=_cparams(1),
        name="inproj",
    )(xs, cs, mod, norm_g, w_p, b_p, cos_t, sin_t, qg, kg, bd, wuh, wul, bup)


def _conv_kernel(u_ref, ul_ref, ur_ref, sg_ref, dw_ref, dwb_ref, lng_ref, lnb_ref, sh_ref,
                 o_ref, win_ref, ext_ref):
    i = pl.program_id(0)
    j = i % BLK_PER_SEQ
    is_x = i < XB
    left_ok = jnp.logical_and(is_x, j != 0)
    right_ok = jnp.logical_and(is_x, j != BLK_PER_SEQ - 1)
    zero_h = jnp.zeros((HALO, D_CONV), BF16)
    win_ref[0:HALO, :] = jnp.where(left_ok, ul_ref[...], zero_h)
    win_ref[HALO:HALO + TC, :] = u_ref[...]
    win_ref[HALO + TC:, :] = jnp.where(right_ok, ur_ref[...], zero_h)
    ext_ref[0] = win_ref[...].astype(F32)
    half = TC // 2
    for r in range(1, SUBLANES):
        for a in range(2):
            ext_ref[r, a * half:a * half + CONV_SH, :] = jnp.dot(
                sh_ref[r], win_ref[a * half:a * half + CONV_SH, :], preferred_element_type=F32)
    bias = dwb_ref[...]
    lng = lng_ref[...]
    lnb = lnb_ref[...]
    for c in range(TC // CONV_RC):
        r0 = c * CONV_RC
        acc = jnp.broadcast_to(bias, (CONV_RC, D_CONV))
        for t in range(CONV_WIDTH):
            off = t + HALO - CONV_PAD
            a0 = r0 + (off // SUBLANES) * SUBLANES
            w_t = jnp.concatenate([dw_ref[t]] * (CONV_RC // SUBLANES), axis=0)
            acc = acc + ext_ref[off % SUBLANES, a0:a0 + CONV_RC, :] * w_t
        mu = jnp.mean(acc, axis=-1, keepdims=True)
        d = acc - mu
        var = jnp.mean(d * d, axis=-1, keepdims=True)
        yn = d * lax.rsqrt(var + EPS) * lng + lnb
        o_ref[r0:r0 + CONV_RC, :] = (_silu(yn) * sg_ref[r0:r0 + CONV_RC, :].astype(F32)).astype(BF16)


def _conv(l, p, dw_w, dw_b, ln_g, ln_b, shifts):
    nhb = M_ALL // HALO
    per = TC // HALO
    cu = P_U // D_CONV
    cg = P_CG // D_CONV
    vec = lambda i: (l, 0, 0)
    return pl.pallas_call(
        _conv_kernel,
        grid=(M_ALL // TC,),
        in_specs=[pl.BlockSpec((TC, D_CONV), lambda i: (i, cu)),
                  pl.BlockSpec((HALO, D_CONV), lambda i: (jnp.maximum(i * per - 1, 0), cu)),
                  pl.BlockSpec((HALO, D_CONV), lambda i: (jnp.minimum((i + 1) * per, nhb - 1), cu)),
                  pl.BlockSpec((TC, D_CONV), lambda i: (i, cg)),
                  pl.BlockSpec((None, CONV_WIDTH, SUBLANES, D_CONV), lambda i: (l, 0, 0, 0)),
                  pl.BlockSpec((None, 1, D_CONV), vec),
                  pl.BlockSpec((None, 1, D_CONV), vec),
                  pl.BlockSpec((None, 1, D_CONV), vec),
                  pl.BlockSpec((SUBLANES, CONV_SH, CONV_SH), lambda i: (0, 0, 0))],
        out_specs=pl.BlockSpec((TC, D_CONV), lambda i: (i, 0)),
        out_shape=jax.ShapeDtypeStruct((M_ALL, D_CONV), BF16),
        scratch_shapes=[pltpu.VMEM((TC + 2 * HALO, D_CONV), BF16),
                        pltpu.VMEM((SUBLANES, TC + 2 * HALO, D_CONV), F32)],
        compiler_params=_cparams(1),
        name="conv",
    )(p, p, p, p, dw_w, dw_b, ln_g, ln_b, shifts)


def _gla_kernel(qkk_ref, v_ref, dec_ref, tri_ref, o_ref, st_ref):
    d = pl.program_id(1)
    step = pl.program_id(2)

    @pl.when(step == 0)
    def _():
        st_ref[...] = jnp.zeros_like(st_ref)

    tri4 = jnp.concatenate([tri_ref[...]] * GLA_HEADS, axis=0) > 0.5
    lane_h = lax.broadcasted_iota(jnp.int32, (GLA_CHUNK, GLA_QK), 1) // GLA_DK
    srow_h = lax.broadcasted_iota(jnp.int32, (GLA_V, GLA_QK), 0) // GLA_DV
    scol_h = lax.broadcasted_iota(jnp.int32, (GLA_V, GLA_QK), 1) // GLA_DK
    smask = srow_h == scol_h
    nt = (((1,), (1,)), ((), ()))
    n_chunks = TC // GLA_CHUNK
    chunks = []
    for n in range(n_chunks):
        c = jnp.where(d == 0, n, n_chunks - 1 - n)
        rows = pl.ds(pl.multiple_of(c * GLA_CHUNK, GLA_CHUNK), GLA_CHUNK)
        q_in = qkk_ref[rows, 0:GLA_QK]
        k_in = qkk_ref[rows, GLA_QK:2 * GLA_QK]
        k_st = qkk_ref[rows, 2 * GLA_QK:3 * GLA_QK]
        v = v_ref[rows, :]
        decay = jnp.exp(dec_ref[pl.ds(c, 1), :])
        q_stack = jnp.concatenate(
            [jnp.where(lane_h == h, q_in, jnp.zeros_like(q_in)) for h in range(GLA_HEADS)], axis=0)
        att = lax.dot_general(q_stack, k_in, nt, preferred_element_type=F32)
        att = jnp.where(tri4, att, 0.0).astype(BF16)
        kvt = lax.dot_general(v, k_st, (((0,), (0,)), ((), ())), preferred_element_type=F32)
        chunks.append((rows, q_in, v, decay, att, jnp.where(smask, kvt, 0.0)))
    for rows, q_in, v, decay, att, kvt in chunks:
        st_old = st_ref[...]
        o_inter = lax.dot_general(q_in, st_old.astype(BF16), nt, preferred_element_type=F32)
        o_intra = jnp.concatenate(
            [jnp.dot(att[h * GLA_CHUNK:(h + 1) * GLA_CHUNK, :],
                     v[:, h * GLA_DV:(h + 1) * GLA_DV], preferred_element_type=F32)
             for h in range(GLA_HEADS)], axis=1)
        o_ref[rows, :] = (o_intra + o_inter).astype(BF16)
        st_ref[...] = st_old * decay + kvt


def _gla(p, dec, tri):
    def rb(b, d, s):
        xblk = b * BLK_PER_SEQ + jnp.where(d == 0, s - 1, BLK_PER_SEQ - s)
        return jnp.where(s == 0, XB + b, xblk)

    cgd = P_GD // GD_W
    cv = P_GV // GLA_V
    n_ch = TC // GLA_CHUNK
    dec4 = dec.reshape(2, M_ALL // TC, n_ch, GLA_QK)
    return pl.pallas_call(
        _gla_kernel,
        grid=(BATCH, 2, BLK_PER_SEQ + 1),
        in_specs=[pl.BlockSpec((TC, GD_W), lambda b, d, s: (rb(b, d, s), cgd + d)),
                  pl.BlockSpec((TC, GLA_V), lambda b, d, s: (rb(b, d, s), cv)),
                  pl.BlockSpec((None, None, n_ch, GLA_QK), lambda b, d, s: (d, rb(b, d, s), 0, 0)),
                  pl.BlockSpec((None, GLA_CHUNK, GLA_CHUNK), lambda b, d, s: (d, 0, 0))],
        out_specs=pl.BlockSpec((None, TC, GLA_V), lambda b, d, s: (d, rb(b, d, s), 0)),
        out_shape=jax.ShapeDtypeStruct((2, M_ALL, GLA_V), BF16),
        scratch_shapes=[pltpu.VMEM((GLA_V, GLA_QK), F32)],
        compiler_params=_cparams(3),
        name="gla",
    )(p, p, dec4, tri)


NKEYS = CTX_LEN + SEQ
VROWS = 80
NT_DIMS = (((1,), (1,)), ((), ()))


def _attn_kernel(q_ref, g_ref, ckv_ref, xkv_ref, selk_ref, selvt_ref, eye_ref, o_ref,
                 kz_ref, vtc_ref, vtx_ref, qt_ref, m_ref, al_ref, acc_ref, s_ref):
    qi = pl.program_id(1)
    row_v = lax.broadcasted_iota(jnp.int32, (VROWS, 1), 0)

    def ext_values(svt, blk):
        vt = lax.dot_general(svt, blk, NT_DIMS, preferred_element_type=F32)
        return jnp.where(row_v == ATT_HD, 1.0, vt).astype(BF16)

    @pl.when(qi == 0)
    def _():
        for h in range(ATT_KV_HEADS):
            for e in range(2):
                sk = selk_ref[h, e]
                for (src, r0, n) in ((ckv_ref, 0, CTX_LEN), (xkv_ref, CTX_LEN, SEQ)):
                    for r in range(0, n, 1024):
                        nr = min(1024, n - r)
                        kz_ref[h, e, r0 + r:r0 + r + nr, :] = jnp.dot(
                            src[r:r + nr, :], sk, preferred_element_type=F32).astype(BF16)
            svt = selvt_ref[h]
            vtc_ref[h] = ext_values(svt, ckv_ref[...])
            for t in range(SEQ // KB):
                vtx_ref[h, t] = ext_values(svt, xkv_ref[t * KB:(t + 1) * KB, :])

    eye = eye_ref[...]
    for pr in range(ATT_HEADS // 2):
        qt_ref[pr] = lax.dot_general(eye, q_ref[:, pr * LANES:(pr + 1) * LANES], NT_DIMS,
                                     preferred_element_type=F32).astype(BF16)
    m_ref[...] = jnp.full_like(m_ref, -jnp.inf)
    acc_ref[...] = jnp.zeros_like(acc_ref)

    def scores(nxt, hd):
        slot, k0, nk, _ = nxt
        h, pr, e = hd // ATT_GROUP, hd // 2, hd % 2
        s_ref[slot, hd, 0:nk, :] = jnp.dot(kz_ref[h, e, pl.ds(k0, nk), :], qt_ref[pr],
                                           preferred_element_type=F32)

    def stage(cur, nxt):
        if cur is not None:
            slot, _, nk, vt_of = cur
            for hd in range(ATT_HEADS):
                s3 = s_ref[slot, hd, 0:nk, :].reshape(nk // SUBLANES, SUBLANES, TC)
                m_col = jnp.max(jnp.max(s3, axis=0), axis=0, keepdims=True)
                m_prev = m_ref[hd]
                m_next = jnp.maximum(m_prev, m_col)
                al_ref[hd] = jnp.exp2(m_prev - m_next)
                m_ref[hd] = m_next
        for hd in range(ATT_HEADS):
            if nxt is not None:
                scores(nxt, hd)
            if cur is not None:
                s3 = s_ref[slot, hd, 0:nk, :].reshape(nk // SUBLANES, SUBLANES, TC)
                p = jnp.exp2(s3 - m_ref[hd][None]).reshape(nk, TC).astype(BF16)
                pv = jnp.dot(vt_of(hd // ATT_GROUP), p, preferred_element_type=F32)
                acc3 = acc_ref[hd].reshape(VROWS // SUBLANES, SUBLANES, TC) * al_ref[hd][None]
                acc_ref[hd] = acc3.reshape(VROWS, TC) + pv

    n_xb = SEQ // KB
    ctx_stage = (1, 0, CTX_LEN, lambda h: vtc_ref[h])

    def x_stage(t, slot):
        k0 = CTX_LEN + t * KB
        if not isinstance(t, int):
            k0 = pl.multiple_of(k0, CTX_LEN)
        return (slot, k0, KB, lambda h: vtx_ref[h, t])

    stage(None, ctx_stage)

    @pl.when(qi == 0)
    def _():
        stage(ctx_stage, None)

    @pl.when(qi > 0)
    def _():
        stage(ctx_stage, x_stage(0, 0))

        def body(j, carry):
            t = 2 * j
            stage(x_stage(t, 0), x_stage(t + 1, 1))
            stage(x_stage(t + 1, 1), x_stage(t + 2, 0))
            return carry
        lax.fori_loop(0, (n_xb - 2) // 2, body, 0)
        stage(x_stage(n_xb - 2, 0), x_stage(n_xb - 1, 1))
        stage(x_stage(n_xb - 1, 1), None)

    outs = []
    for hd in range(ATT_HEADS):
        a = acc_ref[hd]
        outs.append(a[0:ATT_HD, :] / a[ATT_HD:ATT_HD + 1, :])
    o_nat = jnp.concatenate(outs, axis=0).T
    o_ref[...] = (o_nat * g_ref[...].astype(F32)).astype(BF16)


def _attn(p, selk, selvt, eye):
    cq = P_AQ // ATT_Q
    cg = P_AG // ATT_Q
    ckv = P_AKV // (2 * ATT_KV)

    def qrow(b, qi):
        return jnp.where(qi == 0, XB + b, b * BLK_PER_SEQ + qi - 1)

    return pl.pallas_call(
        _attn_kernel,
        grid=(BATCH, BLK_PER_SEQ + 1),
        in_specs=[pl.BlockSpec((TC, ATT_Q), lambda b, qi: (qrow(b, qi), cq)),
                  pl.BlockSpec((TC, ATT_Q), lambda b, qi: (qrow(b, qi), cg)),
                  pl.BlockSpec((CTX_LEN, 2 * ATT_KV), lambda b, qi: (XB + b, ckv)),
                  pl.BlockSpec((SEQ, 2 * ATT_KV), lambda b, qi: (b, ckv)),
                  pl.BlockSpec((ATT_KV_HEADS, 2, 2 * ATT_KV, LANES), lambda b, qi: (0, 0, 0, 0)),
                  pl.BlockSpec((ATT_KV_HEADS, VROWS, 2 * ATT_KV), lambda b, qi: (0, 0, 0)),
                  pl.BlockSpec((LANES, LANES), lambda b, qi: (0, 0))],
        out_specs=pl.BlockSpec((TC, ATT_Q), lambda b, qi: (qrow(b, qi), 0)),
        out_shape=jax.ShapeDtypeStruct((M_ALL, ATT_Q), BF16),
        scratch_shapes=[pltpu.VMEM((ATT_KV_HEADS, 2, NKEYS, LANES), BF16),
                        pltpu.VMEM((ATT_KV_HEADS, VROWS, CTX_LEN), BF16),
                        pltpu.VMEM((ATT_KV_HEADS, SEQ // KB, VROWS, KB), BF16),
                        pltpu.VMEM((ATT_HEADS // 2, LANES, TC), BF16),
                        pltpu.VMEM((ATT_HEADS, SUBLANES, TC), F32),
                        pltpu.VMEM((ATT_HEADS, SUBLANES, TC), F32),
                        pltpu.VMEM((ATT_HEADS, VROWS, TC), F32),
                        pltpu.VMEM((2, ATT_HEADS, KB, TC), F32)],
        compiler_params=_cparams(2),
        name="attention",
    )(p, p, p, p, selk, selvt, eye)


def _final_kernel(sx_ref, sc_ref, sig_ref, gg_ref, ua_ref, of_ref, ob_ref, oc_ref, mod_ref,
                  wc_ref, wg_ref, wa_ref, wo_ref, gn_ref, o_ref):
    i = pl.program_id(0)
    row = jnp.where(i >= N_XT, BATCH, i // TILES_PER_SEQ)
    ya = jnp.dot(ua_ref[...], wc_ref[...], preferred_element_type=F32)
    og = of_ref[...].astype(F32) + ob_ref[...].astype(F32)
    gn = gn_ref[...]
    parts = []
    for h in range(GLA_HEADS):
        oh = og[:, h * GLA_DV:(h + 1) * GLA_DV]
        ms = jnp.mean(oh * oh, axis=-1, keepdims=True)
        parts.append(oh * lax.rsqrt(ms + EPS) * gn)
    on = jnp.concatenate(parts, axis=1) * gg_ref[...].astype(F32)
    yb = jnp.dot(on.astype(BF16), wg_ref[...], preferred_element_type=F32)
    yc = jnp.dot(oc_ref[...], wa_ref[...], preferred_element_type=F32)
    merged = (sig_ref[:, 0:D_MODEL].astype(F32) * ya
              + sig_ref[:, D_MODEL:2 * D_MODEL].astype(F32) * yb
              + sig_ref[:, 2 * D_MODEL:3 * D_MODEL].astype(F32) * yc)
    out = jnp.dot(merged.astype(BF16), wo_ref[...], preferred_element_type=F32)
    gate = mod_ref[pl.ds(row, 1), 2 * D_MODEL:3 * D_MODEL]
    o_ref[...] = jnp.where(i >= N_XT, sc_ref[...], sx_ref[...]) + gate * out


def _final(l, n_tiles, xs, cs, ctx_off, p, ua, o_gla, oc, mod, wc, wg, wa, wo, gn):
    rowblk = lambda i: (i, 0)
    wspec = lambda k: pl.BlockSpec((None, k, D_MODEL), lambda i: (l, 0, 0))
    return pl.pallas_call(
        _final_kernel,
        grid=(n_tiles,),
        in_specs=_stream_specs(ctx_off) + [
                  pl.BlockSpec((TM, 3 * D_MODEL), lambda i: (i, P_SIG // (3 * D_MODEL))),
                  pl.BlockSpec((TM, GLA_V), lambda i: (i, P_GG // GLA_V)),
                  pl.BlockSpec((TM, D_CONV), rowblk),
                  pl.BlockSpec((None, TM, GLA_V), lambda i: (0, i, 0)),
                  pl.BlockSpec((None, TM, GLA_V), lambda i: (1, i, 0)),
                  pl.BlockSpec((TM, ATT_Q), rowblk),
                  pl.BlockSpec((None, 16, 3 * D_MODEL), lambda i: (l, 0, 0)),
                  wspec(D_CONV), wspec(GLA_V), wspec(ATT_Q), wspec(D_MODEL),
                  pl.BlockSpec((None, 1, GLA_DV), lambda i: (l, 0, 0))],
        out_specs=pl.BlockSpec((TM, D_MODEL), rowblk),
        out_shape=jax.ShapeDtypeStruct((n_tiles * TM, D_MODEL), F32),
        compiler_params=_cparams(1),
        name="merge_out",
    )(xs, cs, p, p, ua, o_gla, o_gla, oc, mod, wc, wg, wa, wo, gn)


def _rope_tables():
    t = np.arange(SEQ)
    row = (t // GRID_W).astype(np.float32)
    col = (t % GRID_W).astype(np.float32)
    n_freq = ROPE_AXIS_DIM // 2
    freqs = (np.float32(ROPE_THETA) ** (-np.arange(n_freq, dtype=np.float32) / n_freq)).astype(np.float32)
    ar = row[:, None] * freqs
    ac = col[:, None] * freqs
    cos64 = np.concatenate([np.cos(ar), np.cos(ar), np.cos(ac), np.cos(ac)], axis=1)
    sin64 = np.concatenate([-np.sin(ar), np.sin(ar), -np.sin(ac), np.sin(ac)], axis=1)
    return (np.tile(cos64, (1, ATT_HEADS)).astype(np.float32),
            np.tile(sin64, (1, ATT_HEADS)).astype(np.float32))


def _static_tables():
    bd = np.kron(np.eye(ATT_HEADS, dtype=np.float32), np.ones((ATT_HD, ATT_HD), np.float32))
    idx = np.arange(GLA_CHUNK)
    tri = np.stack([(idx[None, :] <= idx[:, None]), (idx[None, :] >= idx[:, None])]).astype(np.float32)
    selk = np.zeros((ATT_KV_HEADS, 2, 2 * ATT_KV, LANES), np.float32)
    selvt = np.zeros((ATT_KV_HEADS, VROWS, 2 * ATT_KV), np.float32)
    for h in range(ATT_KV_HEADS):
        for dd in range(ATT_HD):
            selvt[h, dd, ATT_KV + h * ATT_HD + dd] = 1.0
            for e in range(2):
                selk[h, e, h * ATT_HD + dd, e * ATT_HD + dd] = 1.0
    shifts = np.stack([np.eye(CONV_SH, k=r, dtype=np.float32) for r in range(SUBLANES)])
    return bd, tri, selk, selvt, shifts


def _permute_cols(a):
    pieces = [a[..., O_MA:N_IN], a[..., O_VAL:O_GQ], a[..., O_GQ:O_LR], a[..., O_AQ:O_AK],
              a[..., O_AG:O_MA], a[..., O_AK:O_AG], a[..., O_LR:O_AQ]]
    pad = jnp.zeros(a.shape[:-1] + (NW - N_IN,), a.dtype)
    return jnp.concatenate(pieces + [pad], axis=-1)


def kernel(x, c, ctx, c_ctx, norm_g, w_mod, b_mod, w_in, b_in, conv_dw_w, conv_dw_b, conv_ln_g,
           conv_ln_b, w_conv_out, gla_w_gate, gla_b_gate, gla_norm_g, w_gla_out, q_norm_g,
           k_norm_g, w_attn_out, w_out):
    cos_np, sin_np = _rope_tables()
    bd_np, tri_np, selk_np, selvt_np, shifts_np = _static_tables()
    shifts = jnp.asarray(shifts_np, BF16)
    cos_t, sin_t = jnp.asarray(cos_np), jnp.asarray(sin_np)
    bd = jnp.asarray(bd_np, BF16)
    tri = jnp.asarray(tri_np)
    selk = jnp.asarray(selk_np, BF16)
    selvt = jnp.asarray(selvt_np, BF16)
    eye = jnp.eye(LANES, dtype=BF16)

    w_p = _permute_cols(w_in.astype(BF16))
    b_p = _permute_cols(b_in).reshape(DEPTH, 1, NW)
    cc = jnp.concatenate([c, c_ctx[None, :], jnp.zeros((16 - BATCH - 1, D_MODEL), F32)], axis=0)
    dw_w = jnp.broadcast_to(conv_dw_w[:, :, None, :], (DEPTH, CONV_WIDTH, SUBLANES, D_CONV))
    r3 = lambda a: a.reshape(DEPTH, 1, a.shape[-1])
    wup = jnp.zeros((DEPTH, LANES, 2 * GLA_QK), F32)
    wup = wup.at[:, 0:GLA_RANK, 0:GLA_QK].set(gla_w_gate[:, 0])
    wup = wup.at[:, GLA_RANK:2 * GLA_RANK, GLA_QK:].set(gla_w_gate[:, 1])
    wuh = wup.astype(BF16)
    wul = (wup - wuh.astype(F32)).astype(BF16)
    bup = gla_b_gate.reshape(DEPTH, 1, 2 * GLA_QK)
    qg = jnp.tile(q_norm_g, (1, ATT_HEADS)).reshape(DEPTH, 1, ATT_Q)
    kg = jnp.tile(k_norm_g, (1, ATT_KV_HEADS)).reshape(DEPTH, 1, ATT_KV)
    wc, wg, wa, wo = (w.astype(BF16) for w in (w_conv_out, w_gla_out, w_attn_out, w_out))

    mod = _modulation(cc, w_mod, b_mod)
    xs, cs, ctx_off = x.reshape(MX, D_MODEL), ctx.reshape(MC, D_MODEL), 0
    for l in range(DEPTH):
        p, dec = _inproj(l, xs, cs, ctx_off, mod, r3(norm_g), w_p, b_p, cos_t, sin_t, qg, kg, bd, wuh, wul, bup)
        ua = _conv(l, p, dw_w, r3(conv_dw_b), r3(conv_ln_g), r3(conv_ln_b), shifts)
        o_gla = _gla(p, dec, tri)
        oc = _attn(p, selk, selvt, eye)
        n_tiles = M_ALL // TM if l < DEPTH - 1 else N_XT
        xs = _final(l, n_tiles, xs, cs, ctx_off, p, ua, o_gla, oc, mod, wc, wg, wa, wo, r3(gla_norm_g))
        cs, ctx_off = xs, N_XT
    return xs.reshape(BATCH, SEQ, D_MODEL)
```

```python
import functools
import math

import numpy as np
import jax
import jax.numpy as jnp
from jax import lax
from jax.experimental import pallas as pl
from jax.experimental.pallas import tpu as pltpu

F32 = jnp.float32
BF16 = jnp.bfloat16
HIGHEST = lax.Precision.HIGHEST

D_MODEL = 1024
BATCH = 8
SEQ = 4096
DEPTH = 4
CTX_LEN = 256
GRID_W = 64
EPS = 1e-6
D_CONV = 512
CONV_WIDTH = 31
CONV_PAD = CONV_WIDTH // 2
GLA_HEADS = 4
GLA_DK = 64
GLA_DV = 128
GLA_QK = GLA_HEADS * GLA_DK
GLA_V = GLA_HEADS * GLA_DV
GLA_RANK = 16
GLA_GATE_NORM = 16.0
GLA_CHUNK = 64
ATT_HEADS = 8
ATT_KV_HEADS = 2
ATT_GROUP = ATT_HEADS // ATT_KV_HEADS
ATT_HD = 64
ATT_Q = ATT_HEADS * ATT_HD
ATT_KV = ATT_KV_HEADS * ATT_HD
ROPE_AXIS_DIM = ATT_HD // 2
ROPE_THETA = 10000.0

LANES = 128
SUBLANES = 8
MX = BATCH * SEQ
MC = BATCH * CTX_LEN
M_ALL = MX + MC

O_VAL, O_GLU, O_CGATE = 0, 512, 1024
O_GQ, O_GK, O_GV, O_GG = 1536, 1792, 2048, 2560
O_LR = 3072
O_AQ, O_AK, O_AV, O_AG = 3104, 3616, 3744, 3872
O_MA = 4384
N_IN = 7456

A_VAL, A_GLU, A_CG, A_GQK, A_GV, A_GG = 0, 512, 1024, 1536, 2048, 2560
NA = O_LR
B_AQ, B_AKV, B_AG, B_M = 0, 512, 768, 1280
NB = N_IN - O_AQ

P_SIG = 0
P_U = 3072
P_CG = 3584
P_GV = 4096
P_GD = 4608
GD_W = 3 * GLA_QK
P_GG = 6144
P_AQ = 6656
P_AG = 7168
P_AKV = 7680
NP = 7936

TM = 512
N_XT = MX // TM
N_CT = MC // TM
TILES_PER_SEQ = SEQ // TM
TC = 256
XB = MX // TC
BLK_PER_SEQ = SEQ // TC
HALO = 16
CONV_RC = 32
CONV_SH = TC // 2 + 2 * HALO
KB = 512
Q_PRESCALE = (ATT_HD ** -0.5) * math.log2(math.e)

VMEM_LIMIT = 56 * 1024 * 1024


def _cparams(n_axes, vmem=VMEM_LIMIT):
    return pltpu.CompilerParams(dimension_semantics=("arbitrary",) * n_axes,
                                vmem_limit_bytes=vmem)


def _silu(x):
    return x * jax.nn.sigmoid(x)


def _mod_kernel(c_ref, w_ref, b_ref, o_ref):
    s = _silu(c_ref[...])
    o_ref[...] = jnp.dot(s, w_ref[...], preferred_element_type=F32, precision=HIGHEST) + b_ref[...]


def _modulation(cc, w_mod, b_mod):
    nt = 3 * D_MODEL // 1024
    return pl.pallas_call(
        _mod_kernel,
        grid=(DEPTH, nt),
        in_specs=[pl.BlockSpec((16, D_MODEL), lambda l, n: (0, 0)),
                  pl.BlockSpec((None, D_MODEL, 1024), lambda l, n: (l, 0, n)),
                  pl.BlockSpec((None, 1, 1024), lambda l, n: (l, 0, n))],
        out_specs=pl.BlockSpec((None, 16, 1024), lambda l, n: (l, 0, n)),
        out_shape=jax.ShapeDtypeStruct((DEPTH, 16, 3 * D_MODEL), F32),
        compiler_params=_cparams(2),
        name="modulation",
    )(cc, w_mod, b_mod.reshape(DEPTH, 1, 3 * D_MODEL))


def _head_norm(xv, gain, bd):
    ss = jnp.dot((xv * xv).astype(BF16), bd, preferred_element_type=F32)
    return xv * lax.rsqrt(ss * (1.0 / ATT_HD) + EPS) * gain


def _rope(xv, cosv, sinv):
    parts = []
    for s in range(xv.shape[1] // LANES):
        sl = slice(s * LANES, (s + 1) * LANES)
        xs = xv[:, sl]
        up = pltpu.roll(xs, LANES - 16, axis=1)
        dn = pltpu.roll(xs, 16, axis=1)
        lane = lax.broadcasted_iota(jnp.int32, xs.shape, 1)
        partner = jnp.where((lane & 16) == 0, up, dn)
        parts.append(xs * cosv[:, sl] + partner * sinv[:, sl])
    return jnp.concatenate(parts, axis=1) if len(parts) > 1 else parts[0]


def _inproj_kernel(x_ref, c_ref, mod_ref, g_ref, wa_ref, wb_ref, wc_ref, ba_ref, bb_ref, bc_ref,
                   cos_ref, sin_ref, qg_ref, kg_ref,
                   bd_ref, wuh_ref, wul_ref, bup_ref, o_ref, dec_ref):
    i = pl.program_id(0)
    is_ctx = i >= N_XT
    row = jnp.where(is_ctx, BATCH, i // TILES_PER_SEQ)
    x = jnp.where(is_ctx, c_ref[...], x_ref[...])
    ms = jnp.mean(x * x, axis=-1, keepdims=True)
    y = x * lax.rsqrt(ms + EPS) * g_ref[...]
    m = mod_ref[pl.ds(row, 1), :]
    shift = m[:, 0:D_MODEL]
    scale = m[:, D_MODEL:2 * D_MODEL]
    h = (y * (1.0 + scale) + shift).astype(BF16)

    def proj_from(w_ref, b_ref):
        def proj(a, n):
            return jnp.dot(h, w_ref[:, a:a + n], preferred_element_type=F32) + b_ref[:, a:a + n]
        return proj

    proj_a = proj_from(wa_ref, ba_ref)
    proj_b = proj_from(wb_ref, bb_ref)
    proj_c = proj_from(wc_ref, bc_ref)

    def put(a, val):
        o_ref[:, a:a + val.shape[1]] = val.astype(BF16)

    lr = proj_c(0, LANES)
    aq_raw = proj_b(B_AQ, ATT_Q)
    akv = proj_b(B_AKV, 2 * ATT_KV)
    put(P_SIG, jax.nn.sigmoid(proj_b(B_M, 1024)))

    lr_hi = lr.astype(BF16)
    lr_lo = (lr - lr_hi.astype(F32)).astype(BF16)
    wuh = wuh_ref[...]
    z = (jnp.dot(lr_hi, wuh, preferred_element_type=F32)
         + jnp.dot(lr_lo, wuh, preferred_element_type=F32)
         + jnp.dot(lr_hi, wul_ref[...], preferred_element_type=F32)) + bup_ref[...]
    put(P_SIG + 1024, jax.nn.sigmoid(proj_b(B_M + 1024, 1024)))

    cosv = jnp.where(is_ctx, 1.0, cos_ref[...])
    sinv = jnp.where(is_ctx, 0.0, sin_ref[...])
    bd = bd_ref[...]
    aq = _head_norm(aq_raw, qg_ref[...], bd)
    put(P_AQ, _rope(aq, cosv, sinv) * Q_PRESCALE)
    ak = _head_norm(akv[:, :ATT_KV], kg_ref[...], bd[:ATT_KV, :ATT_KV])
    put(P_AKV, _rope(ak, cosv[:, :ATT_KV], sinv[:, :ATT_KV]))
    put(P_AKV + ATT_KV, akv[:, ATT_KV:])
    put(P_SIG + 2048, jax.nn.sigmoid(proj_b(B_M + 2048, 1024)))

    la = (jnp.minimum(z, 0.0) - jnp.log(1.0 + jnp.exp(-jnp.abs(z)))) * (1.0 / GLA_GATE_NORM)
    rowc = lax.broadcasted_iota(jnp.int32, (TM, GLA_QK), 0) & (GLA_CHUNK - 1)
    cf = la[:, :GLA_QK]
    cb = la[:, GLA_QK:]
    sh = 1
    while sh < GLA_CHUNK:
        cf = cf + jnp.where(rowc >= sh, pltpu.roll(cf, sh, axis=0), 0.0)
        cb = cb + jnp.where(rowc < GLA_CHUNK - sh, pltpu.roll(cb, TM - sh, axis=0), 0.0)
        sh *= 2
    n_ch = TM // GLA_CHUNK
    last_f = [cf[c * GLA_CHUNK + GLA_CHUNK - 1:(c + 1) * GLA_CHUNK, :] for c in range(n_ch)]
    last_b = [cb[c * GLA_CHUNK:c * GLA_CHUNK + 1, :] for c in range(n_ch)]
    dec_ref[0] = jnp.concatenate(last_f, axis=0)
    dec_ref[1] = jnp.concatenate(last_b, axis=0)
    qk = proj_a(A_GQK, 512)
    gq = qk[:, :GLA_QK] * (GLA_DK ** -0.5)
    gk = qk[:, GLA_QK:]
    for dd, (cum, last) in enumerate(((cf, last_f), (cb, last_b))):
        tot = jnp.concatenate([jnp.broadcast_to(t, (GLA_CHUNK, GLA_QK)) for t in last], axis=0)
        base = P_GD + dd * GD_W
        put(base, gq * jnp.exp(cum))
        put(base + GLA_QK, gk * jnp.exp(-cum))
        put(base + 2 * GLA_QK, gk * jnp.exp(tot - cum))

    put(P_U, proj_a(A_VAL, 512) * jax.nn.sigmoid(proj_a(A_GLU, 512)))
    put(P_CG, _silu(proj_a(A_CG, 512)))
    put(P_GG, _silu(proj_a(A_GG, 512)))
    put(P_AG, _silu(proj_b(B_AG, ATT_Q)))
    put(P_GV, proj_a(A_GV, 512))


def _stream_specs(ctx_off):
    return [pl.BlockSpec((TM, D_MODEL), lambda i: (jnp.minimum(i, N_XT - 1), 0)),
            pl.BlockSpec((TM, D_MODEL), lambda i: (jnp.maximum(i - N_XT, 0) + ctx_off, 0))]


def _inproj(l, xs, cs, ctx_off, mod, norm_g, w_abc, b_abc, cos_t, sin_t, qg, kg, bd, wuh, wul, bup):
    const = lambda i: (0, 0)
    pos = lambda i: (jnp.where(i >= N_XT, 0, i % TILES_PER_SEQ), 0)
    return pl.pallas_call(
        _inproj_kernel,
        grid=(M_ALL // TM,),
        in_specs=_stream_specs(ctx_off) + [
                  pl.BlockSpec((None, 16, 3 * D_MODEL), lambda i: (l, 0, 0)),
                  pl.BlockSpec((None, 1, D_MODEL), lambda i: (l, 0, 0)),
                  ] + [pl.BlockSpec((None, D_MODEL, w.shape[-1]), lambda i: (l, 0, 0),
                                    pipeline_mode=pl.Buffered(1)) for w in w_abc
                  ] + [pl.BlockSpec((None, 1, b.shape[-1]), lambda i: (l, 0, 0)) for b in b_abc
                  ] + [
                  pl.BlockSpec((TM, ATT_Q), pos),
                  pl.BlockSpec((TM, ATT_Q), pos),
                  pl.BlockSpec((None, 1, ATT_Q), lambda i: (l, 0, 0)),
                  pl.BlockSpec((None, 1, ATT_KV), lambda i: (l, 0, 0)),
                  pl.BlockSpec((ATT_Q, ATT_Q), const),
                  pl.BlockSpec((None, LANES, 2 * GLA_QK), lambda i: (l, 0, 0)),
                  pl.BlockSpec((None, LANES, 2 * GLA_QK), lambda i: (l, 0, 0)),
                  pl.BlockSpec((None, 1, 2 * GLA_QK), lambda i: (l, 0, 0))],
        out_specs=[pl.BlockSpec((TM, NP), lambda i: (i, 0)),
                   pl.BlockSpec((2, TM // GLA_CHUNK, GLA_QK), lambda i: (0, i, 0))],
        out_shape=[jax.ShapeDtypeStruct((M_ALL, NP), BF16),
                   jax.ShapeDtypeStruct((2, M_ALL // GLA_CHUNK, GLA_QK), F32)],
        compiler_params=_cparams(1),
        name="inproj",
    )(xs, cs, mod, norm_g, *w_abc, *b_abc, cos_t, sin_t, qg, kg, bd, wuh, wul, bup)


def _conv_kernel(u_ref, ul_ref, ur_ref, sg_ref, dw_ref, dwb_ref, lng_ref, lnb_ref, sh_ref,
                 o_ref, win_ref, ext_ref):
    i = pl.program_id(0)
    j = i % BLK_PER_SEQ
    is_x = i < XB
    left_ok = jnp.logical_and(is_x, j != 0)
    right_ok = jnp.logical_and(is_x, j != BLK_PER_SEQ - 1)
    zero_h = jnp.zeros((HALO, D_CONV), BF16)
    win_ref[0:HALO, :] = jnp.where(left_ok, ul_ref[...], zero_h)
    win_ref[HALO:HALO + TC, :] = u_ref[...]
    win_ref[HALO + TC:, :] = jnp.where(right_ok, ur_ref[...], zero_h)
    ext_ref[0] = win_ref[...].astype(F32)
    half = TC // 2
    for r in range(1, SUBLANES):
        for a in range(2):
            ext_ref[r, a * half:a * half + CONV_SH, :] = jnp.dot(
                sh_ref[r], win_ref[a * half:a * half + CONV_SH, :], preferred_element_type=F32)
    bias = dwb_ref[...]
    lng = lng_ref[...]
    lnb = lnb_ref[...]
    for c in range(TC // CONV_RC):
        r0 = c * CONV_RC
        acc = jnp.broadcast_to(bias, (CONV_RC, D_CONV))
        for t in range(CONV_WIDTH):
            off = t + HALO - CONV_PAD
            a0 = r0 + (off // SUBLANES) * SUBLANES
            w_t = jnp.concatenate([dw_ref[t]] * (CONV_RC // SUBLANES), axis=0)
            acc = acc + ext_ref[off % SUBLANES, a0:a0 + CONV_RC, :] * w_t
        mu = jnp.mean(acc, axis=-1, keepdims=True)
        d = acc - mu
        var = jnp.mean(d * d, axis=-1, keepdims=True)
        yn = d * lax.rsqrt(var + EPS) * lng + lnb
        o_ref[r0:r0 + CONV_RC, :] = (_silu(yn) * sg_ref[r0:r0 + CONV_RC, :].astype(F32)).astype(BF16)


def _conv(l, p, dw_w, dw_b, ln_g, ln_b, shifts):
    nhb = M_ALL // HALO
    per = TC // HALO
    cu = P_U // D_CONV
    cg = P_CG // D_CONV
    vec = lambda i: (l, 0, 0)
    return pl.pallas_call(
        _conv_kernel,
        grid=(M_ALL // TC,),
        in_specs=[pl.BlockSpec((TC, D_CONV), lambda i: (i, cu)),
                  pl.BlockSpec((HALO, D_CONV), lambda i: (jnp.maximum(i * per - 1, 0), cu)),
                  pl.BlockSpec((HALO, D_CONV), lambda i: (jnp.minimum((i + 1) * per, nhb - 1), cu)),
                  pl.BlockSpec((TC, D_CONV), lambda i: (i, cg)),
                  pl.BlockSpec((None, CONV_WIDTH, SUBLANES, D_CONV), lambda i: (l, 0, 0, 0)),
                  pl.BlockSpec((None, 1, D_CONV), vec),
                  pl.BlockSpec((None, 1, D_CONV), vec),
                  pl.BlockSpec((None, 1, D_CONV), vec),
                  pl.BlockSpec((SUBLANES, CONV_SH, CONV_SH), lambda i: (0, 0, 0))],
        out_specs=pl.BlockSpec((TC, D_CONV), lambda i: (i, 0)),
        out_shape=jax.ShapeDtypeStruct((M_ALL, D_CONV), BF16),
        scratch_shapes=[pltpu.VMEM((TC + 2 * HALO, D_CONV), BF16),
                        pltpu.VMEM((SUBLANES, TC + 2 * HALO, D_CONV), F32)],
        compiler_params=_cparams(1),
        name="conv",
    )(p, p, p, p, dw_w, dw_b, ln_g, ln_b, shifts)


def _gla_kernel(qf_ref, qb_ref, vf_ref, vb_ref, df_ref, db_ref, tri_ref, of_ref, ob_ref, st_ref):
    step = pl.program_id(1)

    @pl.when(step == 0)
    def _():
        st_ref[...] = jnp.zeros_like(st_ref)

    lane_h = lax.broadcasted_iota(jnp.int32, (GLA_CHUNK, GLA_QK), 1) // GLA_DK
    srow_h = lax.broadcasted_iota(jnp.int32, (GLA_V, GLA_QK), 0) // GLA_DV
    scol_h = lax.broadcasted_iota(jnp.int32, (GLA_V, GLA_QK), 1) // GLA_DK
    smask = srow_h == scol_h
    nt = (((1,), (1,)), ((), ()))
    n_chunks = TC // GLA_CHUNK
    dirs = ((qf_ref, vf_ref, df_ref, of_ref), (qb_ref, vb_ref, db_ref, ob_ref))
    work = []
    for n in range(n_chunks):
        for d, (qkk_ref, v_ref, dec_ref, o_ref) in enumerate(dirs):
            c = n if d == 0 else n_chunks - 1 - n
            rows = slice(c * GLA_CHUNK, (c + 1) * GLA_CHUNK)
            q_in = qkk_ref[rows, 0:GLA_QK]
            k_in = qkk_ref[rows, GLA_QK:2 * GLA_QK]
            k_st = qkk_ref[rows, 2 * GLA_QK:3 * GLA_QK]
            v = v_ref[rows, :]
            decay = jnp.exp(dec_ref[c:c + 1, :])
            tri4 = jnp.concatenate([tri_ref[d]] * GLA_HEADS, axis=0) > 0.5
            q_stack = jnp.concatenate(
                [jnp.where(lane_h == h, q_in, jnp.zeros_like(q_in)) for h in range(GLA_HEADS)],
                axis=0)
            att = lax.dot_general(q_stack, k_in, nt, preferred_element_type=F32)
            att = jnp.where(tri4, att, 0.0).astype(BF16)
            kvt = lax.dot_general(v, k_st, (((0,), (0,)), ((), ())),
                                  preferred_element_type=F32)
            work.append((d, o_ref, rows, q_in, v, decay, att, jnp.where(smask, kvt, 0.0)))
    for d, o_ref, rows, q_in, v, decay, att, kvt in work:
        st_old = st_ref[d]
        o_inter = lax.dot_general(q_in, st_old.astype(BF16), nt, preferred_element_type=F32)
        o_intra = jnp.concatenate(
            [jnp.dot(att[h * GLA_CHUNK:(h + 1) * GLA_CHUNK, :],
                     v[:, h * GLA_DV:(h + 1) * GLA_DV], preferred_element_type=F32)
             for h in range(GLA_HEADS)], axis=1)
        o_ref[rows, :] = (o_intra + o_inter).astype(BF16)
        st_ref[d] = st_old * decay + kvt


def _gla(p, dec, tri):
    def rbf(b, s):
        return jnp.where(s == 0, XB + b, b * BLK_PER_SEQ + s - 1)

    def rbb(b, s):
        return jnp.where(s == 0, XB + b, b * BLK_PER_SEQ + BLK_PER_SEQ - s)

    cgd = P_GD // GD_W
    cv = P_GV // GLA_V
    n_ch = TC // GLA_CHUNK
    dec4 = dec.reshape(2, M_ALL // TC, n_ch, GLA_QK)
    out = jax.ShapeDtypeStruct((M_ALL, GLA_V), BF16)
    return pl.pallas_call(
        _gla_kernel,
        grid=(BATCH, BLK_PER_SEQ + 1),
        in_specs=[pl.BlockSpec((TC, GD_W), lambda b, s: (rbf(b, s), cgd)),
                  pl.BlockSpec((TC, GD_W), lambda b, s: (rbb(b, s), cgd + 1)),
                  pl.BlockSpec((TC, GLA_V), lambda b, s: (rbf(b, s), cv)),
                  pl.BlockSpec((TC, GLA_V), lambda b, s: (rbb(b, s), cv)),
                  pl.BlockSpec((None, None, n_ch, GLA_QK), lambda b, s: (0, rbf(b, s), 0, 0)),
                  pl.BlockSpec((None, None, n_ch, GLA_QK), lambda b, s: (1, rbb(b, s), 0, 0)),
                  pl.BlockSpec((2, GLA_CHUNK, GLA_CHUNK), lambda b, s: (0, 0, 0))],
        out_specs=[pl.BlockSpec((TC, GLA_V), lambda b, s: (rbf(b, s), 0)),
                   pl.BlockSpec((TC, GLA_V), lambda b, s: (rbb(b, s), 0))],
        out_shape=[out, out],
        scratch_shapes=[pltpu.VMEM((2, GLA_V, GLA_QK), F32)],
        compiler_params=_cparams(2),
        name="gla",
    )(p, p, p, p, dec4, dec4, tri)


NKEYS = CTX_LEN + SEQ
VROWS = 80
NT_DIMS = (((1,), (1,)), ((), ()))


def _attn_kernel(q_ref, g_ref, ckv_ref, xkv_ref, selk_ref, selvt_ref, eye_ref, o_ref,
                 kz_ref, vtc_ref, vtx_ref, qt_ref, m_ref, al_ref, acc_ref, s_ref):
    qi = pl.program_id(1)
    row_v = lax.broadcasted_iota(jnp.int32, (VROWS, 1), 0)

    def ext_values(svt, blk):
        vt = lax.dot_general(svt, blk, NT_DIMS, preferred_element_type=F32)
        return jnp.where(row_v == ATT_HD, 1.0, vt).astype(BF16)

    @pl.when(qi == 0)
    def _():
        for h in range(ATT_KV_HEADS):
            for e in range(2):
                sk = selk_ref[h, e]
                for (src, r0, n) in ((ckv_ref, 0, CTX_LEN), (xkv_ref, CTX_LEN, SEQ)):
                    for r in range(0, n, 1024):
                        nr = min(1024, n - r)
                        kz_ref[h, e, r0 + r:r0 + r + nr, :] = jnp.dot(
                            src[r:r + nr, :], sk, preferred_element_type=F32).astype(BF16)
            svt = selvt_ref[h]
            vtc_ref[h] = ext_values(svt, ckv_ref[...])
            for t in range(SEQ // KB):
                vtx_ref[h, t] = ext_values(svt, xkv_ref[t * KB:(t + 1) * KB, :])

    eye = eye_ref[...]
    for pr in range(ATT_HEADS // 2):
        qt_ref[pr] = lax.dot_general(eye, q_ref[:, pr * LANES:(pr + 1) * LANES], NT_DIMS,
                                     preferred_element_type=F32).astype(BF16)
    m_ref[...] = jnp.full_like(m_ref, -jnp.inf)
    acc_ref[...] = jnp.zeros_like(acc_ref)

    def scores(nxt, hd):
        slot, k0, nk, _ = nxt
        h, pr, e = hd // ATT_GROUP, hd // 2, hd % 2
        s_ref[slot, hd, 0:nk, :] = jnp.dot(kz_ref[h, e, pl.ds(k0, nk), :], qt_ref[pr],
                                           preferred_element_type=F32)

    def stage(cur, nxt):
        if cur is not None:
            slot, _, nk, vt_of = cur
            for hd in range(ATT_HEADS):
                s3 = s_ref[slot, hd, 0:nk, :].reshape(nk // SUBLANES, SUBLANES, TC)
                m_col = jnp.max(jnp.max(s3, axis=0), axis=0, keepdims=True)
                m_prev = m_ref[hd]
                m_next = jnp.maximum(m_prev, m_col)
                al_ref[hd] = jnp.exp2(m_prev - m_next)
                m_ref[hd] = m_next
        for hd in range(ATT_HEADS):
            if nxt is not None:
                scores(nxt, hd)
            if cur is not None:
                s3 = s_ref[slot, hd, 0:nk, :].reshape(nk // SUBLANES, SUBLANES, TC)
                p = jnp.exp2(s3 - m_ref[hd][None]).reshape(nk, TC).astype(BF16)
                pv = jnp.dot(vt_of(hd // ATT_GROUP), p, preferred_element_type=F32)
                acc3 = acc_ref[hd].reshape(VROWS // SUBLANES, SUBLANES, TC) * al_ref[hd][None]
                acc_ref[hd] = acc3.reshape(VROWS, TC) + pv

    n_xb = SEQ // KB
    ctx_stage = (1, 0, CTX_LEN, lambda h: vtc_ref[h])

    def x_stage(t, slot):
        k0 = CTX_LEN + t * KB
        if not isinstance(t, int):
            k0 = pl.multiple_of(k0, CTX_LEN)
        return (slot, k0, KB, lambda h: vtx_ref[h, t])

    stage(None, ctx_stage)

    @pl.when(qi == 0)
    def _():
        stage(ctx_stage, None)

    @pl.when(qi > 0)
    def _():
        stage(ctx_stage, x_stage(0, 0))

        def body(j, carry):
            t = 2 * j
            stage(x_stage(t, 0), x_stage(t + 1, 1))
            stage(x_stage(t + 1, 1), x_stage(t + 2, 0))
            return carry
        lax.fori_loop(0, (n_xb - 2) // 2, body, 0)
        stage(x_stage(n_xb - 2, 0), x_stage(n_xb - 1, 1))
        stage(x_stage(n_xb - 1, 1), None)

    outs = []
    for hd in range(ATT_HEADS):
        a = acc_ref[hd]
        outs.append(a[0:ATT_HD, :] / a[ATT_HD:ATT_HD + 1, :])
    o_nat = jnp.concatenate(outs, axis=0).T
    o_ref[...] = (o_nat * g_ref[...].astype(F32)).astype(BF16)


def _attn(p, selk, selvt, eye):
    cq = P_AQ // ATT_Q
    cg = P_AG // ATT_Q
    ckv = P_AKV // (2 * ATT_KV)

    def qrow(b, qi):
        return jnp.where(qi == 0, XB + b, b * BLK_PER_SEQ + qi - 1)

    return pl.pallas_call(
        _attn_kernel,
        grid=(BATCH, BLK_PER_SEQ + 1),
        in_specs=[pl.BlockSpec((TC, ATT_Q), lambda b, qi: (qrow(b, qi), cq)),
                  pl.BlockSpec((TC, ATT_Q), lambda b, qi: (qrow(b, qi), cg)),
                  pl.BlockSpec((CTX_LEN, 2 * ATT_KV), lambda b, qi: (XB + b, ckv)),
                  pl.BlockSpec((SEQ, 2 * ATT_KV), lambda b, qi: (b, ckv)),
                  pl.BlockSpec((ATT_KV_HEADS, 2, 2 * ATT_KV, LANES), lambda b, qi: (0, 0, 0, 0)),
                  pl.BlockSpec((ATT_KV_HEADS, VROWS, 2 * ATT_KV), lambda b, qi: (0, 0, 0)),
                  pl.BlockSpec((LANES, LANES), lambda b, qi: (0, 0))],
        out_specs=pl.BlockSpec((TC, ATT_Q), lambda b, qi: (qrow(b, qi), 0)),
        out_shape=jax.ShapeDtypeStruct((M_ALL, ATT_Q), BF16),
        scratch_shapes=[pltpu.VMEM((ATT_KV_HEADS, 2, NKEYS, LANES), BF16),
                        pltpu.VMEM((ATT_KV_HEADS, VROWS, CTX_LEN), BF16),
                        pltpu.VMEM((ATT_KV_HEADS, SEQ // KB, VROWS, KB), BF16),
                        pltpu.VMEM((ATT_HEADS // 2, LANES, TC), BF16),
                        pltpu.VMEM((ATT_HEADS, SUBLANES, TC), F32),
                        pltpu.VMEM((ATT_HEADS, SUBLANES, TC), F32),
                        pltpu.VMEM((ATT_HEADS, VROWS, TC), F32),
                        pltpu.VMEM((2, ATT_HEADS, KB, TC), F32)],
        compiler_params=_cparams(2),
        name="attention",
    )(p, p, p, p, selk, selvt, eye)


def _final_kernel(sx_ref, sc_ref, sig_ref, gg_ref, ua_ref, of_ref, ob_ref, oc_ref, mod_ref,
                  wc_ref, wg_ref, wa_ref, wo_ref, gn_ref, o_ref):
    i = pl.program_id(0)
    row = jnp.where(i >= N_XT, BATCH, i // TILES_PER_SEQ)
    ya = jnp.dot(ua_ref[...], wc_ref[...], preferred_element_type=F32)
    og = of_ref[...].astype(F32) + ob_ref[...].astype(F32)
    gn = gn_ref[...]
    parts = []
    for h in range(GLA_HEADS):
        oh = og[:, h * GLA_DV:(h + 1) * GLA_DV]
        ms = jnp.mean(oh * oh, axis=-1, keepdims=True)
        parts.append(oh * lax.rsqrt(ms + EPS) * gn)
    on = jnp.concatenate(parts, axis=1) * gg_ref[...].astype(F32)
    yb = jnp.dot(on.astype(BF16), wg_ref[...], preferred_element_type=F32)
    yc = jnp.dot(oc_ref[...], wa_ref[...], preferred_element_type=F32)
    merged = (sig_ref[:, 0:D_MODEL].astype(F32) * ya
              + sig_ref[:, D_MODEL:2 * D_MODEL].astype(F32) * yb
              + sig_ref[:, 2 * D_MODEL:3 * D_MODEL].astype(F32) * yc)
    out = jnp.dot(merged.astype(BF16), wo_ref[...], preferred_element_type=F32)
    gate = mod_ref[pl.ds(row, 1), 2 * D_MODEL:3 * D_MODEL]
    o_ref[...] = jnp.where(i >= N_XT, sc_ref[...], sx_ref[...]) + gate * out


def _final(l, n_tiles, xs, cs, ctx_off, p, ua, o_gla, oc, mod, wc, wg, wa, wo, gn):
    rowblk = lambda i: (i, 0)
    wspec = lambda k: pl.BlockSpec((None, k, D_MODEL), lambda i: (l, 0, 0))
    return pl.pallas_call(
        _final_kernel,
        grid=(n_tiles,),
        in_specs=_stream_specs(ctx_off) + [
                  pl.BlockSpec((TM, 3 * D_MODEL), lambda i: (i, P_SIG // (3 * D_MODEL))),
                  pl.BlockSpec((TM, GLA_V), lambda i: (i, P_GG // GLA_V)),
                  pl.BlockSpec((TM, D_CONV), rowblk),
                  pl.BlockSpec((TM, GLA_V), rowblk),
                  pl.BlockSpec((TM, GLA_V), rowblk),
                  pl.BlockSpec((TM, ATT_Q), rowblk),
                  pl.BlockSpec((None, 16, 3 * D_MODEL), lambda i: (l, 0, 0)),
                  wspec(D_CONV), wspec(GLA_V), wspec(ATT_Q), wspec(D_MODEL),
                  pl.BlockSpec((None, 1, GLA_DV), lambda i: (l, 0, 0))],
        out_specs=pl.BlockSpec((TM, D_MODEL), rowblk),
        out_shape=jax.ShapeDtypeStruct((n_tiles * TM, D_MODEL), F32),
        compiler_params=_cparams(1),
        name="merge_out",
    )(xs, cs, p, p, ua, o_gla[0], o_gla[1], oc, mod, wc, wg, wa, wo, gn)


def _rope_tables():
    t = np.arange(SEQ)
    row = (t // GRID_W).astype(np.float32)
    col = (t % GRID_W).astype(np.float32)
    n_freq = ROPE_AXIS_DIM // 2
    freqs = (np.float32(ROPE_THETA) ** (-np.arange(n_freq, dtype=np.float32) / n_freq)).astype(np.float32)
    ar = row[:, None] * freqs
    ac = col[:, None] * freqs
    cos64 = np.concatenate([np.cos(ar), np.cos(ar), np.cos(ac), np.cos(ac)], axis=1)
    sin64 = np.concatenate([-np.sin(ar), np.sin(ar), -np.sin(ac), np.sin(ac)], axis=1)
    return (np.tile(cos64, (1, ATT_HEADS)).astype(np.float32),
            np.tile(sin64, (1, ATT_HEADS)).astype(np.float32))


def _static_tables():
    bd = np.kron(np.eye(ATT_HEADS, dtype=np.float32), np.ones((ATT_HD, ATT_HD), np.float32))
    idx = np.arange(GLA_CHUNK)
    tri = np.stack([(idx[None, :] <= idx[:, None]), (idx[None, :] >= idx[:, None])]).astype(np.float32)
    selk = np.zeros((ATT_KV_HEADS, 2, 2 * ATT_KV, LANES), np.float32)
    selvt = np.zeros((ATT_KV_HEADS, VROWS, 2 * ATT_KV), np.float32)
    for h in range(ATT_KV_HEADS):
        for dd in range(ATT_HD):
            selvt[h, dd, ATT_KV + h * ATT_HD + dd] = 1.0
            for e in range(2):
                selk[h, e, h * ATT_HD + dd, e * ATT_HD + dd] = 1.0
    shifts = np.stack([np.eye(CONV_SH, k=r, dtype=np.float32) for r in range(SUBLANES)])
    return bd, tri, selk, selvt, shifts


def _split_cols(a):
    pad = jnp.zeros(a.shape[:-1] + (LANES - 2 * GLA_RANK,), a.dtype)
    return (a[..., :O_LR], a[..., O_AQ:], jnp.concatenate([a[..., O_LR:O_AQ], pad], axis=-1))


def kernel(x, c, ctx, c_ctx, norm_g, w_mod, b_mod, w_in, b_in, conv_dw_w, conv_dw_b, conv_ln_g,
           conv_ln_b, w_conv_out, gla_w_gate, gla_b_gate, gla_norm_g, w_gla_out, q_norm_g,
           k_norm_g, w_attn_out, w_out):
    cos_np, sin_np = _rope_tables()
    bd_np, tri_np, selk_np, selvt_np, shifts_np = _static_tables()
    shifts = jnp.asarray(shifts_np, BF16)
    cos_t, sin_t = jnp.asarray(cos_np), jnp.asarray(sin_np)
    bd = jnp.asarray(bd_np, BF16)
    tri = jnp.asarray(tri_np)
    selk = jnp.asarray(selk_np, BF16)
    selvt = jnp.asarray(selvt_np, BF16)
    eye = jnp.eye(LANES, dtype=BF16)

    w_abc = tuple(w.astype(BF16) for w in _split_cols(w_in))
    b_abc = tuple(b.reshape(DEPTH, 1, b.shape[-1]) for b in _split_cols(b_in))
    cc = jnp.concatenate([c, c_ctx[None, :], jnp.zeros((16 - BATCH - 1, D_MODEL), F32)], axis=0)
    dw_w = jnp.broadcast_to(conv_dw_w[:, :, None, :], (DEPTH, CONV_WIDTH, SUBLANES, D_CONV))
    r3 = lambda a: a.reshape(DEPTH, 1, a.shape[-1])
    wup = jnp.zeros((DEPTH, LANES, 2 * GLA_QK), F32)
    wup = wup.at[:, 0:GLA_RANK, 0:GLA_QK].set(gla_w_gate[:, 0])
    wup = wup.at[:, GLA_RANK:2 * GLA_RANK, GLA_QK:].set(gla_w_gate[:, 1])
    wuh = wup.astype(BF16)
    wul = (wup - wuh.astype(F32)).astype(BF16)
    bup = gla_b_gate.reshape(DEPTH, 1, 2 * GLA_QK)
    qg = jnp.tile(q_norm_g, (1, ATT_HEADS)).reshape(DEPTH, 1, ATT_Q)
    kg = jnp.tile(k_norm_g, (1, ATT_KV_HEADS)).reshape(DEPTH, 1, ATT_KV)
    wc, wg, wa, wo = (w.astype(BF16) for w in (w_conv_out, w_gla_out, w_attn_out, w_out))

    mod = _modulation(cc, w_mod, b_mod)
    xs, cs, ctx_off = x.reshape(MX, D_MODEL), ctx.reshape(MC, D_MODEL), 0
    for l in range(DEPTH):
        p, dec = _inproj(l, xs, cs, ctx_off, mod, r3(norm_g), w_abc, b_abc, cos_t, sin_t, qg, kg, bd, wuh, wul, bup)
        ua = _conv(l, p, dw_w, r3(conv_dw_b), r3(conv_ln_g), r3(conv_ln_b), shifts)
        o_gla = _gla(p, dec, tri)
        oc = _attn(p, selk, selvt, eye)
        n_tiles = M_ALL // TM if l < DEPTH - 1 else N_XT
        xs = _final(l, n_tiles, xs, cs, ctx_off, p, ua, o_gla, oc, mod, wc, wg, wa, wo, r3(gla_norm_g))
        cs, ctx_off = xs, N_XT
    return xs.reshape(BATCH, SEQ, D_MODEL)
```

```python
import functools
import math

import numpy as np
import jax
import jax.numpy as jnp
from jax import lax
from jax.experimental import pallas as pl
from jax.experimental.pallas import tpu as pltpu

F32 = jnp.float32
BF16 = jnp.bfloat16
HIGHEST = lax.Precision.HIGHEST

D_MODEL = 1024
BATCH = 8
SEQ = 4096
DEPTH = 4
CTX_LEN = 256
GRID_W = 64
EPS = 1e-6
D_CONV = 512
CONV_WIDTH = 31
CONV_PAD = CONV_WIDTH // 2
GLA_HEADS = 4
GLA_DK = 64
GLA_DV = 128
GLA_QK = GLA_HEADS * GLA_DK
GLA_V = GLA_HEADS * GLA_DV
GLA_RANK = 16
GLA_GATE_NORM = 16.0
GLA_CHUNK = 64
ATT_HEADS = 8
ATT_KV_HEADS = 2
ATT_GROUP = ATT_HEADS // ATT_KV_HEADS
ATT_HD = 64
ATT_Q = ATT_HEADS * ATT_HD
ATT_KV = ATT_KV_HEADS * ATT_HD
ROPE_AXIS_DIM = ATT_HD // 2
ROPE_THETA = 10000.0

LANES = 128
SUBLANES = 8
MX = BATCH * SEQ
MC = BATCH * CTX_LEN
M_ALL = MX + MC

O_VAL, O_GLU, O_CGATE = 0, 512, 1024
O_GQ, O_GK, O_GV, O_GG = 1536, 1792, 2048, 2560
O_LR = 3072
O_AQ, O_AK, O_AV, O_AG = 3104, 3616, 3744, 3872
O_MA = 4384
N_IN = 7456

A_VAL, A_GLU, A_CG, A_GQK, A_GV, A_GG = 0, 512, 1024, 1536, 2048, 2560
NA = O_LR
B_AQ, B_AKV, B_AG, B_M = 0, 512, 768, 1280
NB = N_IN - O_AQ

P_SIG = 0
P_U = 3072
P_CG = 3584
P_GV = 4096
P_GD = 4608
GD_W = 3 * GLA_QK
P_GG = 6144
P_AQ = 6656
P_AG = 7168
P_AKV = 7680
NP = 7936

TM = 512
N_XT = MX // TM
N_CT = MC // TM
TILES_PER_SEQ = SEQ // TM
TC = 256
XB = MX // TC
BLK_PER_SEQ = SEQ // TC
HALO = 16
CONV_RC = 32
CONV_SH = TC // 2 + 2 * HALO
KB = 512
Q_PRESCALE = (ATT_HD ** -0.5) * math.log2(math.e)

VMEM_LIMIT = 56 * 1024 * 1024


def _cparams(n_axes, vmem=VMEM_LIMIT):
    return pltpu.CompilerParams(dimension_semantics=("arbitrary",) * n_axes,
                                vmem_limit_bytes=vmem)


def _silu(x):
    return x * jax.nn.sigmoid(x)


def _mod_kernel(c_ref, w_ref, b_ref, o_ref):
    s = _silu(c_ref[...])
    o_ref[...] = jnp.dot(s, w_ref[...], preferred_element_type=F32, precision=HIGHEST) + b_ref[...]


def _modulation(cc, w_mod, b_mod):
    nt = 3 * D_MODEL // 1024
    return pl.pallas_call(
        _mod_kernel,
        grid=(DEPTH, nt),
        in_specs=[pl.BlockSpec((16, D_MODEL), lambda l, n: (0, 0)),
                  pl.BlockSpec((None, D_MODEL, 1024), lambda l, n: (l, 0, n)),
                  pl.BlockSpec((None, 1, 1024), lambda l, n: (l, 0, n))],
        out_specs=pl.BlockSpec((None, 16, 1024), lambda l, n: (l, 0, n)),
        out_shape=jax.ShapeDtypeStruct((DEPTH, 16, 3 * D_MODEL), F32),
        compiler_params=_cparams(2),
        name="modulation",
    )(cc, w_mod, b_mod.reshape(DEPTH, 1, 3 * D_MODEL))


def _head_norm(xv, gain, bd):
    ss = jnp.dot((xv * xv).astype(BF16), bd, preferred_element_type=F32)
    return xv * lax.rsqrt(ss * (1.0 / ATT_HD) + EPS) * gain


def _rope(xv, cosv, sinv):
    parts = []
    for s in range(xv.shape[1] // LANES):
        sl = slice(s * LANES, (s + 1) * LANES)
        xs = xv[:, sl]
        up = pltpu.roll(xs, LANES - 16, axis=1)
        dn = pltpu.roll(xs, 16, axis=1)
        lane = lax.broadcasted_iota(jnp.int32, xs.shape, 1)
        partner = jnp.where((lane & 16) == 0, up, dn)
        parts.append(xs * cosv[:, sl] + partner * sinv[:, sl])
    return jnp.concatenate(parts, axis=1) if len(parts) > 1 else parts[0]


def _inproj_kernel(x_ref, c_ref, mod_ref, g_ref, wa_ref, wb_ref, wc_ref, ba_ref, bb_ref, bc_ref,
                   cos_ref, sin_ref, qg_ref, kg_ref,
                   bd_ref, wuh_ref, wul_ref, bup_ref, o_ref, dec_ref):
    i = pl.program_id(0)
    is_ctx = i >= N_XT
    row = jnp.where(is_ctx, BATCH, i // TILES_PER_SEQ)
    x = jnp.where(is_ctx, c_ref[...], x_ref[...])
    ms = jnp.mean(x * x, axis=-1, keepdims=True)
    y = x * lax.rsqrt(ms + EPS) * g_ref[...]
    m = mod_ref[pl.ds(row, 1), :]
    shift = m[:, 0:D_MODEL]
    scale = m[:, D_MODEL:2 * D_MODEL]
    h = (y * (1.0 + scale) + shift).astype(BF16)

    def proj_from(w_ref, b_ref):
        def proj(a, n):
            return jnp.dot(h, w_ref[:, a:a + n], preferred_element_type=F32) + b_ref[:, a:a + n]
        return proj

    proj_a = proj_from(wa_ref, ba_ref)
    proj_b = proj_from(wb_ref, bb_ref)
    proj_c = proj_from(wc_ref, bc_ref)

    def put(a, val):
        o_ref[:, a:a + val.shape[1]] = val.astype(BF16)

    lr = proj_c(0, LANES)
    aq_raw = proj_b(B_AQ, ATT_Q)
    akv = proj_b(B_AKV, 2 * ATT_KV)
    put(P_SIG, jax.nn.sigmoid(proj_b(B_M, 1024)))

    lr_hi = lr.astype(BF16)
    lr_lo = (lr - lr_hi.astype(F32)).astype(BF16)
    wuh = wuh_ref[...]
    z = (jnp.dot(lr_hi, wuh, preferred_element_type=F32)
         + jnp.dot(lr_lo, wuh, preferred_element_type=F32)
         + jnp.dot(lr_hi, wul_ref[...], preferred_element_type=F32)) + bup_ref[...]
    put(P_SIG + 1024, jax.nn.sigmoid(proj_b(B_M + 1024, 1024)))

    cosv = jnp.where(is_ctx, 1.0, cos_ref[...])
    sinv = jnp.where(is_ctx, 0.0, sin_ref[...])
    bd = bd_ref[...]
    aq = _head_norm(aq_raw, qg_ref[...], bd)
    put(P_AQ, _rope(aq, cosv, sinv) * Q_PRESCALE)
    ak = _head_norm(akv[:, :ATT_KV], kg_ref[...], bd[:ATT_KV, :ATT_KV])
    put(P_AKV, _rope(ak, cosv[:, :ATT_KV], sinv[:, :ATT_KV]))
    put(P_AKV + ATT_KV, akv[:, ATT_KV:])
    put(P_SIG + 2048, jax.nn.sigmoid(proj_b(B_M + 2048, 1024)))

    la = (jnp.minimum(z, 0.0) - jnp.log(1.0 + jnp.exp(-jnp.abs(z)))) * (1.0 / GLA_GATE_NORM)
    rowc = lax.broadcasted_iota(jnp.int32, (TM, GLA_QK), 0) & (GLA_CHUNK - 1)
    cf = la[:, :GLA_QK]
    cb = la[:, GLA_QK:]
    sh = 1
    while sh < GLA_CHUNK:
        cf = cf + jnp.where(rowc >= sh, pltpu.roll(cf, sh, axis=0), 0.0)
        cb = cb + jnp.where(rowc < GLA_CHUNK - sh, pltpu.roll(cb, TM - sh, axis=0), 0.0)
        sh *= 2
    n_ch = TM // GLA_CHUNK
    last_f = [cf[c * GLA_CHUNK + GLA_CHUNK - 1:(c + 1) * GLA_CHUNK, :] for c in range(n_ch)]
    last_b = [cb[c * GLA_CHUNK:c * GLA_CHUNK + 1, :] for c in range(n_ch)]
    dec_ref[0] = jnp.concatenate(last_f, axis=0)
    dec_ref[1] = jnp.concatenate(last_b, axis=0)
    qk = proj_a(A_GQK, 512)
    gq = qk[:, :GLA_QK] * (GLA_DK ** -0.5)
    gk = qk[:, GLA_QK:]
    for dd, (cum, last) in enumerate(((cf, last_f), (cb, last_b))):
        tot = jnp.concatenate([jnp.broadcast_to(t, (GLA_CHUNK, GLA_QK)) for t in last], axis=0)
        base = P_GD + dd * GD_W
        put(base, gq * jnp.exp(cum))
        put(base + GLA_QK, gk * jnp.exp(-cum))
        put(base + 2 * GLA_QK, gk * jnp.exp(tot - cum))

    put(P_U, proj_a(A_VAL, 512) * jax.nn.sigmoid(proj_a(A_GLU, 512)))
    put(P_CG, _silu(proj_a(A_CG, 512)))
    put(P_GG, _silu(proj_a(A_GG, 512)))
    put(P_AG, _silu(proj_b(B_AG, ATT_Q)))
    put(P_GV, proj_a(A_GV, 512))


def _stream_specs(ctx_off):
    return [pl.BlockSpec((TM, D_MODEL), lambda i: (jnp.minimum(i, N_XT - 1), 0)),
            pl.BlockSpec((TM, D_MODEL), lambda i: (jnp.maximum(i - N_XT, 0) + ctx_off, 0))]


def _inproj(l, xs, cs, ctx_off, mod, norm_g, w_abc, b_abc, cos_t, sin_t, qg, kg, bd, wuh, wul, bup):
    const = lambda i: (0, 0)
    pos = lambda i: (jnp.where(i >= N_XT, 0, i % TILES_PER_SEQ), 0)
    return pl.pallas_call(
        _inproj_kernel,
        grid=(M_ALL // TM,),
        in_specs=_stream_specs(ctx_off) + [
                  pl.BlockSpec((None, 16, 3 * D_MODEL), lambda i: (l, 0, 0)),
                  pl.BlockSpec((None, 1, D_MODEL), lambda i: (l, 0, 0)),
                  ] + [pl.BlockSpec((None, D_MODEL, w.shape[-1]), lambda i: (l, 0, 0),
                                    pipeline_mode=pl.Buffered(1)) for w in w_abc
                  ] + [pl.BlockSpec((None, 1, b.shape[-1]), lambda i: (l, 0, 0)) for b in b_abc
                  ] + [
                  pl.BlockSpec((TM, ATT_Q), pos),
                  pl.BlockSpec((TM, ATT_Q), pos),
                  pl.BlockSpec((None, 1, ATT_Q), lambda i: (l, 0, 0)),
                  pl.BlockSpec((None, 1, ATT_KV), lambda i: (l, 0, 0)),
                  pl.BlockSpec((ATT_Q, ATT_Q), const),
                  pl.BlockSpec((None, LANES, 2 * GLA_QK), lambda i: (l, 0, 0)),
                  pl.BlockSpec((None, LANES, 2 * GLA_QK), lambda i: (l, 0, 0)),
                  pl.BlockSpec((None, 1, 2 * GLA_QK), lambda i: (l, 0, 0))],
        out_specs=[pl.BlockSpec((TM, NP), lambda i: (i, 0)),
                   pl.BlockSpec((2, TM // GLA_CHUNK, GLA_QK), lambda i: (0, i, 0))],
        out_shape=[jax.ShapeDtypeStruct((M_ALL, NP), BF16),
                   jax.ShapeDtypeStruct((2, M_ALL // GLA_CHUNK, GLA_QK), F32)],
        compiler_params=_cparams(1),
        name="inproj",
    )(xs, cs, mod, norm_g, *w_abc, *b_abc, cos_t, sin_t, qg, kg, bd, wuh, wul, bup)


def _conv_kernel(u_ref, ul_ref, ur_ref, sg_ref, dw_ref, dwb_ref, lng_ref, lnb_ref, sh_ref,
                 o_ref, win_ref, ext_ref):
    i = pl.program_id(0)
    j = i % BLK_PER_SEQ
    is_x = i < XB
    left_ok = jnp.logical_and(is_x, j != 0)
    right_ok = jnp.logical_and(is_x, j != BLK_PER_SEQ - 1)
    zero_h = jnp.zeros((HALO, D_CONV), BF16)
    win_ref[0:HALO, :] = jnp.where(left_ok, ul_ref[...], zero_h)
    win_ref[HALO:HALO + TC, :] = u_ref[...]
    win_ref[HALO + TC:, :] = jnp.where(right_ok, ur_ref[...], zero_h)
    ext_ref[0] = win_ref[...].astype(F32)
    half = TC // 2
    for r in range(1, SUBLANES):
        for a in range(2):
            ext_ref[r, a * half:a * half + CONV_SH, :] = jnp.dot(
                sh_ref[r], win_ref[a * half:a * half + CONV_SH, :], preferred_element_type=F32)
    bias = dwb_ref[...]
    lng = lng_ref[...]
    lnb = lnb_ref[...]
    for c in range(TC // CONV_RC):
        r0 = c * CONV_RC
        acc = jnp.broadcast_to(bias, (CONV_RC, D_CONV))
        for t in range(CONV_WIDTH):
            off = t + HALO - CONV_PAD
            a0 = r0 + (off // SUBLANES) * SUBLANES
            w_t = jnp.concatenate([dw_ref[t]] * (CONV_RC // SUBLANES), axis=0)
            acc = acc + ext_ref[off % SUBLANES, a0:a0 + CONV_RC, :] * w_t
        mu = jnp.mean(acc, axis=-1, keepdims=True)
        d = acc - mu
        var = jnp.mean(d * d, axis=-1, keepdims=True)
        yn = d * lax.rsqrt(var + EPS) * lng + lnb
        o_ref[r0:r0 + CONV_RC, :] = (_silu(yn) * sg_ref[r0:r0 + CONV_RC, :].astype(F32)).astype(BF16)


def _conv(l, p, dw_w, dw_b, ln_g, ln_b, shifts):
    nhb = M_ALL // HALO
    per = TC // HALO
    cu = P_U // D_CONV
    cg = P_CG // D_CONV
    vec = lambda i: (l, 0, 0)
    return pl.pallas_call(
        _conv_kernel,
        grid=(M_ALL // TC,),
        in_specs=[pl.BlockSpec((TC, D_CONV), lambda i: (i, cu)),
                  pl.BlockSpec((HALO, D_CONV), lambda i: (jnp.maximum(i * per - 1, 0), cu)),
                  pl.BlockSpec((HALO, D_CONV), lambda i: (jnp.minimum((i + 1) * per, nhb - 1), cu)),
                  pl.BlockSpec((TC, D_CONV), lambda i: (i, cg)),
                  pl.BlockSpec((None, CONV_WIDTH, SUBLANES, D_CONV), lambda i: (l, 0, 0, 0)),
                  pl.BlockSpec((None, 1, D_CONV), vec),
                  pl.BlockSpec((None, 1, D_CONV), vec),
                  pl.BlockSpec((None, 1, D_CONV), vec),
                  pl.BlockSpec((SUBLANES, CONV_SH, CONV_SH), lambda i: (0, 0, 0))],
        out_specs=pl.BlockSpec((TC, D_CONV), lambda i: (i, 0)),
        out_shape=jax.ShapeDtypeStruct((M_ALL, D_CONV), BF16),
        scratch_shapes=[pltpu.VMEM((TC + 2 * HALO, D_CONV), BF16),
                        pltpu.VMEM((SUBLANES, TC + 2 * HALO, D_CONV), F32)],
        compiler_params=_cparams(1),
        name="conv",
    )(p, p, p, p, dw_w, dw_b, ln_g, ln_b, shifts)


def _gla_kernel(qf_ref, qb_ref, vf_ref, vb_ref, df_ref, db_ref, tri_ref, of_ref, ob_ref, st_ref):
    step = pl.program_id(1)

    @pl.when(step == 0)
    def _():
        st_ref[...] = jnp.zeros_like(st_ref)

    lane_h = lax.broadcasted_iota(jnp.int32, (GLA_CHUNK, GLA_QK), 1) // GLA_DK
    srow_h = lax.broadcasted_iota(jnp.int32, (GLA_V, GLA_QK), 0) // GLA_DV
    scol_h = lax.broadcasted_iota(jnp.int32, (GLA_V, GLA_QK), 1) // GLA_DK
    smask = srow_h == scol_h
    nt = (((1,), (1,)), ((), ()))
    n_chunks = TC // GLA_CHUNK
    dirs = ((qf_ref, vf_ref, df_ref, of_ref), (qb_ref, vb_ref, db_ref, ob_ref))
    work = []
    for n in range(n_chunks):
        for d, (qkk_ref, v_ref, dec_ref, o_ref) in enumerate(dirs):
            c = n if d == 0 else n_chunks - 1 - n
            rows = slice(c * GLA_CHUNK, (c + 1) * GLA_CHUNK)
            q_in = qkk_ref[rows, 0:GLA_QK]
            k_in = qkk_ref[rows, GLA_QK:2 * GLA_QK]
            k_st = qkk_ref[rows, 2 * GLA_QK:3 * GLA_QK]
            v = v_ref[rows, :]
            decay = jnp.exp(dec_ref[c:c + 1, :])
            tri4 = jnp.concatenate([tri_ref[d]] * GLA_HEADS, axis=0) > 0.5
            q_stack = jnp.concatenate(
                [jnp.where(lane_h == h, q_in, jnp.zeros_like(q_in)) for h in range(GLA_HEADS)],
                axis=0)
            att = lax.dot_general(q_stack, k_in, nt, preferred_element_type=F32)
            att = jnp.where(tri4, att, 0.0).astype(BF16)
            kvt = lax.dot_general(v, k_st, (((0,), (0,)), ((), ())),
                                  preferred_element_type=F32)
            work.append((d, o_ref, rows, q_in, v, decay, att, jnp.where(smask, kvt, 0.0)))
    for d, o_ref, rows, q_in, v, decay, att, kvt in work:
        st_old = st_ref[d]
        o_inter = lax.dot_general(q_in, st_old.astype(BF16), nt, preferred_element_type=F32)
        o_intra = jnp.concatenate(
            [jnp.dot(att[h * GLA_CHUNK:(h + 1) * GLA_CHUNK, :],
                     v[:, h * GLA_DV:(h + 1) * GLA_DV], preferred_element_type=F32)
             for h in range(GLA_HEADS)], axis=1)
        o_ref[rows, :] = (o_intra + o_inter).astype(BF16)
        st_ref[d] = st_old * decay + kvt


def _gla(p, dec, tri):
    def rbf(b, s):
        return jnp.where(s == 0, XB + b, b * BLK_PER_SEQ + s - 1)

    def rbb(b, s):
        return jnp.where(s == 0, XB + b, b * BLK_PER_SEQ + BLK_PER_SEQ - s)

    cgd = P_GD // GD_W
    cv = P_GV // GLA_V
    n_ch = TC // GLA_CHUNK
    dec4 = dec.reshape(2, M_ALL // TC, n_ch, GLA_QK)
    out = jax.ShapeDtypeStruct((M_ALL, GLA_V), BF16)
    return pl.pallas_call(
        _gla_kernel,
        grid=(BATCH, BLK_PER_SEQ + 1),
        in_specs=[pl.BlockSpec((TC, GD_W), lambda b, s: (rbf(b, s), cgd)),
                  pl.BlockSpec((TC, GD_W), lambda b, s: (rbb(b, s), cgd + 1)),
                  pl.BlockSpec((TC, GLA_V), lambda b, s: (rbf(b, s), cv)),
                  pl.BlockSpec((TC, GLA_V), lambda b, s: (rbb(b, s), cv)),
                  pl.BlockSpec((None, None, n_ch, GLA_QK), lambda b, s: (0, rbf(b, s), 0, 0)),
                  pl.BlockSpec((None, None, n_ch, GLA_QK), lambda b, s: (1, rbb(b, s), 0, 0)),
                  pl.BlockSpec((2, GLA_CHUNK, GLA_CHUNK), lambda b, s: (0, 0, 0))],
        out_specs=[pl.BlockSpec((TC, GLA_V), lambda b, s: (rbf(b, s), 0)),
                   pl.BlockSpec((TC, GLA_V), lambda b, s: (rbb(b, s), 0))],
        out_shape=[out, out],
        scratch_shapes=[pltpu.VMEM((2, GLA_V, GLA_QK), F32)],
        compiler_params=_cparams(2),
        name="gla",
    )(p, p, p, p, dec4, dec4, tri)


NKEYS = CTX_LEN + SEQ
VROWS = 80
NT_DIMS = (((1,), (1,)), ((), ()))


def _make_attn_kernel(n_tiles, with_x):
    n_xb = SEQ // KB

    def kernel(*refs):
        if with_x:
            (q_ref, g_ref, ckv_ref, xkv_ref, selk_ref, selvt_ref, eye_ref, o_ref,
             kz_ref, vtc_ref, vtx_ref, qt_ref, m_ref, al_ref, acc_ref, s_ref) = refs
            key_srcs = ((ckv_ref, 0, CTX_LEN), (xkv_ref, CTX_LEN, SEQ))
        else:
            (q_ref, g_ref, ckv_ref, selk_ref, selvt_ref, eye_ref, _, o_ref,
             kz_ref, vtc_ref, qt_ref, m_ref, al_ref, acc_ref, s_ref) = refs
            key_srcs = ((ckv_ref, 0, CTX_LEN),)
        row_v = lax.broadcasted_iota(jnp.int32, (VROWS, 1), 0)

        def ext_values(svt, blk):
            vt = lax.dot_general(svt, blk, NT_DIMS, preferred_element_type=F32)
            return jnp.where(row_v == ATT_HD, 1.0, vt).astype(BF16)

        def prepare():
            for h in range(ATT_KV_HEADS):
                for e in range(2):
                    sk = selk_ref[h, e]
                    for (src, r0, n) in key_srcs:
                        for r in range(0, n, 1024):
                            nr = min(1024, n - r)
                            kz_ref[h, e, r0 + r:r0 + r + nr, :] = jnp.dot(
                                src[r:r + nr, :], sk, preferred_element_type=F32).astype(BF16)
                svt = selvt_ref[h]
                vtc_ref[h] = ext_values(svt, ckv_ref[...])
                if with_x:
                    for t in range(n_xb):
                        vtx_ref[h, t] = ext_values(svt, xkv_ref[t * KB:(t + 1) * KB, :])

        if with_x:
            pl.when(pl.program_id(1) == 0)(prepare)
        else:
            prepare()

        eye = eye_ref[...]
        for tile in range(n_tiles):
            for pr in range(ATT_HEADS // 2):
                qt_ref[tile, pr] = lax.dot_general(
                    eye, q_ref[tile * TC:(tile + 1) * TC, pr * LANES:(pr + 1) * LANES], NT_DIMS,
                    preferred_element_type=F32).astype(BF16)
        m_ref[...] = jnp.full_like(m_ref, -jnp.inf)
        acc_ref[...] = jnp.zeros_like(acc_ref)

        def scores(nxt, hd):
            tile, slot, k0, nk, _ = nxt
            h, pr, e = hd // ATT_GROUP, hd // 2, hd % 2
            s_ref[slot, hd, 0:nk, :] = jnp.dot(kz_ref[h, e, pl.ds(k0, nk), :], qt_ref[tile, pr],
                                               preferred_element_type=F32)

        def stage(cur, nxt):
            if cur is not None:
                tile, slot, _, nk, vt_of = cur
                for hd in range(ATT_HEADS):
                    s3 = s_ref[slot, hd, 0:nk, :].reshape(nk // SUBLANES, SUBLANES, TC)
                    m_col = jnp.max(jnp.max(s3, axis=0), axis=0, keepdims=True)
                    m_prev = m_ref[tile, hd]
                    m_next = jnp.maximum(m_prev, m_col)
                    al_ref[tile, hd] = jnp.exp2(m_prev - m_next)
                    m_ref[tile, hd] = m_next
            for hd in range(ATT_HEADS):
                if nxt is not None:
                    scores(nxt, hd)
                if cur is not None:
                    s3 = s_ref[slot, hd, 0:nk, :].reshape(nk // SUBLANES, SUBLANES, TC)
                    p = jnp.exp2(s3 - m_ref[tile, hd][None]).reshape(nk, TC).astype(BF16)
                    pv = jnp.dot(vt_of(hd // ATT_GROUP), p, preferred_element_type=F32)
                    acc3 = (acc_ref[tile, hd].reshape(VROWS // SUBLANES, SUBLANES, TC)
                            * al_ref[tile, hd][None])
                    acc_ref[tile, hd] = acc3.reshape(VROWS, TC) + pv

        def finish(tile):
            outs = []
            for hd in range(ATT_HEADS):
                a = acc_ref[tile, hd]
                outs.append(a[0:ATT_HD, :] / a[ATT_HD:ATT_HD + 1, :])
            o_nat = jnp.concatenate(outs, axis=0).T
            rows = slice(tile * TC, (tile + 1) * TC)
            o_ref[rows, :] = (o_nat * g_ref[rows, :].astype(F32)).astype(BF16)

        def ctx_stage(tile, slot):
            return (tile, slot, 0, CTX_LEN, lambda h: vtc_ref[h])

        def x_stage(tile, t, slot):
            k0 = CTX_LEN + t * KB
            if not isinstance(t, int):
                k0 = pl.multiple_of(k0, CTX_LEN)
            return (tile, slot, k0, KB, lambda h: vtx_ref[h, t])

        if not with_x:
            stage(None, ctx_stage(0, 0))
            stage(ctx_stage(0, 0), None)
            finish(0)
            return
        stage(None, ctx_stage(0, 1))
        for tile in range(n_tiles):
            s0 = tile % 2
            stage(ctx_stage(tile, 1 - s0), x_stage(tile, 0, s0))

            def body(j, carry, tile=tile, s0=s0):
                t = 2 * j
                stage(x_stage(tile, t, s0), x_stage(tile, t + 1, 1 - s0))
                stage(x_stage(tile, t + 1, 1 - s0), x_stage(tile, t + 2, s0))
                return carry
            lax.fori_loop(0, (n_xb - 2) // 2, body, 0)
            stage(x_stage(tile, n_xb - 2, s0), x_stage(tile, n_xb - 1, 1 - s0))
            nxt = ctx_stage(tile + 1, s0) if tile + 1 < n_tiles else None
            stage(x_stage(tile, n_xb - 1, 1 - s0), nxt)
            finish(tile)

    return kernel


ATT_XT = 2


def _attn(p, selk, selvt, eye):
    cq = P_AQ // ATT_Q
    cg = P_AG // ATT_Q
    ckv = P_AKV // (2 * ATT_KV)
    tq = ATT_XT * TC
    per_seq = SEQ // tq
    consts = [pl.BlockSpec((ATT_KV_HEADS, 2, 2 * ATT_KV, LANES), lambda *_: (0, 0, 0, 0)),
              pl.BlockSpec((ATT_KV_HEADS, VROWS, 2 * ATT_KV), lambda *_: (0, 0, 0)),
              pl.BlockSpec((LANES, LANES), lambda *_: (0, 0))]

    def scratch(n_tiles, n_keys):
        return [pltpu.VMEM((ATT_KV_HEADS, 2, n_keys, LANES), BF16),
                pltpu.VMEM((ATT_KV_HEADS, VROWS, CTX_LEN), BF16)
                ] + ([pltpu.VMEM((ATT_KV_HEADS, SEQ // KB, VROWS, KB), BF16)] if n_keys > CTX_LEN else []
                ) + [pltpu.VMEM((n_tiles, ATT_HEADS // 2, LANES, TC), BF16),
                     pltpu.VMEM((n_tiles, ATT_HEADS, SUBLANES, TC), F32),
                     pltpu.VMEM((n_tiles, ATT_HEADS, SUBLANES, TC), F32),
                     pltpu.VMEM((n_tiles, ATT_HEADS, VROWS, TC), F32),
                     pltpu.VMEM((2 if n_keys > CTX_LEN else 1, ATT_HEADS,
                                 KB if n_keys > CTX_LEN else CTX_LEN, TC), F32)]

    oc = pl.pallas_call(
        _make_attn_kernel(ATT_XT, True),
        grid=(BATCH, per_seq),
        in_specs=[pl.BlockSpec((tq, ATT_Q), lambda b, qi: (b * per_seq + qi, cq)),
                  pl.BlockSpec((tq, ATT_Q), lambda b, qi: (b * per_seq + qi, cg)),
                  pl.BlockSpec((CTX_LEN, 2 * ATT_KV), lambda b, qi: (XB + b, ckv)),
                  pl.BlockSpec((SEQ, 2 * ATT_KV), lambda b, qi: (b, ckv))] + consts,
        out_specs=pl.BlockSpec((tq, ATT_Q), lambda b, qi: (b * per_seq + qi, 0)),
        out_shape=jax.ShapeDtypeStruct((M_ALL, ATT_Q), BF16),
        scratch_shapes=scratch(ATT_XT, NKEYS),
        compiler_params=_cparams(2),
        name="attention",
    )(p, p, p, p, selk, selvt, eye)
    return pl.pallas_call(
        _make_attn_kernel(1, False),
        grid=(BATCH,),
        in_specs=[pl.BlockSpec((TC, ATT_Q), lambda b: (XB + b, cq)),
                  pl.BlockSpec((TC, ATT_Q), lambda b: (XB + b, cg)),
                  pl.BlockSpec((CTX_LEN, 2 * ATT_KV), lambda b: (XB + b, ckv))] + consts
                 + [pl.BlockSpec(memory_space=pl.ANY)],
        out_specs=pl.BlockSpec((TC, ATT_Q), lambda b: (XB + b, 0)),
        out_shape=jax.ShapeDtypeStruct((M_ALL, ATT_Q), BF16),
        input_output_aliases={6: 0},
        scratch_shapes=scratch(1, CTX_LEN),
        compiler_params=_cparams(1),
        name="attention_ctx",
    )(p, p, p, selk, selvt, eye, oc)


def _final_kernel(sx_ref, sc_ref, sig_ref, gg_ref, ua_ref, of_ref, ob_ref, oc_ref, mod_ref,
                  wc_ref, wg_ref, wa_ref, wo_ref, gn_ref, o_ref):
    i = pl.program_id(0)
    row = jnp.where(i >= N_XT, BATCH, i // TILES_PER_SEQ)
    ya = jnp.dot(ua_ref[...], wc_ref[...], preferred_element_type=F32)
    og = of_ref[...].astype(F32) + ob_ref[...].astype(F32)
    gn = gn_ref[...]
    parts = []
    for h in range(GLA_HEADS):
        oh = og[:, h * GLA_DV:(h + 1) * GLA_DV]
        ms = jnp.mean(oh * oh, axis=-1, keepdims=True)
        parts.append(oh * lax.rsqrt(ms + EPS) * gn)
    on = jnp.concatenate(parts, axis=1) * gg_ref[...].astype(F32)
    yb = jnp.dot(on.astype(BF16), wg_ref[...], preferred_element_type=F32)
    yc = jnp.dot(oc_ref[...], wa_ref[...], preferred_element_type=F32)
    merged = (sig_ref[:, 0:D_MODEL].astype(F32) * ya
              + sig_ref[:, D_MODEL:2 * D_MODEL].astype(F32) * yb
              + sig_ref[:, 2 * D_MODEL:3 * D_MODEL].astype(F32) * yc)
    out = jnp.dot(merged.astype(BF16), wo_ref[...], preferred_element_type=F32)
    gate = mod_ref[pl.ds(row, 1), 2 * D_MODEL:3 * D_MODEL]
    o_ref[...] = jnp.where(i >= N_XT, sc_ref[...], sx_ref[...]) + gate * out


def _final(l, n_tiles, xs, cs, ctx_off, p, ua, o_gla, oc, mod, wc, wg, wa, wo, gn):
    rowblk = lambda i: (i, 0)
    wspec = lambda k: pl.BlockSpec((None, k, D_MODEL), lambda i: (l, 0, 0))
    return pl.pallas_call(
        _final_kernel,
        grid=(n_tiles,),
        in_specs=_stream_specs(ctx_off) + [
                  pl.BlockSpec((TM, 3 * D_MODEL), lambda i: (i, P_SIG // (3 * D_MODEL))),
                  pl.BlockSpec((TM, GLA_V), lambda i: (i, P_GG // GLA_V)),
                  pl.BlockSpec((TM, D_CONV), rowblk),
                  pl.BlockSpec((TM, GLA_V), rowblk),
                  pl.BlockSpec((TM, GLA_V), rowblk),
                  pl.BlockSpec((TM, ATT_Q), rowblk),
                  pl.BlockSpec((None, 16, 3 * D_MODEL), lambda i: (l, 0, 0)),
                  wspec(D_CONV), wspec(GLA_V), wspec(ATT_Q), wspec(D_MODEL),
                  pl.BlockSpec((None, 1, GLA_DV), lambda i: (l, 0, 0))],
        out_specs=pl.BlockSpec((TM, D_MODEL), rowblk),
        out_shape=jax.ShapeDtypeStruct((n_tiles * TM, D_MODEL), F32),
        compiler_params=_cparams(1),
        name="merge_out",
    )(xs, cs, p, p, ua, o_gla[0], o_gla[1], oc, mod, wc, wg, wa, wo, gn)


def _rope_tables():
    t = np.arange(SEQ)
    row = (t // GRID_W).astype(np.float32)
    col = (t % GRID_W).astype(np.float32)
    n_freq = ROPE_AXIS_DIM // 2
    freqs = (np.float32(ROPE_THETA) ** (-np.arange(n_freq, dtype=np.float32) / n_freq)).astype(np.float32)
    ar = row[:, None] * freqs
    ac = col[:, None] * freqs
    cos64 = np.concatenate([np.cos(ar), np.cos(ar), np.cos(ac), np.cos(ac)], axis=1)
    sin64 = np.concatenate([-np.sin(ar), np.sin(ar), -np.sin(ac), np.sin(ac)], axis=1)
    return (np.tile(cos64, (1, ATT_HEADS)).astype(np.float32),
            np.tile(sin64, (1, ATT_HEADS)).astype(np.float32))


def _static_tables():
    bd = np.kron(np.eye(ATT_HEADS, dtype=np.float32), np.ones((ATT_HD, ATT_HD), np.float32))
    idx = np.arange(GLA_CHUNK)
    tri = np.stack([(idx[None, :] <= idx[:, None]), (idx[None, :] >= idx[:, None])]).astype(np.float32)
    selk = np.zeros((ATT_KV_HEADS, 2, 2 * ATT_KV, LANES), np.float32)
    selvt = np.zeros((ATT_KV_HEADS, VROWS, 2 * ATT_KV), np.float32)
    for h in range(ATT_KV_HEADS):
        for dd in range(ATT_HD):
            selvt[h, dd, ATT_KV + h * ATT_HD + dd] = 1.0
            for e in range(2):
                selk[h, e, h * ATT_HD + dd, e * ATT_HD + dd] = 1.0
    shifts = np.stack([np.eye(CONV_SH, k=r, dtype=np.float32) for r in range(SUBLANES)])
    return bd, tri, selk, selvt, shifts


def _split_cols(a):
    pad = jnp.zeros(a.shape[:-1] + (LANES - 2 * GLA_RANK,), a.dtype)
    return (a[..., :O_LR], a[..., O_AQ:], jnp.concatenate([a[..., O_LR:O_AQ], pad], axis=-1))


def kernel(x, c, ctx, c_ctx, norm_g, w_mod, b_mod, w_in, b_in, conv_dw_w, conv_dw_b, conv_ln_g,
           conv_ln_b, w_conv_out, gla_w_gate, gla_b_gate, gla_norm_g, w_gla_out, q_norm_g,
           k_norm_g, w_attn_out, w_out):
    cos_np, sin_np = _rope_tables()
    bd_np, tri_np, selk_np, selvt_np, shifts_np = _static_tables()
    shifts = jnp.asarray(shifts_np, BF16)
    cos_t, sin_t = jnp.asarray(cos_np), jnp.asarray(sin_np)
    bd = jnp.asarray(bd_np, BF16)
    tri = jnp.asarray(tri_np)
    selk = jnp.asarray(selk_np, BF16)
    selvt = jnp.asarray(selvt_np, BF16)
    eye = jnp.eye(LANES, dtype=BF16)

    w_abc = tuple(w.astype(BF16) for w in _split_cols(w_in))
    b_abc = tuple(b.reshape(DEPTH, 1, b.shape[-1]) for b in _split_cols(b_in))
    cc = jnp.concatenate([c, c_ctx[None, :], jnp.zeros((16 - BATCH - 1, D_MODEL), F32)], axis=0)
    dw_w = jnp.broadcast_to(conv_dw_w[:, :, None, :], (DEPTH, CONV_WIDTH, SUBLANES, D_CONV))
    r3 = lambda a: a.reshape(DEPTH, 1, a.shape[-1])
    wup = jnp.zeros((DEPTH, LANES, 2 * GLA_QK), F32)
    wup = wup.at[:, 0:GLA_RANK, 0:GLA_QK].set(gla_w_gate[:, 0])
    wup = wup.at[:, GLA_RANK:2 * GLA_RANK, GLA_QK:].set(gla_w_gate[:, 1])
    wuh = wup.astype(BF16)
    wul = (wup - wuh.astype(F32)).astype(BF16)
    bup = gla_b_gate.reshape(DEPTH, 1, 2 * GLA_QK)
    qg = jnp.tile(q_norm_g, (1, ATT_HEADS)).reshape(DEPTH, 1, ATT_Q)
    kg = jnp.tile(k_norm_g, (1, ATT_KV_HEADS)).reshape(DEPTH, 1, ATT_KV)
    wc, wg, wa, wo = (w.astype(BF16) for w in (w_conv_out, w_gla_out, w_attn_out, w_out))

    mod = _modulation(cc, w_mod, b_mod)
    xs, cs, ctx_off = x.reshape(MX, D_MODEL), ctx.reshape(MC, D_MODEL), 0
    for l in range(DEPTH):
        p, dec = _inproj(l, xs, cs, ctx_off, mod, r3(norm_g), w_abc, b_abc, cos_t, sin_t, qg, kg, bd, wuh, wul, bup)
        ua = _conv(l, p, dw_w, r3(conv_dw_b), r3(conv_ln_g), r3(conv_ln_b), shifts)
        o_gla = _gla(p, dec, tri)
        oc = _attn(p, selk, selvt, eye)
        n_tiles = M_ALL // TM if l < DEPTH - 1 else N_XT
        xs = _final(l, n_tiles, xs, cs, ctx_off, p, ua, o_gla, oc, mod, wc, wg, wa, wo, r3(gla_norm_g))
        cs, ctx_off = xs, N_XT
    return xs.reshape(BATCH, SEQ, D_MODEL)
```

```python
import functools
import math

import numpy as np
import jax
import jax.numpy as jnp
from jax import lax
from jax.experimental import pallas as pl
from jax.experimental.pallas import tpu as pltpu

F32 = jnp.float32
BF16 = jnp.bfloat16
HIGHEST = lax.Precision.HIGHEST

D_MODEL = 1024
BATCH = 8
SEQ = 4096
DEPTH = 4
CTX_LEN = 256
GRID_W = 64
EPS = 1e-6
D_CONV = 512
CONV_WIDTH = 31
CONV_PAD = CONV_WIDTH // 2
GLA_HEADS = 4
GLA_DK = 64
GLA_DV = 128
GLA_QK = GLA_HEADS * GLA_DK
GLA_V = GLA_HEADS * GLA_DV
GLA_RANK = 16
GLA_GATE_NORM = 16.0
GLA_CHUNK = 64
ATT_HEADS = 8
ATT_KV_HEADS = 2
ATT_GROUP = ATT_HEADS // ATT_KV_HEADS
ATT_HD = 64
ATT_Q = ATT_HEADS * ATT_HD
ATT_KV = ATT_KV_HEADS * ATT_HD
ROPE_AXIS_DIM = ATT_HD // 2
ROPE_THETA = 10000.0

LANES = 128
SUBLANES = 8
MX = BATCH * SEQ
MC = BATCH * CTX_LEN
M_ALL = MX + MC

O_VAL, O_GLU, O_CGATE = 0, 512, 1024
O_GQ, O_GK, O_GV, O_GG = 1536, 1792, 2048, 2560
O_LR = 3072
O_AQ, O_AK, O_AV, O_AG = 3104, 3616, 3744, 3872
O_MA = 4384
N_IN = 7456

A_VAL, A_GLU, A_CG, A_GQK, A_GV, A_GG = 0, 512, 1024, 1536, 2048, 2560
NA = O_LR
B_AQ, B_AKV, B_AG, B_M = 0, 512, 768, 1280
NB = N_IN - O_AQ

P_SIG = 0
P_U = 3072
P_CG = 3584
P_GV = 4096
P_GD = 4608
GD_W = 3 * GLA_QK
P_GG = 6144
P_AQ = 6656
P_AG = 7168
P_AKV = 7680
NP = 7936

TM = 512
N_XT = MX // TM
N_CT = MC // TM
TILES_PER_SEQ = SEQ // TM
TC = 256
XB = MX // TC
BLK_PER_SEQ = SEQ // TC
HALO = 16
CONV_RC = 32
CONV_SH = TC // 2 + 2 * HALO
KB = 512
Q_PRESCALE = (ATT_HD ** -0.5) * math.log2(math.e)

VMEM_LIMIT = 56 * 1024 * 1024


def _cparams(n_axes, vmem=VMEM_LIMIT):
    return pltpu.CompilerParams(dimension_semantics=("arbitrary",) * n_axes,
                                vmem_limit_bytes=vmem)


def _silu(x):
    return x * jax.nn.sigmoid(x)


def _mod_kernel(c_ref, w_ref, b_ref, o_ref):
    s = _silu(c_ref[...])
    o_ref[...] = jnp.dot(s, w_ref[...], preferred_element_type=F32, precision=HIGHEST) + b_ref[...]


def _modulation(cc, w_mod, b_mod):
    nt = 3 * D_MODEL // 1024
    return pl.pallas_call(
        _mod_kernel,
        grid=(DEPTH, nt),
        in_specs=[pl.BlockSpec((16, D_MODEL), lambda l, n: (0, 0)),
                  pl.BlockSpec((None, D_MODEL, 1024), lambda l, n: (l, 0, n)),
                  pl.BlockSpec((None, 1, 1024), lambda l, n: (l, 0, n))],
        out_specs=pl.BlockSpec((None, 16, 1024), lambda l, n: (l, 0, n)),
        out_shape=jax.ShapeDtypeStruct((DEPTH, 16, 3 * D_MODEL), F32),
        compiler_params=_cparams(2),
        name="modulation",
    )(cc, w_mod, b_mod.reshape(DEPTH, 1, 3 * D_MODEL))


def _head_norm(xv, gain, bd):
    ss = jnp.dot((xv * xv).astype(BF16), bd, preferred_element_type=F32)
    return xv * lax.rsqrt(ss * (1.0 / ATT_HD) + EPS) * gain


def _rope(xv, cosv, sinv):
    parts = []
    for s in range(xv.shape[1] // LANES):
        sl = slice(s * LANES, (s + 1) * LANES)
        xs = xv[:, sl]
        up = pltpu.roll(xs, LANES - 16, axis=1)
        dn = pltpu.roll(xs, 16, axis=1)
        lane = lax.broadcasted_iota(jnp.int32, xs.shape, 1)
        partner = jnp.where((lane & 16) == 0, up, dn)
        parts.append(xs * cosv[:, sl] + partner * sinv[:, sl])
    return jnp.concatenate(parts, axis=1) if len(parts) > 1 else parts[0]


def _inproj_kernel(x_ref, c_ref, mod_ref, g_ref, wa_ref, wb_ref, wc_ref, ba_ref, bb_ref, bc_ref,
                   cos_ref, sin_ref, qg_ref, kg_ref,
                   bd_ref, wuh_ref, wul_ref, bup_ref, o_ref, dec_ref):
    i = pl.program_id(0)
    is_ctx = i >= N_XT
    row = jnp.where(is_ctx, BATCH, i // TILES_PER_SEQ)
    x = jnp.where(is_ctx, c_ref[...], x_ref[...])
    ms = jnp.mean(x * x, axis=-1, keepdims=True)
    y = x * lax.rsqrt(ms + EPS) * g_ref[...]
    m = mod_ref[pl.ds(row, 1), :]
    shift = m[:, 0:D_MODEL]
    scale = m[:, D_MODEL:2 * D_MODEL]
    h = (y * (1.0 + scale) + shift).astype(BF16)

    def proj_from(w_ref, b_ref):
        def proj(a, n):
            return jnp.dot(h, w_ref[:, a:a + n], preferred_element_type=F32) + b_ref[:, a:a + n]
        return proj

    proj_a = proj_from(wa_ref, ba_ref)
    proj_b = proj_from(wb_ref, bb_ref)
    proj_c = proj_from(wc_ref, bc_ref)

    def put(a, val):
        o_ref[:, a:a + val.shape[1]] = val.astype(BF16)

    lr = proj_c(0, LANES)
    aq_raw = proj_b(B_AQ, ATT_Q)
    akv = proj_b(B_AKV, 2 * ATT_KV)
    put(P_SIG, jax.nn.sigmoid(proj_b(B_M, 1024)))

    lr_hi = lr.astype(BF16)
    lr_lo = (lr - lr_hi.astype(F32)).astype(BF16)
    wuh = wuh_ref[...]
    z = (jnp.dot(lr_hi, wuh, preferred_element_type=F32)
         + jnp.dot(lr_lo, wuh, preferred_element_type=F32)
         + jnp.dot(lr_hi, wul_ref[...], preferred_element_type=F32)) + bup_ref[...]
    put(P_SIG + 1024, jax.nn.sigmoid(proj_b(B_M + 1024, 1024)))

    cosv = jnp.where(is_ctx, 1.0, cos_ref[...])
    sinv = jnp.where(is_ctx, 0.0, sin_ref[...])
    bd = bd_ref[...]
    aq = _head_norm(aq_raw, qg_ref[...], bd)
    put(P_AQ, _rope(aq, cosv, sinv) * Q_PRESCALE)
    ak = _head_norm(akv[:, :ATT_KV], kg_ref[...], bd[:ATT_KV, :ATT_KV])
    put(P_AKV, _rope(ak, cosv[:, :ATT_KV], sinv[:, :ATT_KV]))
    put(P_AKV + ATT_KV, akv[:, ATT_KV:])
    put(P_SIG + 2048, jax.nn.sigmoid(proj_b(B_M + 2048, 1024)))

    la = (jnp.minimum(z, 0.0) - jnp.log(1.0 + jnp.exp(-jnp.abs(z)))) * (1.0 / GLA_GATE_NORM)
    rowc = lax.broadcasted_iota(jnp.int32, (TM, GLA_QK), 0) & (GLA_CHUNK - 1)
    cf = la[:, :GLA_QK]
    cb = la[:, GLA_QK:]
    sh = 1
    while sh < GLA_CHUNK:
        cf = cf + jnp.where(rowc >= sh, pltpu.roll(cf, sh, axis=0), 0.0)
        cb = cb + jnp.where(rowc < GLA_CHUNK - sh, pltpu.roll(cb, TM - sh, axis=0), 0.0)
        sh *= 2
    n_ch = TM // GLA_CHUNK
    last_f = [cf[c * GLA_CHUNK + GLA_CHUNK - 1:(c + 1) * GLA_CHUNK, :] for c in range(n_ch)]
    last_b = [cb[c * GLA_CHUNK:c * GLA_CHUNK + 1, :] for c in range(n_ch)]
    dec_ref[0] = jnp.concatenate(last_f, axis=0)
    dec_ref[1] = jnp.concatenate(last_b, axis=0)
    qk = proj_a(A_GQK, 512)
    gq = qk[:, :GLA_QK] * (GLA_DK ** -0.5)
    gk = qk[:, GLA_QK:]
    for dd, (cum, last) in enumerate(((cf, last_f), (cb, last_b))):
        tot = jnp.concatenate([jnp.broadcast_to(t, (GLA_CHUNK, GLA_QK)) for t in last], axis=0)
        base = P_GD + dd * GD_W
        put(base, gq * jnp.exp(cum))
        put(base + GLA_QK, gk * jnp.exp(-cum))
        put(base + 2 * GLA_QK, gk * jnp.exp(tot - cum))

    put(P_U, proj_a(A_VAL, 512) * jax.nn.sigmoid(proj_a(A_GLU, 512)))
    put(P_CG, _silu(proj_a(A_CG, 512)))
    put(P_GG, _silu(proj_a(A_GG, 512)))
    put(P_AG, _silu(proj_b(B_AG, ATT_Q)))
    put(P_GV, proj_a(A_GV, 512))


def _stream_specs(ctx_off):
    return [pl.BlockSpec((TM, D_MODEL), lambda i: (jnp.minimum(i, N_XT - 1), 0)),
            pl.BlockSpec((TM, D_MODEL), lambda i: (jnp.maximum(i - N_XT, 0) + ctx_off, 0))]


def _inproj(l, xs, cs, ctx_off, mod, norm_g, w_abc, b_abc, cos_t, sin_t, qg, kg, bd, wuh, wul, bup):
    const = lambda i: (0, 0)
    pos = lambda i: (jnp.where(i >= N_XT, 0, i % TILES_PER_SEQ), 0)
    return pl.pallas_call(
        _inproj_kernel,
        grid=(M_ALL // TM,),
        in_specs=_stream_specs(ctx_off) + [
                  pl.BlockSpec((None, 16, 3 * D_MODEL), lambda i: (l, 0, 0)),
                  pl.BlockSpec((None, 1, D_MODEL), lambda i: (l, 0, 0)),
                  ] + [pl.BlockSpec((None, D_MODEL, w.shape[-1]), lambda i: (l, 0, 0),
                                    pipeline_mode=pl.Buffered(1)) for w in w_abc
                  ] + [pl.BlockSpec((None, 1, b.shape[-1]), lambda i: (l, 0, 0)) for b in b_abc
                  ] + [
                  pl.BlockSpec((TM, ATT_Q), pos),
                  pl.BlockSpec((TM, ATT_Q), pos),
                  pl.BlockSpec((None, 1, ATT_Q), lambda i: (l, 0, 0)),
                  pl.BlockSpec((None, 1, ATT_KV), lambda i: (l, 0, 0)),
                  pl.BlockSpec((ATT_Q, ATT_Q), const),
                  pl.BlockSpec((None, LANES, 2 * GLA_QK), lambda i: (l, 0, 0)),
                  pl.BlockSpec((None, LANES, 2 * GLA_QK), lambda i: (l, 0, 0)),
                  pl.BlockSpec((None, 1, 2 * GLA_QK), lambda i: (l, 0, 0))],
        out_specs=[pl.BlockSpec((TM, NP), lambda i: (i, 0)),
                   pl.BlockSpec((2, TM // GLA_CHUNK, GLA_QK), lambda i: (0, i, 0))],
        out_shape=[jax.ShapeDtypeStruct((M_ALL, NP), BF16),
                   jax.ShapeDtypeStruct((2, M_ALL // GLA_CHUNK, GLA_QK), F32)],
        compiler_params=_cparams(1),
        name="inproj",
    )(xs, cs, mod, norm_g, *w_abc, *b_abc, cos_t, sin_t, qg, kg, bd, wuh, wul, bup)


def _conv_kernel(u_ref, ul_ref, ur_ref, sg_ref, dw_ref, dwb_ref, lng_ref, lnb_ref, sh_ref,
                 o_ref, win_ref, ext_ref):
    i = pl.program_id(0)
    j = i % BLK_PER_SEQ
    is_x = i < XB
    left_ok = jnp.logical_and(is_x, j != 0)
    right_ok = jnp.logical_and(is_x, j != BLK_PER_SEQ - 1)
    zero_h = jnp.zeros((HALO, D_CONV), BF16)
    win_ref[0:HALO, :] = jnp.where(left_ok, ul_ref[...], zero_h)
    win_ref[HALO:HALO + TC, :] = u_ref[...]
    win_ref[HALO + TC:, :] = jnp.where(right_ok, ur_ref[...], zero_h)
    ext_ref[0] = win_ref[...].astype(F32)
    half = TC // 2
    for r in range(1, SUBLANES):
        for a in range(2):
            ext_ref[r, a * half:a * half + CONV_SH, :] = jnp.dot(
                sh_ref[r], win_ref[a * half:a * half + CONV_SH, :], preferred_element_type=F32)
    bias = dwb_ref[...]
    lng = lng_ref[...]
    lnb = lnb_ref[...]
    for c in range(TC // CONV_RC):
        r0 = c * CONV_RC
        acc = jnp.broadcast_to(bias, (CONV_RC, D_CONV))
        for t in range(CONV_WIDTH):
            off = t + HALO - CONV_PAD
            a0 = r0 + (off // SUBLANES) * SUBLANES
            w_t = jnp.concatenate([dw_ref[t]] * (CONV_RC // SUBLANES), axis=0)
            acc = acc + ext_ref[off % SUBLANES, a0:a0 + CONV_RC, :] * w_t
        mu = jnp.mean(acc, axis=-1, keepdims=True)
        d = acc - mu
        var = jnp.mean(d * d, axis=-1, keepdims=True)
        yn = d * lax.rsqrt(var + EPS) * lng + lnb
        o_ref[r0:r0 + CONV_RC, :] = (_silu(yn) * sg_ref[r0:r0 + CONV_RC, :].astype(F32)).astype(BF16)


def _conv(l, p, dw_w, dw_b, ln_g, ln_b, shifts):
    nhb = M_ALL // HALO
    per = TC // HALO
    cu = P_U // D_CONV
    cg = P_CG // D_CONV
    vec = lambda i: (l, 0, 0)
    return pl.pallas_call(
        _conv_kernel,
        grid=(M_ALL // TC,),
        in_specs=[pl.BlockSpec((TC, D_CONV), lambda i: (i, cu)),
                  pl.BlockSpec((HALO, D_CONV), lambda i: (jnp.maximum(i * per - 1, 0), cu)),
                  pl.BlockSpec((HALO, D_CONV), lambda i: (jnp.minimum((i + 1) * per, nhb - 1), cu)),
                  pl.BlockSpec((TC, D_CONV), lambda i: (i, cg)),
                  pl.BlockSpec((None, CONV_WIDTH, SUBLANES, D_CONV), lambda i: (l, 0, 0, 0)),
                  pl.BlockSpec((None, 1, D_CONV), vec),
                  pl.BlockSpec((None, 1, D_CONV), vec),
                  pl.BlockSpec((None, 1, D_CONV), vec),
                  pl.BlockSpec((SUBLANES, CONV_SH, CONV_SH), lambda i: (0, 0, 0))],
        out_specs=pl.BlockSpec((TC, D_CONV), lambda i: (i, 0)),
        out_shape=jax.ShapeDtypeStruct((M_ALL, D_CONV), BF16),
        scratch_shapes=[pltpu.VMEM((TC + 2 * HALO, D_CONV), BF16),
                        pltpu.VMEM((SUBLANES, TC + 2 * HALO, D_CONV), F32)],
        compiler_params=_cparams(1),
        name="conv",
    )(p, p, p, p, dw_w, dw_b, ln_g, ln_b, shifts)


def _gla_kernel(qf_ref, qb_ref, vf_ref, vb_ref, df_ref, db_ref, tri_ref, of_ref, ob_ref, st_ref):
    step = pl.program_id(1)

    @pl.when(step == 0)
    def _():
        st_ref[...] = jnp.zeros_like(st_ref)

    lane_h = lax.broadcasted_iota(jnp.int32, (GLA_CHUNK, GLA_QK), 1) // GLA_DK
    srow_h = lax.broadcasted_iota(jnp.int32, (GLA_V, GLA_QK), 0) // GLA_DV
    scol_h = lax.broadcasted_iota(jnp.int32, (GLA_V, GLA_QK), 1) // GLA_DK
    smask = srow_h == scol_h
    nt = (((1,), (1,)), ((), ()))
    n_chunks = TC // GLA_CHUNK
    dirs = ((qf_ref, vf_ref, df_ref, of_ref), (qb_ref, vb_ref, db_ref, ob_ref))
    work = []
    for n in range(n_chunks):
        for d, (qkk_ref, v_ref, dec_ref, o_ref) in enumerate(dirs):
            c = n if d == 0 else n_chunks - 1 - n
            rows = slice(c * GLA_CHUNK, (c + 1) * GLA_CHUNK)
            q_in = qkk_ref[rows, 0:GLA_QK]
            k_in = qkk_ref[rows, GLA_QK:2 * GLA_QK]
            k_st = qkk_ref[rows, 2 * GLA_QK:3 * GLA_QK]
            v = v_ref[rows, :]
            decay = jnp.exp(dec_ref[c:c + 1, :])
            tri4 = jnp.concatenate([tri_ref[d]] * GLA_HEADS, axis=0) > 0.5
            q_stack = jnp.concatenate(
                [jnp.where(lane_h == h, q_in, jnp.zeros_like(q_in)) for h in range(GLA_HEADS)],
                axis=0)
            att = lax.dot_general(q_stack, k_in, nt, preferred_element_type=F32)
            att = jnp.where(tri4, att, 0.0).astype(BF16)
            kvt = lax.dot_general(v, k_st, (((0,), (0,)), ((), ())),
                                  preferred_element_type=F32)
            work.append((d, o_ref, rows, q_in, v, decay, att, jnp.where(smask, kvt, 0.0)))
    for d, o_ref, rows, q_in, v, decay, att, kvt in work:
        st_old = st_ref[d]
        o_inter = lax.dot_general(q_in, st_old.astype(BF16), nt, preferred_element_type=F32)
        o_intra = jnp.concatenate(
            [jnp.dot(att[h * GLA_CHUNK:(h + 1) * GLA_CHUNK, :],
                     v[:, h * GLA_DV:(h + 1) * GLA_DV], preferred_element_type=F32)
             for h in range(GLA_HEADS)], axis=1)
        o_ref[rows, :] = (o_intra + o_inter).astype(BF16)
        st_ref[d] = st_old * decay + kvt


def _gla(p, dec, tri):
    def rbf(b, s):
        return jnp.where(s == 0, XB + b, b * BLK_PER_SEQ + s - 1)

    def rbb(b, s):
        return jnp.where(s == 0, XB + b, b * BLK_PER_SEQ + BLK_PER_SEQ - s)

    cgd = P_GD // GD_W
    cv = P_GV // GLA_V
    n_ch = TC // GLA_CHUNK
    dec4 = dec.reshape(2, M_ALL // TC, n_ch, GLA_QK)
    out = jax.ShapeDtypeStruct((M_ALL, GLA_V), BF16)
    return pl.pallas_call(
        _gla_kernel,
        grid=(BATCH, BLK_PER_SEQ + 1),
        in_specs=[pl.BlockSpec((TC, GD_W), lambda b, s: (rbf(b, s), cgd)),
                  pl.BlockSpec((TC, GD_W), lambda b, s: (rbb(b, s), cgd + 1)),
                  pl.BlockSpec((TC, GLA_V), lambda b, s: (rbf(b, s), cv)),
                  pl.BlockSpec((TC, GLA_V), lambda b, s: (rbb(b, s), cv)),
                  pl.BlockSpec((None, None, n_ch, GLA_QK), lambda b, s: (0, rbf(b, s), 0, 0)),
                  pl.BlockSpec((None, None, n_ch, GLA_QK), lambda b, s: (1, rbb(b, s), 0, 0)),
                  pl.BlockSpec((2, GLA_CHUNK, GLA_CHUNK), lambda b, s: (0, 0, 0))],
        out_specs=[pl.BlockSpec((TC, GLA_V), lambda b, s: (rbf(b, s), 0)),
                   pl.BlockSpec((TC, GLA_V), lambda b, s: (rbb(b, s), 0))],
        out_shape=[out, out],
        scratch_shapes=[pltpu.VMEM((2, GLA_V, GLA_QK), F32)],
        compiler_params=_cparams(2),
        name="gla",
    )(p, p, p, p, dec4, dec4, tri)


NKEYS = CTX_LEN + SEQ
VROWS = 80
NT_DIMS = (((1,), (1,)), ((), ()))


def _make_attn_kernel(n_tiles, with_x):
    n_xb = SEQ // KB

    def kernel(*refs):
        if with_x:
            (q_ref, g_ref, ckv_ref, xkv_ref, selk_ref, selvt_ref, eye_ref, o_ref,
             kz_ref, vtc_ref, vtx_ref, qt_ref, m_ref, al_ref, acc_ref, s_ref) = refs
            key_srcs = ((ckv_ref, 0, CTX_LEN), (xkv_ref, CTX_LEN, SEQ))
        else:
            (q_ref, g_ref, ckv_ref, selk_ref, selvt_ref, eye_ref, _, o_ref,
             kz_ref, vtc_ref, qt_ref, m_ref, al_ref, acc_ref, s_ref) = refs
            key_srcs = ((ckv_ref, 0, CTX_LEN),)
        row_v = lax.broadcasted_iota(jnp.int32, (VROWS, 1), 0)

        def ext_values(svt, blk):
            vt = lax.dot_general(svt, blk, NT_DIMS, preferred_element_type=F32)
            return jnp.where(row_v == ATT_HD, 1.0, vt).astype(BF16)

        def prepare():
            for h in range(ATT_KV_HEADS):
                for e in range(2):
                    sk = selk_ref[h, e]
                    for (src, r0, n) in key_srcs:
                        for r in range(0, n, 1024):
                            nr = min(1024, n - r)
                            kz_ref[h, e, r0 + r:r0 + r + nr, :] = jnp.dot(
                                src[r:r + nr, :], sk, preferred_element_type=F32).astype(BF16)
                svt = selvt_ref[h]
                vtc_ref[h] = ext_values(svt, ckv_ref[...])
                if with_x:
                    for t in range(n_xb):
                        vtx_ref[h, t] = ext_values(svt, xkv_ref[t * KB:(t + 1) * KB, :])

        if with_x:
            pl.when(pl.program_id(1) == 0)(prepare)
        else:
            prepare()

        eye = eye_ref[...]
        for tile in range(n_tiles):
            for pr in range(ATT_HEADS // 2):
                qt_ref[tile, pr] = lax.dot_general(
                    eye, q_ref[tile * TC:(tile + 1) * TC, pr * LANES:(pr + 1) * LANES], NT_DIMS,
                    preferred_element_type=F32).astype(BF16)
        m_ref[...] = jnp.full_like(m_ref, -jnp.inf)
        acc_ref[...] = jnp.zeros_like(acc_ref)

        def scores(nxt, hd):
            tile, slot, k0, nk, _ = nxt
            h, pr, e = hd // ATT_GROUP, hd // 2, hd % 2
            s_ref[slot, hd, 0:nk, :] = jnp.dot(kz_ref[h, e, pl.ds(k0, nk), :], qt_ref[tile, pr],
                                               preferred_element_type=F32)

        def stage(cur, nxt):
            if cur is not None:
                tile, slot, _, nk, vt_of = cur
                for hd in range(ATT_HEADS):
                    s3 = s_ref[slot, hd, 0:nk, :].reshape(nk // SUBLANES, SUBLANES, TC)
                    m_col = jnp.max(jnp.max(s3, axis=0), axis=0, keepdims=True)
                    m_prev = m_ref[tile, hd]
                    m_next = jnp.maximum(m_prev, m_col)
                    al_ref[tile, hd] = jnp.exp2(m_prev - m_next)
                    m_ref[tile, hd] = m_next
            for hd in range(ATT_HEADS):
                if nxt is not None:
                    scores(nxt, hd)
                if cur is not None:
                    s3 = s_ref[slot, hd, 0:nk, :].reshape(nk // SUBLANES, SUBLANES, TC)
                    p = jnp.exp2(s3 - m_ref[tile, hd][None]).reshape(nk, TC).astype(BF16)
                    pv = jnp.dot(vt_of(hd // ATT_GROUP), p, preferred_element_type=F32)
                    acc3 = (acc_ref[tile, hd].reshape(VROWS // SUBLANES, SUBLANES, TC)
                            * al_ref[tile, hd][None])
                    acc_ref[tile, hd] = acc3.reshape(VROWS, TC) + pv

        def finish(tile):
            outs = []
            for hd in range(ATT_HEADS):
                a = acc_ref[tile, hd]
                outs.append(a[0:ATT_HD, :] / a[ATT_HD:ATT_HD + 1, :])
            o_nat = jnp.concatenate(outs, axis=0).T
            rows = slice(tile * TC, (tile + 1) * TC)
            o_ref[rows, :] = (o_nat * g_ref[rows, :].astype(F32)).astype(BF16)

        def ctx_stage(tile, slot):
            return (tile, slot, 0, CTX_LEN, lambda h: vtc_ref[h])

        def x_stage(tile, t, slot):
            k0 = CTX_LEN + t * KB
            if not isinstance(t, int):
                k0 = pl.multiple_of(k0, CTX_LEN)
            return (tile, slot, k0, KB, lambda h: vtx_ref[h, t])

        if not with_x:
            stage(None, ctx_stage(0, 0))
            stage(ctx_stage(0, 0), None)
            finish(0)
            return
        stage(None, ctx_stage(0, 1))
        for tile in range(n_tiles):
            s0 = tile % 2
            stage(ctx_stage(tile, 1 - s0), x_stage(tile, 0, s0))

            def body(j, carry, tile=tile, s0=s0):
                t = 2 * j
                stage(x_stage(tile, t, s0), x_stage(tile, t + 1, 1 - s0))
                stage(x_stage(tile, t + 1, 1 - s0), x_stage(tile, t + 2, s0))
                return carry
            lax.fori_loop(0, (n_xb - 2) // 2, body, 0)
            stage(x_stage(tile, n_xb - 2, s0), x_stage(tile, n_xb - 1, 1 - s0))
            nxt = ctx_stage(tile + 1, s0) if tile + 1 < n_tiles else None
            stage(x_stage(tile, n_xb - 1, 1 - s0), nxt)
            finish(tile)

    return kernel


ATT_XT = 4


def _attn(p, selk, selvt, eye):
    cq = P_AQ // ATT_Q
    cg = P_AG // ATT_Q
    ckv = P_AKV // (2 * ATT_KV)
    tq = ATT_XT * TC
    per_seq = SEQ // tq
    consts = [pl.BlockSpec((ATT_KV_HEADS, 2, 2 * ATT_KV, LANES), lambda *_: (0, 0, 0, 0)),
              pl.BlockSpec((ATT_KV_HEADS, VROWS, 2 * ATT_KV), lambda *_: (0, 0, 0)),
              pl.BlockSpec((LANES, LANES), lambda *_: (0, 0))]

    def scratch(n_tiles, n_keys):
        return [pltpu.VMEM((ATT_KV_HEADS, 2, n_keys, LANES), BF16),
                pltpu.VMEM((ATT_KV_HEADS, VROWS, CTX_LEN), BF16)
                ] + ([pltpu.VMEM((ATT_KV_HEADS, SEQ // KB, VROWS, KB), BF16)] if n_keys > CTX_LEN else []
                ) + [pltpu.VMEM((n_tiles, ATT_HEADS // 2, LANES, TC), BF16),
                     pltpu.VMEM((n_tiles, ATT_HEADS, SUBLANES, TC), F32),
                     pltpu.VMEM((n_tiles, ATT_HEADS, SUBLANES, TC), F32),
                     pltpu.VMEM((n_tiles, ATT_HEADS, VROWS, TC), F32),
                     pltpu.VMEM((2 if n_keys > CTX_LEN else 1, ATT_HEADS,
                                 KB if n_keys > CTX_LEN else CTX_LEN, TC), F32)]

    oc = pl.pallas_call(
        _make_attn_kernel(ATT_XT, True),
        grid=(BATCH, per_seq),
        in_specs=[pl.BlockSpec((tq, ATT_Q), lambda b, qi: (b * per_seq + qi, cq)),
                  pl.BlockSpec((tq, ATT_Q), lambda b, qi: (b * per_seq + qi, cg)),
                  pl.BlockSpec((CTX_LEN, 2 * ATT_KV), lambda b, qi: (XB + b, ckv)),
                  pl.BlockSpec((SEQ, 2 * ATT_KV), lambda b, qi: (b, ckv))] + consts,
        out_specs=pl.BlockSpec((tq, ATT_Q), lambda b, qi: (b * per_seq + qi, 0)),
        out_shape=jax.ShapeDtypeStruct((M_ALL, ATT_Q), BF16),
        scratch_shapes=scratch(ATT_XT, NKEYS),
        compiler_params=_cparams(2),
        name="attention",
    )(p, p, p, p, selk, selvt, eye)
    return pl.pallas_call(
        _make_attn_kernel(1, False),
        grid=(BATCH,),
        in_specs=[pl.BlockSpec((TC, ATT_Q), lambda b: (XB + b, cq)),
                  pl.BlockSpec((TC, ATT_Q), lambda b: (XB + b, cg)),
                  pl.BlockSpec((CTX_LEN, 2 * ATT_KV), lambda b: (XB + b, ckv))] + consts
                 + [pl.BlockSpec(memory_space=pl.ANY)],
        out_specs=pl.BlockSpec((TC, ATT_Q), lambda b: (XB + b, 0)),
        out_shape=jax.ShapeDtypeStruct((M_ALL, ATT_Q), BF16),
        input_output_aliases={6: 0},
        scratch_shapes=scratch(1, CTX_LEN),
        compiler_params=_cparams(1),
        name="attention_ctx",
    )(p, p, p, selk, selvt, eye, oc)


def _final_kernel(sx_ref, sc_ref, sig_ref, gg_ref, ua_ref, of_ref, ob_ref, oc_ref, mod_ref,
                  wc_ref, wg_ref, wa_ref, wo_ref, gn_ref, o_ref):
    i = pl.program_id(0)
    row = jnp.where(i >= N_XT, BATCH, i // TILES_PER_SEQ)
    ya = jnp.dot(ua_ref[...], wc_ref[...], preferred_element_type=F32)
    og = of_ref[...].astype(F32) + ob_ref[...].astype(F32)
    gn = gn_ref[...]
    parts = []
    for h in range(GLA_HEADS):
        oh = og[:, h * GLA_DV:(h + 1) * GLA_DV]
        ms = jnp.mean(oh * oh, axis=-1, keepdims=True)
        parts.append(oh * lax.rsqrt(ms + EPS) * gn)
    on = jnp.concatenate(parts, axis=1) * gg_ref[...].astype(F32)
    yb = jnp.dot(on.astype(BF16), wg_ref[...], preferred_element_type=F32)
    yc = jnp.dot(oc_ref[...], wa_ref[...], preferred_element_type=F32)
    merged = (sig_ref[:, 0:D_MODEL].astype(F32) * ya
              + sig_ref[:, D_MODEL:2 * D_MODEL].astype(F32) * yb
              + sig_ref[:, 2 * D_MODEL:3 * D_MODEL].astype(F32) * yc)
    out = jnp.dot(merged.astype(BF16), wo_ref[...], preferred_element_type=F32)
    gate = mod_ref[pl.ds(row, 1), 2 * D_MODEL:3 * D_MODEL]
    o_ref[...] = jnp.where(i >= N_XT, sc_ref[...], sx_ref[...]) + gate * out


def _final(l, n_tiles, xs, cs, ctx_off, p, ua, o_gla, oc, mod, wc, wg, wa, wo, gn):
    rowblk = lambda i: (i, 0)
    wspec = lambda k: pl.BlockSpec((None, k, D_MODEL), lambda i: (l, 0, 0))
    return pl.pallas_call(
        _final_kernel,
        grid=(n_tiles,),
        in_specs=_stream_specs(ctx_off) + [
                  pl.BlockSpec((TM, 3 * D_MODEL), lambda i: (i, P_SIG // (3 * D_MODEL))),
                  pl.BlockSpec((TM, GLA_V), lambda i: (i, P_GG // GLA_V)),
                  pl.BlockSpec((TM, D_CONV), rowblk),
                  pl.BlockSpec((TM, GLA_V), rowblk),
                  pl.BlockSpec((TM, GLA_V), rowblk),
                  pl.BlockSpec((TM, ATT_Q), rowblk),
                  pl.BlockSpec((None, 16, 3 * D_MODEL), lambda i: (l, 0, 0)),
                  wspec(D_CONV), wspec(GLA_V), wspec(ATT_Q), wspec(D_MODEL),
                  pl.BlockSpec((None, 1, GLA_DV), lambda i: (l, 0, 0))],
        out_specs=pl.BlockSpec((TM, D_MODEL), rowblk),
        out_shape=jax.ShapeDtypeStruct((n_tiles * TM, D_MODEL), F32),
        compiler_params=_cparams(1),
        name="merge_out",
    )(xs, cs, p, p, ua, o_gla[0], o_gla[1], oc, mod, wc, wg, wa, wo, gn)


def _rope_tables():
    t = np.arange(SEQ)
    row = (t // GRID_W).astype(np.float32)
    col = (t % GRID_W).astype(np.float32)
    n_freq = ROPE_AXIS_DIM // 2
    freqs = (np.float32(ROPE_THETA) ** (-np.arange(n_freq, dtype=np.float32) / n_freq)).astype(np.float32)
    ar = row[:, None] * freqs
    ac = col[:, None] * freqs
    cos64 = np.concatenate([np.cos(ar), np.cos(ar), np.cos(ac), np.cos(ac)], axis=1)
    sin64 = np.concatenate([-np.sin(ar), np.sin(ar), -np.sin(ac), np.sin(ac)], axis=1)
    return (np.tile(cos64, (1, ATT_HEADS)).astype(np.float32),
            np.tile(sin64, (1, ATT_HEADS)).astype(np.float32))


def _static_tables():
    bd = np.kron(np.eye(ATT_HEADS, dtype=np.float32), np.ones((ATT_HD, ATT_HD), np.float32))
    idx = np.arange(GLA_CHUNK)
    tri = np.stack([(idx[None, :] <= idx[:, None]), (idx[None, :] >= idx[:, None])]).astype(np.float32)
    selk = np.zeros((ATT_KV_HEADS, 2, 2 * ATT_KV, LANES), np.float32)
    selvt = np.zeros((ATT_KV_HEADS, VROWS, 2 * ATT_KV), np.float32)
    for h in range(ATT_KV_HEADS):
        for dd in range(ATT_HD):
            selvt[h, dd, ATT_KV + h * ATT_HD + dd] = 1.0
            for e in range(2):
                selk[h, e, h * ATT_HD + dd, e * ATT_HD + dd] = 1.0
    shifts = np.stack([np.eye(CONV_SH, k=r, dtype=np.float32) for r in range(SUBLANES)])
    return bd, tri, selk, selvt, shifts


def _split_cols(a):
    pad = jnp.zeros(a.shape[:-1] + (LANES - 2 * GLA_RANK,), a.dtype)
    return (a[..., :O_LR], a[..., O_AQ:], jnp.concatenate([a[..., O_LR:O_AQ], pad], axis=-1))


def kernel(x, c, ctx, c_ctx, norm_g, w_mod, b_mod, w_in, b_in, conv_dw_w, conv_dw_b, conv_ln_g,
           conv_ln_b, w_conv_out, gla_w_gate, gla_b_gate, gla_norm_g, w_gla_out, q_norm_g,
           k_norm_g, w_attn_out, w_out):
    cos_np, sin_np = _rope_tables()
    bd_np, tri_np, selk_np, selvt_np, shifts_np = _static_tables()
    shifts = jnp.asarray(shifts_np, BF16)
    cos_t, sin_t = jnp.asarray(cos_np), jnp.asarray(sin_np)
    bd = jnp.asarray(bd_np, BF16)
    tri = jnp.asarray(tri_np)
    selk = jnp.asarray(selk_np, BF16)
    selvt = jnp.asarray(selvt_np, BF16)
    eye = jnp.eye(LANES, dtype=BF16)

    w_abc = tuple(w.astype(BF16) for w in _split_cols(w_in))
    b_abc = tuple(b.reshape(DEPTH, 1, b.shape[-1]) for b in _split_cols(b_in))
    cc = jnp.concatenate([c, c_ctx[None, :], jnp.zeros((16 - BATCH - 1, D_MODEL), F32)], axis=0)
    dw_w = jnp.broadcast_to(conv_dw_w[:, :, None, :], (DEPTH, CONV_WIDTH, SUBLANES, D_CONV))
    r3 = lambda a: a.reshape(DEPTH, 1, a.shape[-1])
    wup = jnp.zeros((DEPTH, LANES, 2 * GLA_QK), F32)
    wup = wup.at[:, 0:GLA_RANK, 0:GLA_QK].set(gla_w_gate[:, 0])
    wup = wup.at[:, GLA_RANK:2 * GLA_RANK, GLA_QK:].set(gla_w_gate[:, 1])
    wuh = wup.astype(BF16)
    wul = (wup - wuh.astype(F32)).astype(BF16)
    bup = gla_b_gate.reshape(DEPTH, 1, 2 * GLA_QK)
    qg = jnp.tile(q_norm_g, (1, ATT_HEADS)).reshape(DEPTH, 1, ATT_Q)
    kg = jnp.tile(k_norm_g, (1, ATT_KV_HEADS)).reshape(DEPTH, 1, ATT_KV)
    wc, wg, wa, wo = (w.astype(BF16) for w in (w_conv_out, w_gla_out, w_attn_out, w_out))

    mod = _modulation(cc, w_mod, b_mod)
    xs, cs, ctx_off = x.reshape(MX, D_MODEL), ctx.reshape(MC, D_MODEL), 0
    for l in range(DEPTH):
        p, dec = _inproj(l, xs, cs, ctx_off, mod, r3(norm_g), w_abc, b_abc, cos_t, sin_t, qg, kg, bd, wuh, wul, bup)
        ua = _conv(l, p, dw_w, r3(conv_dw_b), r3(conv_ln_g), r3(conv_ln_b), shifts)
        o_gla = _gla(p, dec, tri)
        oc = _attn(p, selk, selvt, eye)
        n_tiles = M_ALL // TM if l < DEPTH - 1 else N_XT
        xs = _final(l, n_tiles, xs, cs, ctx_off, p, ua, o_gla, oc, mod, wc, wg, wa, wo, r3(gla_norm_g))
        cs, ctx_off = xs, N_XT
    return xs.reshape(BATCH, SEQ, D_MODEL)
```

```python
import functools
import math

import numpy as np
import jax
import jax.numpy as jnp
from jax import lax
from jax.experimental import pallas as pl
from jax.experimental.pallas import tpu as pltpu

F32 = jnp.float32
BF16 = jnp.bfloat16
HIGHEST = lax.Precision.HIGHEST

D_MODEL = 1024
BATCH = 8
SEQ = 4096
DEPTH = 4
CTX_LEN = 256
GRID_W = 64
EPS = 1e-6
D_CONV = 512
CONV_WIDTH = 31
CONV_PAD = CONV_WIDTH // 2
GLA_HEADS = 4
GLA_DK = 64
GLA_DV = 128
GLA_QK = GLA_HEADS * GLA_DK
GLA_V = GLA_HEADS * GLA_DV
GLA_RANK = 16
GLA_GATE_NORM = 16.0
GLA_CHUNK = 64
ATT_HEADS = 8
ATT_KV_HEADS = 2
ATT_GROUP = ATT_HEADS // ATT_KV_HEADS
ATT_HD = 64
ATT_Q = ATT_HEADS * ATT_HD
ATT_KV = ATT_KV_HEADS * ATT_HD
ROPE_AXIS_DIM = ATT_HD // 2
ROPE_THETA = 10000.0

LANES = 128
SUBLANES = 8
MX = BATCH * SEQ
MC = BATCH * CTX_LEN
M_ALL = MX + MC

O_VAL, O_GLU, O_CGATE = 0, 512, 1024
O_GQ, O_GK, O_GV, O_GG = 1536, 1792, 2048, 2560
O_LR = 3072
O_AQ, O_AK, O_AV, O_AG = 3104, 3616, 3744, 3872
O_MA = 4384
N_IN = 7456

A_VAL, A_GLU, A_CG, A_GQK, A_GV, A_GG = 0, 512, 1024, 1536, 2048, 2560
NA = O_LR
B_AQ, B_AKV, B_AG, B_M = 0, 512, 768, 1280
NB = N_IN - O_AQ

P_SIG = 0
P_U = 3072
P_CG = 3584
P_GV = 4096
P_GD = 4608
GD_W = 3 * GLA_QK
P_GG = 6144
P_AQ = 6656
P_AG = 7168
P_AKV = 7680
NP = 7936

TM = 512
N_XT = MX // TM
N_CT = MC // TM
TILES_PER_SEQ = SEQ // TM
TC = 256
XB = MX // TC
BLK_PER_SEQ = SEQ // TC
HALO = 16
CONV_RC = 32
CONV_SH = TC // 2 + 2 * HALO
KB = 512
Q_PRESCALE = (ATT_HD ** -0.5) * math.log2(math.e)

VMEM_LIMIT = 56 * 1024 * 1024


def _cparams(n_axes, vmem=VMEM_LIMIT):
    return pltpu.CompilerParams(dimension_semantics=("arbitrary",) * n_axes,
                                vmem_limit_bytes=vmem)


def _silu(x):
    return x * jax.nn.sigmoid(x)


def _mod_kernel(c_ref, w_ref, b_ref, o_ref):
    s = _silu(c_ref[...])
    o_ref[...] = jnp.dot(s, w_ref[...], preferred_element_type=F32, precision=HIGHEST) + b_ref[...]


def _modulation(cc, w_mod, b_mod):
    nt = 3 * D_MODEL // 1024
    return pl.pallas_call(
        _mod_kernel,
        grid=(DEPTH, nt),
        in_specs=[pl.BlockSpec((16, D_MODEL), lambda l, n: (0, 0)),
                  pl.BlockSpec((None, D_MODEL, 1024), lambda l, n: (l, 0, n)),
                  pl.BlockSpec((None, 1, 1024), lambda l, n: (l, 0, n))],
        out_specs=pl.BlockSpec((None, 16, 1024), lambda l, n: (l, 0, n)),
        out_shape=jax.ShapeDtypeStruct((DEPTH, 16, 3 * D_MODEL), F32),
        compiler_params=_cparams(2),
        name="modulation",
    )(cc, w_mod, b_mod.reshape(DEPTH, 1, 3 * D_MODEL))


def _head_norm(xv, gain, bd):
    ss = jnp.dot((xv * xv).astype(BF16), bd, preferred_element_type=F32)
    return xv * lax.rsqrt(ss * (1.0 / ATT_HD) + EPS) * gain


def _rope(xv, cosv, sinv):
    parts = []
    for s in range(xv.shape[1] // LANES):
        sl = slice(s * LANES, (s + 1) * LANES)
        xs = xv[:, sl]
        up = pltpu.roll(xs, LANES - 16, axis=1)
        dn = pltpu.roll(xs, 16, axis=1)
        lane = lax.broadcasted_iota(jnp.int32, xs.shape, 1)
        partner = jnp.where((lane & 16) == 0, up, dn)
        parts.append(xs * cosv[:, sl] + partner * sinv[:, sl])
    return jnp.concatenate(parts, axis=1) if len(parts) > 1 else parts[0]


def _inproj_kernel(x_ref, c_ref, mod_ref, g_ref, wa_ref, wb_ref, wc_ref, ba_ref, bb_ref, bc_ref,
                   cos_ref, sin_ref, qg_ref, kg_ref,
                   bd_ref, wuh_ref, wul_ref, bup_ref, o_ref, dec_ref):
    i = pl.program_id(0)
    is_ctx = i >= N_XT
    row = jnp.where(is_ctx, BATCH, i // TILES_PER_SEQ)
    x = jnp.where(is_ctx, c_ref[...], x_ref[...])
    ms = jnp.mean(x * x, axis=-1, keepdims=True)
    y = x * lax.rsqrt(ms + EPS) * g_ref[...]
    m = mod_ref[pl.ds(row, 1), :]
    shift = m[:, 0:D_MODEL]
    scale = m[:, D_MODEL:2 * D_MODEL]
    h = (y * (1.0 + scale) + shift).astype(BF16)

    def proj_from(w_ref, b_ref):
        def proj(a, n):
            return jnp.dot(h, w_ref[:, a:a + n], preferred_element_type=F32) + b_ref[:, a:a + n]
        return proj

    proj_a = proj_from(wa_ref, ba_ref)
    proj_b = proj_from(wb_ref, bb_ref)
    proj_c = proj_from(wc_ref, bc_ref)

    def put(a, val):
        o_ref[:, a:a + val.shape[1]] = val.astype(BF16)

    lr = proj_c(0, LANES)
    aq_raw = proj_b(B_AQ, ATT_Q)
    akv = proj_b(B_AKV, 2 * ATT_KV)
    put(P_SIG, jax.nn.sigmoid(proj_b(B_M, 1024)))

    lr_hi = lr.astype(BF16)
    lr_lo = (lr - lr_hi.astype(F32)).astype(BF16)
    wuh = wuh_ref[...]
    z = (jnp.dot(lr_hi, wuh, preferred_element_type=F32)
         + jnp.dot(lr_lo, wuh, preferred_element_type=F32)
         + jnp.dot(lr_hi, wul_ref[...], preferred_element_type=F32)) + bup_ref[...]
    put(P_GV, proj_a(A_GV, 512))
    put(P_SIG + 1024, jax.nn.sigmoid(proj_b(B_M + 1024, 1024)))

    cosv = jnp.where(is_ctx, 1.0, cos_ref[...])
    sinv = jnp.where(is_ctx, 0.0, sin_ref[...])
    bd = bd_ref[...]
    aq = _head_norm(aq_raw, qg_ref[...], bd)
    put(P_AQ, _rope(aq, cosv, sinv) * Q_PRESCALE)
    ak = _head_norm(akv[:, :ATT_KV], kg_ref[...], bd[:ATT_KV, :ATT_KV])
    put(P_AKV, _rope(ak, cosv[:, :ATT_KV], sinv[:, :ATT_KV]))
    put(P_AKV + ATT_KV, akv[:, ATT_KV:])
    put(P_U, proj_a(A_VAL, 512) * jax.nn.sigmoid(proj_a(A_GLU, 512)))
    put(P_SIG + 2048, jax.nn.sigmoid(proj_b(B_M + 2048, 1024)))

    la = (jnp.minimum(z, 0.0) - jnp.log(1.0 + jnp.exp(-jnp.abs(z)))) * (1.0 / GLA_GATE_NORM)
    rowc = lax.broadcasted_iota(jnp.int32, (TM, GLA_QK), 0) & (GLA_CHUNK - 1)
    cf = la[:, :GLA_QK]
    cb = la[:, GLA_QK:]
    sh = 1
    while sh < GLA_CHUNK:
        cf = cf + jnp.where(rowc >= sh, pltpu.roll(cf, sh, axis=0), 0.0)
        cb = cb + jnp.where(rowc < GLA_CHUNK - sh, pltpu.roll(cb, TM - sh, axis=0), 0.0)
        sh *= 2
    n_ch = TM // GLA_CHUNK
    last_f = [cf[c * GLA_CHUNK + GLA_CHUNK - 1:(c + 1) * GLA_CHUNK, :] for c in range(n_ch)]
    last_b = [cb[c * GLA_CHUNK:c * GLA_CHUNK + 1, :] for c in range(n_ch)]
    dec_ref[0] = jnp.concatenate(last_f, axis=0)
    dec_ref[1] = jnp.concatenate(last_b, axis=0)
    qk = proj_a(A_GQK, 512)
    gq = qk[:, :GLA_QK] * (GLA_DK ** -0.5)
    gk = qk[:, GLA_QK:]
    for dd, (cum, last) in enumerate(((cf, last_f), (cb, last_b))):
        tot = jnp.concatenate([jnp.broadcast_to(t, (GLA_CHUNK, GLA_QK)) for t in last], axis=0)
        base = P_GD + dd * GD_W
        put(base, gq * jnp.exp(cum))
        put(base + GLA_QK, gk * jnp.exp(-cum))
        put(base + 2 * GLA_QK, gk * jnp.exp(tot - cum))

    put(P_CG, _silu(proj_a(A_CG, 512)))
    put(P_GG, _silu(proj_a(A_GG, 512)))
    put(P_AG, _silu(proj_b(B_AG, ATT_Q)))


def _stream_specs(ctx_off):
    return [pl.BlockSpec((TM, D_MODEL), lambda i: (jnp.minimum(i, N_XT - 1), 0)),
            pl.BlockSpec((TM, D_MODEL), lambda i: (jnp.maximum(i - N_XT, 0) + ctx_off, 0))]


def _inproj(l, xs, cs, ctx_off, mod, norm_g, w_abc, b_abc, cos_t, sin_t, qg, kg, bd, wuh, wul, bup):
    const = lambda i: (0, 0)
    pos = lambda i: (jnp.where(i >= N_XT, 0, i % TILES_PER_SEQ), 0)
    return pl.pallas_call(
        _inproj_kernel,
        grid=(M_ALL // TM,),
        in_specs=_stream_specs(ctx_off) + [
                  pl.BlockSpec((None, 16, 3 * D_MODEL), lambda i: (l, 0, 0)),
                  pl.BlockSpec((None, 1, D_MODEL), lambda i: (l, 0, 0)),
                  ] + [pl.BlockSpec((None, D_MODEL, n), lambda i: (l, 0, 0),
                                    pipeline_mode=pl.Buffered(1)) for n in (NA, NB, LANES)
                  ] + [pl.BlockSpec((None, 1, b.shape[-1]), lambda i: (l, 0, 0)) for b in b_abc
                  ] + [
                  pl.BlockSpec((TM, ATT_Q), pos),
                  pl.BlockSpec((TM, ATT_Q), pos),
                  pl.BlockSpec((None, 1, ATT_Q), lambda i: (l, 0, 0)),
                  pl.BlockSpec((None, 1, ATT_KV), lambda i: (l, 0, 0)),
                  pl.BlockSpec((ATT_Q, ATT_Q), const),
                  pl.BlockSpec((None, LANES, 2 * GLA_QK), lambda i: (l, 0, 0)),
                  pl.BlockSpec((None, LANES, 2 * GLA_QK), lambda i: (l, 0, 0)),
                  pl.BlockSpec((None, 1, 2 * GLA_QK), lambda i: (l, 0, 0))],
        out_specs=[pl.BlockSpec((TM, NP), lambda i: (i, 0)),
                   pl.BlockSpec((2, TM // GLA_CHUNK, GLA_QK), lambda i: (0, i, 0))],
        out_shape=[jax.ShapeDtypeStruct((M_ALL, NP), BF16),
                   jax.ShapeDtypeStruct((2, M_ALL // GLA_CHUNK, GLA_QK), F32)],
        compiler_params=_cparams(1),
        name="inproj",
    )(xs, cs, mod, norm_g, *w_abc, *b_abc, cos_t, sin_t, qg, kg, bd, wuh, wul, bup)


def _conv_kernel(u_ref, ul_ref, ur_ref, sg_ref, dw_ref, dwb_ref, lng_ref, lnb_ref, sh_ref,
                 o_ref, win_ref, ext_ref):
    i = pl.program_id(0)
    j = i % BLK_PER_SEQ
    is_x = i < XB
    left_ok = jnp.logical_and(is_x, j != 0)
    right_ok = jnp.logical_and(is_x, j != BLK_PER_SEQ - 1)
    zero_h = jnp.zeros((HALO, D_CONV), BF16)
    win_ref[0:HALO, :] = jnp.where(left_ok, ul_ref[...], zero_h)
    win_ref[HALO:HALO + TC, :] = u_ref[...]
    win_ref[HALO + TC:, :] = jnp.where(right_ok, ur_ref[...], zero_h)
    ext_ref[0] = win_ref[...].astype(F32)
    half = TC // 2
    for r in range(1, SUBLANES):
        for a in range(2):
            ext_ref[r, a * half:a * half + CONV_SH, :] = jnp.dot(
                sh_ref[r], win_ref[a * half:a * half + CONV_SH, :], preferred_element_type=F32)
    bias = dwb_ref[...]
    lng = lng_ref[...]
    lnb = lnb_ref[...]
    for c in range(TC // CONV_RC):
        r0 = c * CONV_RC
        acc = jnp.broadcast_to(bias, (CONV_RC, D_CONV))
        for t in range(CONV_WIDTH):
            off = t + HALO - CONV_PAD
            a0 = r0 + (off // SUBLANES) * SUBLANES
            w_t = jnp.concatenate([dw_ref[t]] * (CONV_RC // SUBLANES), axis=0)
            acc = acc + ext_ref[off % SUBLANES, a0:a0 + CONV_RC, :] * w_t
        mu = jnp.mean(acc, axis=-1, keepdims=True)
        d = acc - mu
        var = jnp.mean(d * d, axis=-1, keepdims=True)
        yn = d * lax.rsqrt(var + EPS) * lng + lnb
        o_ref[r0:r0 + CONV_RC, :] = (_silu(yn) * sg_ref[r0:r0 + CONV_RC, :].astype(F32)).astype(BF16)


def _conv(l, p, dw_w, dw_b, ln_g, ln_b, shifts):
    nhb = M_ALL // HALO
    per = TC // HALO
    cu = P_U // D_CONV
    cg = P_CG // D_CONV
    vec = lambda i: (l, 0, 0)
    return pl.pallas_call(
        _conv_kernel,
        grid=(M_ALL // TC,),
        in_specs=[pl.BlockSpec((TC, D_CONV), lambda i: (i, cu)),
                  pl.BlockSpec((HALO, D_CONV), lambda i: (jnp.maximum(i * per - 1, 0), cu)),
                  pl.BlockSpec((HALO, D_CONV), lambda i: (jnp.minimum((i + 1) * per, nhb - 1), cu)),
                  pl.BlockSpec((TC, D_CONV), lambda i: (i, cg)),
                  pl.BlockSpec((None, CONV_WIDTH, SUBLANES, D_CONV), lambda i: (l, 0, 0, 0)),
                  pl.BlockSpec((None, 1, D_CONV), vec),
                  pl.BlockSpec((None, 1, D_CONV), vec),
                  pl.BlockSpec((None, 1, D_CONV), vec),
                  pl.BlockSpec((SUBLANES, CONV_SH, CONV_SH), lambda i: (0, 0, 0))],
        out_specs=pl.BlockSpec((TC, D_CONV), lambda i: (i, 0)),
        out_shape=jax.ShapeDtypeStruct((M_ALL, D_CONV), BF16),
        scratch_shapes=[pltpu.VMEM((TC + 2 * HALO, D_CONV), BF16),
                        pltpu.VMEM((SUBLANES, TC + 2 * HALO, D_CONV), F32)],
        compiler_params=_cparams(1),
        name="conv",
    )(p, p, p, p, dw_w, dw_b, ln_g, ln_b, shifts)


def _gla_kernel(qf_ref, qb_ref, vf_ref, vb_ref, df_ref, db_ref, tri_ref, of_ref, ob_ref, st_ref):
    step = pl.program_id(1)

    @pl.when(step == 0)
    def _():
        st_ref[...] = jnp.zeros_like(st_ref)

    lane_h = lax.broadcasted_iota(jnp.int32, (GLA_CHUNK, GLA_QK), 1) // GLA_DK
    srow_h = lax.broadcasted_iota(jnp.int32, (GLA_V, GLA_QK), 0) // GLA_DV
    scol_h = lax.broadcasted_iota(jnp.int32, (GLA_V, GLA_QK), 1) // GLA_DK
    smask = srow_h == scol_h
    nt = (((1,), (1,)), ((), ()))
    n_chunks = TC // GLA_CHUNK
    dirs = ((qf_ref, vf_ref, df_ref, of_ref), (qb_ref, vb_ref, db_ref, ob_ref))
    work = []
    for n in range(n_chunks):
        for d, (qkk_ref, v_ref, dec_ref, o_ref) in enumerate(dirs):
            c = n if d == 0 else n_chunks - 1 - n
            rows = slice(c * GLA_CHUNK, (c + 1) * GLA_CHUNK)
            q_in = qkk_ref[rows, 0:GLA_QK]
            k_in = qkk_ref[rows, GLA_QK:2 * GLA_QK]
            k_st = qkk_ref[rows, 2 * GLA_QK:3 * GLA_QK]
            v = v_ref[rows, :]
            decay = jnp.exp(dec_ref[c:c + 1, :])
            tri4 = jnp.concatenate([tri_ref[d]] * GLA_HEADS, axis=0) > 0.5
            q_stack = jnp.concatenate(
                [jnp.where(lane_h == h, q_in, jnp.zeros_like(q_in)) for h in range(GLA_HEADS)],
                axis=0)
            att = lax.dot_general(q_stack, k_in, nt, preferred_element_type=F32)
            att = jnp.where(tri4, att, 0.0).astype(BF16)
            kvt = lax.dot_general(v, k_st, (((0,), (0,)), ((), ())),
                                  preferred_element_type=F32)
            work.append((d, o_ref, rows, q_in, v, decay, att, jnp.where(smask, kvt, 0.0)))
    for d, o_ref, rows, q_in, v, decay, att, kvt in work:
        st_old = st_ref[d]
        o_inter = lax.dot_general(q_in, st_old.astype(BF16), nt, preferred_element_type=F32)
        o_intra = jnp.concatenate(
            [jnp.dot(att[h * GLA_CHUNK:(h + 1) * GLA_CHUNK, :],
                     v[:, h * GLA_DV:(h + 1) * GLA_DV], preferred_element_type=F32)
             for h in range(GLA_HEADS)], axis=1)
        o_ref[rows, :] = (o_intra + o_inter).astype(BF16)
        st_ref[d] = st_old * decay + kvt


def _gla(p, dec, tri):
    def rbf(b, s):
        return jnp.where(s == 0, XB + b, b * BLK_PER_SEQ + s - 1)

    def rbb(b, s):
        return jnp.where(s == 0, XB + b, b * BLK_PER_SEQ + BLK_PER_SEQ - s)

    cgd = P_GD // GD_W
    cv = P_GV // GLA_V
    n_ch = TC // GLA_CHUNK
    dec4 = dec.reshape(2, M_ALL // TC, n_ch, GLA_QK)
    out = jax.ShapeDtypeStruct((M_ALL, GLA_V), BF16)
    return pl.pallas_call(
        _gla_kernel,
        grid=(BATCH, BLK_PER_SEQ + 1),
        in_specs=[pl.BlockSpec((TC, GD_W), lambda b, s: (rbf(b, s), cgd)),
                  pl.BlockSpec((TC, GD_W), lambda b, s: (rbb(b, s), cgd + 1)),
                  pl.BlockSpec((TC, GLA_V), lambda b, s: (rbf(b, s), cv)),
                  pl.BlockSpec((TC, GLA_V), lambda b, s: (rbb(b, s), cv)),
                  pl.BlockSpec((None, None, n_ch, GLA_QK), lambda b, s: (0, rbf(b, s), 0, 0)),
                  pl.BlockSpec((None, None, n_ch, GLA_QK), lambda b, s: (1, rbb(b, s), 0, 0)),
                  pl.BlockSpec((2, GLA_CHUNK, GLA_CHUNK), lambda b, s: (0, 0, 0))],
        out_specs=[pl.BlockSpec((TC, GLA_V), lambda b, s: (rbf(b, s), 0)),
                   pl.BlockSpec((TC, GLA_V), lambda b, s: (rbb(b, s), 0))],
        out_shape=[out, out],
        scratch_shapes=[pltpu.VMEM((2, GLA_V, GLA_QK), F32)],
        compiler_params=_cparams(2),
        name="gla",
    )(p, p, p, p, dec4, dec4, tri)


NKEYS = CTX_LEN + SEQ
VROWS = 80
NT_DIMS = (((1,), (1,)), ((), ()))


def _make_attn_kernel(n_tiles, with_x):
    n_xb = SEQ // KB

    def kernel(*refs):
        if with_x:
            (q_ref, g_ref, ckv_ref, xkv_ref, selk_ref, selvt_ref, eye_ref, o_ref,
             kz_ref, vtc_ref, vtx_ref, qt_ref, m_ref, al_ref, acc_ref, s_ref) = refs
            key_srcs = ((ckv_ref, 0, CTX_LEN), (xkv_ref, CTX_LEN, SEQ))
        else:
            (q_ref, g_ref, ckv_ref, selk_ref, selvt_ref, eye_ref, _, o_ref,
             kz_ref, vtc_ref, qt_ref, m_ref, al_ref, acc_ref, s_ref) = refs
            key_srcs = ((ckv_ref, 0, CTX_LEN),)
        row_v = lax.broadcasted_iota(jnp.int32, (VROWS, 1), 0)

        def ext_values(svt, blk):
            vt = lax.dot_general(svt, blk, NT_DIMS, preferred_element_type=F32)
            return jnp.where(row_v == ATT_HD, 1.0, vt).astype(BF16)

        def prepare():
            for h in range(ATT_KV_HEADS):
                for e in range(2):
                    sk = selk_ref[h, e]
                    for (src, r0, n) in key_srcs:
                        for r in range(0, n, 1024):
                            nr = min(1024, n - r)
                            kz_ref[h, e, r0 + r:r0 + r + nr, :] = jnp.dot(
                                src[r:r + nr, :], sk, preferred_element_type=F32).astype(BF16)
                svt = selvt_ref[h]
                vtc_ref[h] = ext_values(svt, ckv_ref[...])
                if with_x:
                    for t in range(n_xb):
                        vtx_ref[h, t] = ext_values(svt, xkv_ref[t * KB:(t + 1) * KB, :])

        if with_x:
            pl.when(pl.program_id(1) == 0)(prepare)
        else:
            prepare()

        eye = eye_ref[...]
        for tile in range(n_tiles):
            for pr in range(ATT_HEADS // 2):
                qt_ref[tile, pr] = lax.dot_general(
                    eye, q_ref[tile * TC:(tile + 1) * TC, pr * LANES:(pr + 1) * LANES], NT_DIMS,
                    preferred_element_type=F32).astype(BF16)
        m_ref[...] = jnp.full_like(m_ref, -jnp.inf)
        acc_ref[...] = jnp.zeros_like(acc_ref)

        def scores(nxt, hd):
            tile, slot, k0, nk, _ = nxt
            h, pr, e = hd // ATT_GROUP, hd // 2, hd % 2
            s_ref[slot, hd, 0:nk, :] = jnp.dot(kz_ref[h, e, pl.ds(k0, nk), :], qt_ref[tile, pr],
                                               preferred_element_type=F32)

        def stage(cur, nxt):
            if cur is not None:
                tile, slot, _, nk, vt_of = cur
                for hd in range(ATT_HEADS):
                    s3 = s_ref[slot, hd, 0:nk, :].reshape(nk // SUBLANES, SUBLANES, TC)
                    m_col = jnp.max(jnp.max(s3, axis=0), axis=0, keepdims=True)
                    m_prev = m_ref[tile, hd]
                    m_next = jnp.maximum(m_prev, m_col)
                    al_ref[tile, hd] = jnp.exp2(m_prev - m_next)
                    m_ref[tile, hd] = m_next
            for hd in range(ATT_HEADS):
                if nxt is not None:
                    scores(nxt, hd)
                if cur is not None:
                    s3 = s_ref[slot, hd, 0:nk, :].reshape(nk // SUBLANES, SUBLANES, TC)
                    p = jnp.exp2(s3 - m_ref[tile, hd][None]).reshape(nk, TC).astype(BF16)
                    pv = jnp.dot(vt_of(hd // ATT_GROUP), p, preferred_element_type=F32)
                    acc3 = (acc_ref[tile, hd].reshape(VROWS // SUBLANES, SUBLANES, TC)
                            * al_ref[tile, hd][None])
                    acc_ref[tile, hd] = acc3.reshape(VROWS, TC) + pv

        def finish(tile):
            outs = []
            for hd in range(ATT_HEADS):
                a = acc_ref[tile, hd]
                outs.append(a[0:ATT_HD, :] / a[ATT_HD:ATT_HD + 1, :])
            o_nat = jnp.concatenate(outs, axis=0).T
            rows = slice(tile * TC, (tile + 1) * TC)
            o_ref[rows, :] = (o_nat * g_ref[rows, :].astype(F32)).astype(BF16)

        def ctx_stage(tile, slot):
            return (tile, slot, 0, CTX_LEN, lambda h: vtc_ref[h])

        def x_stage(tile, t, slot):
            k0 = CTX_LEN + t * KB
            if not isinstance(t, int):
                k0 = pl.multiple_of(k0, CTX_LEN)
            return (tile, slot, k0, KB, lambda h: vtx_ref[h, t])

        if not with_x:
            stage(None, ctx_stage(0, 0))
            stage(ctx_stage(0, 0), None)
            finish(0)
            return
        stage(None, ctx_stage(0, 1))
        for tile in range(n_tiles):
            s0 = tile % 2
            stage(ctx_stage(tile, 1 - s0), x_stage(tile, 0, s0))

            def body(j, carry, tile=tile, s0=s0):
                t = 2 * j
                stage(x_stage(tile, t, s0), x_stage(tile, t + 1, 1 - s0))
                stage(x_stage(tile, t + 1, 1 - s0), x_stage(tile, t + 2, s0))
                return carry
            lax.fori_loop(0, (n_xb - 2) // 2, body, 0)
            stage(x_stage(tile, n_xb - 2, s0), x_stage(tile, n_xb - 1, 1 - s0))
            nxt = ctx_stage(tile + 1, s0) if tile + 1 < n_tiles else None
            stage(x_stage(tile, n_xb - 1, 1 - s0), nxt)
            finish(tile)

    return kernel


ATT_XT = 2


def _attn(p, selk, selvt, eye):
    cq = P_AQ // ATT_Q
    cg = P_AG // ATT_Q
    ckv = P_AKV // (2 * ATT_KV)
    tq = ATT_XT * TC
    per_seq = SEQ // tq
    consts = [pl.BlockSpec((ATT_KV_HEADS, 2, 2 * ATT_KV, LANES), lambda *_: (0, 0, 0, 0)),
              pl.BlockSpec((ATT_KV_HEADS, VROWS, 2 * ATT_KV), lambda *_: (0, 0, 0)),
              pl.BlockSpec((LANES, LANES), lambda *_: (0, 0))]

    def scratch(n_tiles, n_keys):
        return [pltpu.VMEM((ATT_KV_HEADS, 2, n_keys, LANES), BF16),
                pltpu.VMEM((ATT_KV_HEADS, VROWS, CTX_LEN), BF16)
                ] + ([pltpu.VMEM((ATT_KV_HEADS, SEQ // KB, VROWS, KB), BF16)] if n_keys > CTX_LEN else []
                ) + [pltpu.VMEM((n_tiles, ATT_HEADS // 2, LANES, TC), BF16),
                     pltpu.VMEM((n_tiles, ATT_HEADS, SUBLANES, TC), F32),
                     pltpu.VMEM((n_tiles, ATT_HEADS, SUBLANES, TC), F32),
                     pltpu.VMEM((n_tiles, ATT_HEADS, VROWS, TC), F32),
                     pltpu.VMEM((2 if n_keys > CTX_LEN else 1, ATT_HEADS,
                                 KB if n_keys > CTX_LEN else CTX_LEN, TC), F32)]

    oc = pl.pallas_call(
        _make_attn_kernel(ATT_XT, True),
        grid=(BATCH, per_seq),
        in_specs=[pl.BlockSpec((tq, ATT_Q), lambda b, qi: (b * per_seq + qi, cq)),
                  pl.BlockSpec((tq, ATT_Q), lambda b, qi: (b * per_seq + qi, cg)),
                  pl.BlockSpec((CTX_LEN, 2 * ATT_KV), lambda b, qi: (XB + b, ckv)),
                  pl.BlockSpec((SEQ, 2 * ATT_KV), lambda b, qi: (b, ckv))] + consts,
        out_specs=pl.BlockSpec((tq, ATT_Q), lambda b, qi: (b * per_seq + qi, 0)),
        out_shape=jax.ShapeDtypeStruct((M_ALL, ATT_Q), BF16),
        scratch_shapes=scratch(ATT_XT, NKEYS),
        compiler_params=_cparams(2),
        name="attention",
    )(p, p, p, p, selk, selvt, eye)
    return pl.pallas_call(
        _make_attn_kernel(1, False),
        grid=(BATCH,),
        in_specs=[pl.BlockSpec((TC, ATT_Q), lambda b: (XB + b, cq)),
                  pl.BlockSpec((TC, ATT_Q), lambda b: (XB + b, cg)),
                  pl.BlockSpec((CTX_LEN, 2 * ATT_KV), lambda b: (XB + b, ckv))] + consts
                 + [pl.BlockSpec(memory_space=pl.ANY)],
        out_specs=pl.BlockSpec((TC, ATT_Q), lambda b: (XB + b, 0)),
        out_shape=jax.ShapeDtypeStruct((M_ALL, ATT_Q), BF16),
        input_output_aliases={6: 0},
        scratch_shapes=scratch(1, CTX_LEN),
        compiler_params=_cparams(1),
        name="attention_ctx",
    )(p, p, p, selk, selvt, eye, oc)


def _final_kernel(sx_ref, sc_ref, sig_ref, gg_ref, ua_ref, of_ref, ob_ref, oc_ref, mod_ref,
                  wc_ref, wg_ref, wa_ref, wo_ref, gn_ref, o_ref):
    i = pl.program_id(0)
    row = jnp.where(i >= N_XT, BATCH, i // TILES_PER_SEQ)
    ya = jnp.dot(ua_ref[...], wc_ref[...], preferred_element_type=F32)
    og = of_ref[...].astype(F32) + ob_ref[...].astype(F32)
    gn = gn_ref[...]
    parts = []
    for h in range(GLA_HEADS):
        oh = og[:, h * GLA_DV:(h + 1) * GLA_DV]
        ms = jnp.mean(oh * oh, axis=-1, keepdims=True)
        parts.append(oh * lax.rsqrt(ms + EPS) * gn)
    on = jnp.concatenate(parts, axis=1) * gg_ref[...].astype(F32)
    yb = jnp.dot(on.astype(BF16), wg_ref[...], preferred_element_type=F32)
    yc = jnp.dot(oc_ref[...], wa_ref[...], preferred_element_type=F32)
    merged = (sig_ref[:, 0:D_MODEL].astype(F32) * ya
              + sig_ref[:, D_MODEL:2 * D_MODEL].astype(F32) * yb
              + sig_ref[:, 2 * D_MODEL:3 * D_MODEL].astype(F32) * yc)
    out = jnp.dot(merged.astype(BF16), wo_ref[...], preferred_element_type=F32)
    gate = mod_ref[pl.ds(row, 1), 2 * D_MODEL:3 * D_MODEL]
    o_ref[...] = jnp.where(i >= N_XT, sc_ref[...], sx_ref[...]) + gate * out


def _final(l, n_tiles, xs, cs, ctx_off, p, ua, o_gla, oc, mod, wc, wg, wa, wo, gn):
    rowblk = lambda i: (i, 0)
    wspec = lambda k: pl.BlockSpec((None, k, D_MODEL), lambda i: (l, 0, 0))
    return pl.pallas_call(
        _final_kernel,
        grid=(n_tiles,),
        in_specs=_stream_specs(ctx_off) + [
                  pl.BlockSpec((TM, 3 * D_MODEL), lambda i: (i, P_SIG // (3 * D_MODEL))),
                  pl.BlockSpec((TM, GLA_V), lambda i: (i, P_GG // GLA_V)),
                  pl.BlockSpec((TM, D_CONV), rowblk),
                  pl.BlockSpec((TM, GLA_V), rowblk),
                  pl.BlockSpec((TM, GLA_V), rowblk),
                  pl.BlockSpec((TM, ATT_Q), rowblk),
                  pl.BlockSpec((None, 16, 3 * D_MODEL), lambda i: (l, 0, 0)),
                  wspec(D_CONV), wspec(GLA_V), wspec(ATT_Q), wspec(D_MODEL),
                  pl.BlockSpec((None, 1, GLA_DV), lambda i: (l, 0, 0))],
        out_specs=pl.BlockSpec((TM, D_MODEL), rowblk),
        out_shape=jax.ShapeDtypeStruct((n_tiles * TM, D_MODEL), F32),
        compiler_params=_cparams(1),
        name="merge_out",
    )(xs, cs, p, p, ua, o_gla[0], o_gla[1], oc, mod, wc, wg, wa, wo, gn)


def _rope_tables():
    t = np.arange(SEQ)
    row = (t // GRID_W).astype(np.float32)
    col = (t % GRID_W).astype(np.float32)
    n_freq = ROPE_AXIS_DIM // 2
    freqs = (np.float32(ROPE_THETA) ** (-np.arange(n_freq, dtype=np.float32) / n_freq)).astype(np.float32)
    ar = row[:, None] * freqs
    ac = col[:, None] * freqs
    cos64 = np.concatenate([np.cos(ar), np.cos(ar), np.cos(ac), np.cos(ac)], axis=1)
    sin64 = np.concatenate([-np.sin(ar), np.sin(ar), -np.sin(ac), np.sin(ac)], axis=1)
    return (np.tile(cos64, (1, ATT_HEADS)).astype(np.float32),
            np.tile(sin64, (1, ATT_HEADS)).astype(np.float32))


def _static_tables():
    bd = np.kron(np.eye(ATT_HEADS, dtype=np.float32), np.ones((ATT_HD, ATT_HD), np.float32))
    idx = np.arange(GLA_CHUNK)
    tri = np.stack([(idx[None, :] <= idx[:, None]), (idx[None, :] >= idx[:, None])]).astype(np.float32)
    selk = np.zeros((ATT_KV_HEADS, 2, 2 * ATT_KV, LANES), np.float32)
    selvt = np.zeros((ATT_KV_HEADS, VROWS, 2 * ATT_KV), np.float32)
    for h in range(ATT_KV_HEADS):
        for dd in range(ATT_HD):
            selvt[h, dd, ATT_KV + h * ATT_HD + dd] = 1.0
            for e in range(2):
                selk[h, e, h * ATT_HD + dd, e * ATT_HD + dd] = 1.0
    shifts = np.stack([np.eye(CONV_SH, k=r, dtype=np.float32) for r in range(SUBLANES)])
    return bd, tri, selk, selvt, shifts


def _split_cols(a):
    pad = jnp.zeros(a.shape[:-1] + (LANES - 2 * GLA_RANK,), a.dtype)
    return (a[..., :O_LR], a[..., O_AQ:], jnp.concatenate([a[..., O_LR:O_AQ], pad], axis=-1))


def kernel(x, c, ctx, c_ctx, norm_g, w_mod, b_mod, w_in, b_in, conv_dw_w, conv_dw_b, conv_ln_g,
           conv_ln_b, w_conv_out, gla_w_gate, gla_b_gate, gla_norm_g, w_gla_out, q_norm_g,
           k_norm_g, w_attn_out, w_out):
    cos_np, sin_np = _rope_tables()
    bd_np, tri_np, selk_np, selvt_np, shifts_np = _static_tables()
    shifts = jnp.asarray(shifts_np, BF16)
    cos_t, sin_t = jnp.asarray(cos_np), jnp.asarray(sin_np)
    bd = jnp.asarray(bd_np, BF16)
    tri = jnp.asarray(tri_np)
    selk = jnp.asarray(selk_np, BF16)
    selvt = jnp.asarray(selvt_np, BF16)
    eye = jnp.eye(LANES, dtype=BF16)

    w_bf = w_in.astype(BF16)
    w_abc = (w_bf,) + _split_cols(w_bf)[1:]
    b_abc = tuple(b.reshape(DEPTH, 1, b.shape[-1]) for b in _split_cols(b_in))
    cc = jnp.concatenate([c, c_ctx[None, :], jnp.zeros((16 - BATCH - 1, D_MODEL), F32)], axis=0)
    dw_w = jnp.broadcast_to(conv_dw_w[:, :, None, :], (DEPTH, CONV_WIDTH, SUBLANES, D_CONV))
    r3 = lambda a: a.reshape(DEPTH, 1, a.shape[-1])
    wup = jnp.zeros((DEPTH, LANES, 2 * GLA_QK), F32)
    wup = wup.at[:, 0:GLA_RANK, 0:GLA_QK].set(gla_w_gate[:, 0])
    wup = wup.at[:, GLA_RANK:2 * GLA_RANK, GLA_QK:].set(gla_w_gate[:, 1])
    wuh = wup.astype(BF16)
    wul = (wup - wuh.astype(F32)).astype(BF16)
    bup = gla_b_gate.reshape(DEPTH, 1, 2 * GLA_QK)
    qg = jnp.tile(q_norm_g, (1, ATT_HEADS)).reshape(DEPTH, 1, ATT_Q)
    kg = jnp.tile(k_norm_g, (1, ATT_KV_HEADS)).reshape(DEPTH, 1, ATT_KV)
    wc, wg, wa, wo = (w.astype(BF16) for w in (w_conv_out, w_gla_out, w_attn_out, w_out))

    mod = _modulation(cc, w_mod, b_mod)
    xs, cs, ctx_off = x.reshape(MX, D_MODEL), ctx.reshape(MC, D_MODEL), 0
    for l in range(DEPTH):
        p, dec = _inproj(l, xs, cs, ctx_off, mod, r3(norm_g), w_abc, b_abc, cos_t, sin_t, qg, kg, bd, wuh, wul, bup)
        ua = _conv(l, p, dw_w, r3(conv_dw_b), r3(conv_ln_g), r3(conv_ln_b), shifts)
        o_gla = _gla(p, dec, tri)
        oc = _attn(p, selk, selvt, eye)
        n_tiles = M_ALL // TM if l < DEPTH - 1 else N_XT
        xs = _final(l, n_tiles, xs, cs, ctx_off, p, ua, o_gla, oc, mod, wc, wg, wa, wo, r3(gla_norm_g))
        cs, ctx_off = xs, N_XT
    return xs.reshape(BATCH, SEQ, D_MODEL)
```

```python
import functools
import math

import numpy as np
import jax
import jax.numpy as jnp
from jax import lax
from jax.experimental import pallas as pl
from jax.experimental.pallas import tpu as pltpu

F32 = jnp.float32
BF16 = jnp.bfloat16
HIGHEST = lax.Precision.HIGHEST

D_MODEL = 1024
BATCH = 8
SEQ = 4096
DEPTH = 4
CTX_LEN = 256
GRID_W = 64
EPS = 1e-6
D_CONV = 512
CONV_WIDTH = 31
CONV_PAD = CONV_WIDTH // 2
GLA_HEADS = 4
GLA_DK = 64
GLA_DV = 128
GLA_QK = GLA_HEADS * GLA_DK
GLA_V = GLA_HEADS * GLA_DV
GLA_RANK = 16
GLA_GATE_NORM = 16.0
GLA_CHUNK = 64
ATT_HEADS = 8
ATT_KV_HEADS = 2
ATT_GROUP = ATT_HEADS // ATT_KV_HEADS
ATT_HD = 64
ATT_Q = ATT_HEADS * ATT_HD
ATT_KV = ATT_KV_HEADS * ATT_HD
ROPE_AXIS_DIM = ATT_HD // 2
ROPE_THETA = 10000.0

LANES = 128
SUBLANES = 8
MX = BATCH * SEQ
MC = BATCH * CTX_LEN
M_ALL = MX + MC

O_VAL, O_GLU, O_CGATE = 0, 512, 1024
O_GQ, O_GK, O_GV, O_GG = 1536, 1792, 2048, 2560
O_LR = 3072
O_AQ, O_AK, O_AV, O_AG = 3104, 3616, 3744, 3872
O_MA = 4384
N_IN = 7456

A_VAL, A_GLU, A_CG, A_GQK, A_GV, A_GG = 0, 512, 1024, 1536, 2048, 2560
NA = O_LR
B_AQ, B_AKV, B_AG, B_M = 0, 512, 768, 1280
NB = N_IN - O_AQ

P_SIG = 0
P_U = 3072
P_CG = 3584
P_GV = 4096
P_GD = 4608
GD_W = 3 * GLA_QK
P_GG = 6144
P_AQ = 6656
P_AG = 7168
P_AKV = 7680
NP = 7936

TM = 512
N_XT = MX // TM
N_CT = MC // TM
TILES_PER_SEQ = SEQ // TM
TC = 256
XB = MX // TC
BLK_PER_SEQ = SEQ // TC
HALO = 16
CONV_RC = 32
CONV_SH = TC // 2 + 2 * HALO
KB = 512
Q_PRESCALE = (ATT_HD ** -0.5) * math.log2(math.e)

VMEM_LIMIT = 56 * 1024 * 1024


def _cparams(n_axes, vmem=VMEM_LIMIT):
    return pltpu.CompilerParams(dimension_semantics=("arbitrary",) * n_axes,
                                vmem_limit_bytes=vmem)


def _silu(x):
    return x * jax.nn.sigmoid(x)


def _mod_kernel(c_ref, w_ref, b_ref, o_ref):
    s = _silu(c_ref[...])
    o_ref[...] = jnp.dot(s, w_ref[...], preferred_element_type=F32, precision=HIGHEST) + b_ref[...]


def _modulation(cc, w_mod, b_mod):
    nt = 3 * D_MODEL // 1024
    return pl.pallas_call(
        _mod_kernel,
        grid=(DEPTH, nt),
        in_specs=[pl.BlockSpec((16, D_MODEL), lambda l, n: (0, 0)),
                  pl.BlockSpec((None, D_MODEL, 1024), lambda l, n: (l, 0, n)),
                  pl.BlockSpec((None, 1, 1024), lambda l, n: (l, 0, n))],
        out_specs=pl.BlockSpec((None, 16, 1024), lambda l, n: (l, 0, n)),
        out_shape=jax.ShapeDtypeStruct((DEPTH, 16, 3 * D_MODEL), F32),
        compiler_params=_cparams(2),
        name="modulation",
    )(cc, w_mod, b_mod.reshape(DEPTH, 1, 3 * D_MODEL))


def _head_norm(xv, gain, bd):
    ss = jnp.dot((xv * xv).astype(BF16), bd, preferred_element_type=F32)
    return xv * lax.rsqrt(ss * (1.0 / ATT_HD) + EPS) * gain


def _rope(xv, cosv, sinv):
    parts = []
    for s in range(xv.shape[1] // LANES):
        sl = slice(s * LANES, (s + 1) * LANES)
        xs = xv[:, sl]
        up = pltpu.roll(xs, LANES - 16, axis=1)
        dn = pltpu.roll(xs, 16, axis=1)
        lane = lax.broadcasted_iota(jnp.int32, xs.shape, 1)
        partner = jnp.where((lane & 16) == 0, up, dn)
        parts.append(xs * cosv[:, sl] + partner * sinv[:, sl])
    return jnp.concatenate(parts, axis=1) if len(parts) > 1 else parts[0]


def _inproj_kernel(x_ref, c_ref, mod_ref, g_ref, wa_ref, wb_ref, wc_ref, ba_ref, bb_ref, bc_ref,
                   cos_ref, sin_ref, qg_ref, kg_ref,
                   bd_ref, wuh_ref, wul_ref, bup_ref, o_ref, dec_ref):
    i = pl.program_id(0)
    is_ctx = i >= N_XT
    row = jnp.where(is_ctx, BATCH, i // TILES_PER_SEQ)
    x = jnp.where(is_ctx, c_ref[...], x_ref[...])
    ms = jnp.mean(x * x, axis=-1, keepdims=True)
    y = x * lax.rsqrt(ms + EPS) * g_ref[...]
    m = mod_ref[pl.ds(row, 1), :]
    shift = m[:, 0:D_MODEL]
    scale = m[:, D_MODEL:2 * D_MODEL]
    h = (y * (1.0 + scale) + shift).astype(BF16)

    def proj_from(w_ref, b_ref):
        def proj(a, n):
            return jnp.dot(h, w_ref[:, a:a + n], preferred_element_type=F32) + b_ref[:, a:a + n]
        return proj

    proj_a = proj_from(wa_ref, ba_ref)
    proj_b = proj_from(wb_ref, bb_ref)
    proj_c = proj_from(wc_ref, bc_ref)

    def put(a, val):
        o_ref[:, a:a + val.shape[1]] = val.astype(BF16)

    lr = proj_c(0, LANES)
    aq_raw = proj_b(B_AQ, ATT_Q)
    akv = proj_b(B_AKV, 2 * ATT_KV)
    put(P_SIG, jax.nn.sigmoid(proj_b(B_M, 1024)))

    lr_hi = lr.astype(BF16)
    lr_lo = (lr - lr_hi.astype(F32)).astype(BF16)
    wuh = wuh_ref[...]
    z = (jnp.dot(lr_hi, wuh, preferred_element_type=F32)
         + jnp.dot(lr_lo, wuh, preferred_element_type=F32)
         + jnp.dot(lr_hi, wul_ref[...], preferred_element_type=F32)) + bup_ref[...]
    put(P_SIG + 1024, jax.nn.sigmoid(proj_b(B_M + 1024, 1024)))

    cosv = jnp.where(is_ctx, 1.0, cos_ref[...])
    sinv = jnp.where(is_ctx, 0.0, sin_ref[...])
    bd = bd_ref[...]
    aq = _head_norm(aq_raw, qg_ref[...], bd)
    put(P_AQ, _rope(aq, cosv, sinv) * Q_PRESCALE)
    ak = _head_norm(akv[:, :ATT_KV], kg_ref[...], bd[:ATT_KV, :ATT_KV])
    put(P_AKV, _rope(ak, cosv[:, :ATT_KV], sinv[:, :ATT_KV]))
    put(P_AKV + ATT_KV, akv[:, ATT_KV:])
    put(P_SIG + 2048, jax.nn.sigmoid(proj_b(B_M + 2048, 1024)))

    la = (jnp.minimum(z, 0.0) - jnp.log(1.0 + jnp.exp(-jnp.abs(z)))) * (1.0 / GLA_GATE_NORM)
    rowc = lax.broadcasted_iota(jnp.int32, (TM, GLA_QK), 0) & (GLA_CHUNK - 1)
    cf = la[:, :GLA_QK]
    cb = la[:, GLA_QK:]
    sh = 1
    while sh < GLA_CHUNK:
        cf = cf + jnp.where(rowc >= sh, pltpu.roll(cf, sh, axis=0), 0.0)
        cb = cb + jnp.where(rowc < GLA_CHUNK - sh, pltpu.roll(cb, TM - sh, axis=0), 0.0)
        sh *= 2
    n_ch = TM // GLA_CHUNK
    last_f = [cf[c * GLA_CHUNK + GLA_CHUNK - 1:(c + 1) * GLA_CHUNK, :] for c in range(n_ch)]
    last_b = [cb[c * GLA_CHUNK:c * GLA_CHUNK + 1, :] for c in range(n_ch)]
    dec_ref[0] = jnp.concatenate(last_f, axis=0)
    dec_ref[1] = jnp.concatenate(last_b, axis=0)
    qk = proj_a(A_GQK, 512)
    gq = qk[:, :GLA_QK] * (GLA_DK ** -0.5)
    gk = qk[:, GLA_QK:]
    for dd, (cum, last) in enumerate(((cf, last_f), (cb, last_b))):
        tot = jnp.concatenate([jnp.broadcast_to(t, (GLA_CHUNK, GLA_QK)) for t in last], axis=0)
        base = P_GD + dd * GD_W
        put(base, gq * jnp.exp(cum))
        put(base + GLA_QK, gk * jnp.exp(-cum))
        put(base + 2 * GLA_QK, gk * jnp.exp(tot - cum))

    put(P_U, proj_a(A_VAL, 512) * jax.nn.sigmoid(proj_a(A_GLU, 512)))
    put(P_CG, _silu(proj_a(A_CG, 512)))
    put(P_GG, _silu(proj_a(A_GG, 512)))
    put(P_AG, _silu(proj_b(B_AG, ATT_Q)))
    put(P_GV, proj_a(A_GV, 512))


def _stream_specs(ctx_off):
    return [pl.BlockSpec((TM, D_MODEL), lambda i: (jnp.minimum(i, N_XT - 1), 0)),
            pl.BlockSpec((TM, D_MODEL), lambda i: (jnp.maximum(i - N_XT, 0) + ctx_off, 0))]


def _inproj(l, xs, cs, ctx_off, mod, norm_g, w_abc, b_abc, cos_t, sin_t, qg, kg, bd, wuh, wul, bup):
    const = lambda i: (0, 0)
    pos = lambda i: (jnp.where(i >= N_XT, 0, i % TILES_PER_SEQ), 0)
    return pl.pallas_call(
        _inproj_kernel,
        grid=(M_ALL // TM,),
        in_specs=_stream_specs(ctx_off) + [
                  pl.BlockSpec((None, 16, 3 * D_MODEL), lambda i: (l, 0, 0)),
                  pl.BlockSpec((None, 1, D_MODEL), lambda i: (l, 0, 0)),
                  ] + [pl.BlockSpec((None, D_MODEL, n), lambda i: (l, 0, 0),
                                    pipeline_mode=pl.Buffered(1)) for n in (NA, NB, LANES)
                  ] + [pl.BlockSpec((None, 1, b.shape[-1]), lambda i: (l, 0, 0)) for b in b_abc
                  ] + [
                  pl.BlockSpec((TM, ATT_Q), pos),
                  pl.BlockSpec((TM, ATT_Q), pos),
                  pl.BlockSpec((None, 1, ATT_Q), lambda i: (l, 0, 0)),
                  pl.BlockSpec((None, 1, ATT_KV), lambda i: (l, 0, 0)),
                  pl.BlockSpec((ATT_Q, ATT_Q), const),
                  pl.BlockSpec((None, LANES, 2 * GLA_QK), lambda i: (l, 0, 0)),
                  pl.BlockSpec((None, LANES, 2 * GLA_QK), lambda i: (l, 0, 0)),
                  pl.BlockSpec((None, 1, 2 * GLA_QK), lambda i: (l, 0, 0))],
        out_specs=[pl.BlockSpec((TM, NP), lambda i: (i, 0)),
                   pl.BlockSpec((2, TM // GLA_CHUNK, GLA_QK), lambda i: (0, i, 0))],
        out_shape=[jax.ShapeDtypeStruct((M_ALL, NP), BF16),
                   jax.ShapeDtypeStruct((2, M_ALL // GLA_CHUNK, GLA_QK), F32)],
        compiler_params=_cparams(1),
        name="inproj",
    )(xs, cs, mod, norm_g, *w_abc, *b_abc, cos_t, sin_t, qg, kg, bd, wuh, wul, bup)


def _conv_kernel(u_ref, ul_ref, ur_ref, sg_ref, dw_ref, dwb_ref, lng_ref, lnb_ref, sh_ref,
                 o_ref, win_ref, ext_ref):
    i = pl.program_id(0)
    j = i % BLK_PER_SEQ
    is_x = i < XB
    left_ok = jnp.logical_and(is_x, j != 0)
    right_ok = jnp.logical_and(is_x, j != BLK_PER_SEQ - 1)
    zero_h = jnp.zeros((HALO, D_CONV), BF16)
    win_ref[0:HALO, :] = jnp.where(left_ok, ul_ref[...], zero_h)
    win_ref[HALO:HALO + TC, :] = u_ref[...]
    win_ref[HALO + TC:, :] = jnp.where(right_ok, ur_ref[...], zero_h)
    ext_ref[0] = win_ref[...].astype(F32)
    half = TC // 2
    for r in range(1, SUBLANES):
        for a in range(2):
            ext_ref[r, a * half:a * half + CONV_SH, :] = jnp.dot(
                sh_ref[r], win_ref[a * half:a * half + CONV_SH, :], preferred_element_type=F32)
    bias = dwb_ref[...]
    lng = lng_ref[...]
    lnb = lnb_ref[...]
    for c in range(TC // CONV_RC):
        r0 = c * CONV_RC
        acc = jnp.broadcast_to(bias, (CONV_RC, D_CONV))
        for t in range(CONV_WIDTH):
            off = t + HALO - CONV_PAD
            a0 = r0 + (off // SUBLANES) * SUBLANES
            w_t = jnp.concatenate([dw_ref[t]] * (CONV_RC // SUBLANES), axis=0)
            acc = acc + ext_ref[off % SUBLANES, a0:a0 + CONV_RC, :] * w_t
        mu = jnp.mean(acc, axis=-1, keepdims=True)
        d = acc - mu
        var = jnp.mean(d * d, axis=-1, keepdims=True)
        yn = d * lax.rsqrt(var + EPS) * lng + lnb
        o_ref[r0:r0 + CONV_RC, :] = (_silu(yn) * sg_ref[r0:r0 + CONV_RC, :].astype(F32)).astype(BF16)


def _conv(l, p, dw_w, dw_b, ln_g, ln_b, shifts):
    nhb = M_ALL // HALO
    per = TC // HALO
    cu = P_U // D_CONV
    cg = P_CG // D_CONV
    vec = lambda i: (l, 0, 0)
    return pl.pallas_call(
        _conv_kernel,
        grid=(M_ALL // TC,),
        in_specs=[pl.BlockSpec((TC, D_CONV), lambda i: (i, cu)),
                  pl.BlockSpec((HALO, D_CONV), lambda i: (jnp.maximum(i * per - 1, 0), cu)),
                  pl.BlockSpec((HALO, D_CONV), lambda i: (jnp.minimum((i + 1) * per, nhb - 1), cu)),
                  pl.BlockSpec((TC, D_CONV), lambda i: (i, cg)),
                  pl.BlockSpec((None, CONV_WIDTH, SUBLANES, D_CONV), lambda i: (l, 0, 0, 0)),
                  pl.BlockSpec((None, 1, D_CONV), vec),
                  pl.BlockSpec((None, 1, D_CONV), vec),
                  pl.BlockSpec((None, 1, D_CONV), vec),
                  pl.BlockSpec((SUBLANES, CONV_SH, CONV_SH), lambda i: (0, 0, 0))],
        out_specs=pl.BlockSpec((TC, D_CONV), lambda i: (i, 0)),
        out_shape=jax.ShapeDtypeStruct((M_ALL, D_CONV), BF16),
        scratch_shapes=[pltpu.VMEM((TC + 2 * HALO, D_CONV), BF16),
                        pltpu.VMEM((SUBLANES, TC + 2 * HALO, D_CONV), F32)],
        compiler_params=_cparams(1),
        name="conv",
    )(p, p, p, p, dw_w, dw_b, ln_g, ln_b, shifts)


def _gla_kernel(qf_ref, qb_ref, vf_ref, vb_ref, df_ref, db_ref, tri_ref, of_ref, ob_ref, st_ref):
    step = pl.program_id(1)

    @pl.when(step == 0)
    def _():
        st_ref[...] = jnp.zeros_like(st_ref)

    lane_h = lax.broadcasted_iota(jnp.int32, (GLA_CHUNK, GLA_QK), 1) // GLA_DK
    srow_h = lax.broadcasted_iota(jnp.int32, (GLA_V, GLA_QK), 0) // GLA_DV
    scol_h = lax.broadcasted_iota(jnp.int32, (GLA_V, GLA_QK), 1) // GLA_DK
    smask = srow_h == scol_h
    nt = (((1,), (1,)), ((), ()))
    n_chunks = TC // GLA_CHUNK
    dirs = ((qf_ref, vf_ref, df_ref, of_ref), (qb_ref, vb_ref, db_ref, ob_ref))
    work = []
    for n in range(n_chunks):
        for d, (qkk_ref, v_ref, dec_ref, o_ref) in enumerate(dirs):
            c = n if d == 0 else n_chunks - 1 - n
            rows = slice(c * GLA_CHUNK, (c + 1) * GLA_CHUNK)
            q_in = qkk_ref[rows, 0:GLA_QK]
            k_in = qkk_ref[rows, GLA_QK:2 * GLA_QK]
            k_st = qkk_ref[rows, 2 * GLA_QK:3 * GLA_QK]
            v = v_ref[rows, :]
            decay = jnp.exp(dec_ref[c:c + 1, :])
            tri4 = jnp.concatenate([tri_ref[d]] * GLA_HEADS, axis=0) > 0.5
            q_stack = jnp.concatenate(
                [jnp.where(lane_h == h, q_in, jnp.zeros_like(q_in)) for h in range(GLA_HEADS)],
                axis=0)
            att = lax.dot_general(q_stack, k_in, nt, preferred_element_type=F32)
            att = jnp.where(tri4, att, 0.0).astype(BF16)
            kvt = lax.dot_general(v, k_st, (((0,), (0,)), ((), ())),
                                  preferred_element_type=F32)
            work.append((d, o_ref, rows, q_in, v, decay, att, jnp.where(smask, kvt, 0.0)))
    for d, o_ref, rows, q_in, v, decay, att, kvt in work:
        st_old = st_ref[d]
        o_inter = lax.dot_general(q_in, st_old.astype(BF16), nt, preferred_element_type=F32)
        o_intra = jnp.concatenate(
            [jnp.dot(att[h * GLA_CHUNK:(h + 1) * GLA_CHUNK, :],
                     v[:, h * GLA_DV:(h + 1) * GLA_DV], preferred_element_type=F32)
             for h in range(GLA_HEADS)], axis=1)
        o_ref[rows, :] = (o_intra + o_inter).astype(BF16)
        st_ref[d] = st_old * decay + kvt


def _gla(p, dec, tri):
    def rbf(b, s):
        return jnp.where(s == 0, XB + b, b * BLK_PER_SEQ + s - 1)

    def rbb(b, s):
        return jnp.where(s == 0, XB + b, b * BLK_PER_SEQ + BLK_PER_SEQ - s)

    cgd = P_GD // GD_W
    cv = P_GV // GLA_V
    n_ch = TC // GLA_CHUNK
    dec4 = dec.reshape(2, M_ALL // TC, n_ch, GLA_QK)
    out = jax.ShapeDtypeStruct((M_ALL, GLA_V), BF16)
    return pl.pallas_call(
        _gla_kernel,
        grid=(BATCH, BLK_PER_SEQ + 1),
        in_specs=[pl.BlockSpec((TC, GD_W), lambda b, s: (rbf(b, s), cgd)),
                  pl.BlockSpec((TC, GD_W), lambda b, s: (rbb(b, s), cgd + 1)),
                  pl.BlockSpec((TC, GLA_V), lambda b, s: (rbf(b, s), cv)),
                  pl.BlockSpec((TC, GLA_V), lambda b, s: (rbb(b, s), cv)),
                  pl.BlockSpec((None, None, n_ch, GLA_QK), lambda b, s: (0, rbf(b, s), 0, 0)),
                  pl.BlockSpec((None, None, n_ch, GLA_QK), lambda b, s: (1, rbb(b, s), 0, 0)),
                  pl.BlockSpec((2, GLA_CHUNK, GLA_CHUNK), lambda b, s: (0, 0, 0))],
        out_specs=[pl.BlockSpec((TC, GLA_V), lambda b, s: (rbf(b, s), 0)),
                   pl.BlockSpec((TC, GLA_V), lambda b, s: (rbb(b, s), 0))],
        out_shape=[out, out],
        scratch_shapes=[pltpu.VMEM((2, GLA_V, GLA_QK), F32)],
        compiler_params=_cparams(2),
        name="gla",
    )(p, p, p, p, dec4, dec4, tri)


NKEYS = CTX_LEN + SEQ
VROWS = 80
NT_DIMS = (((1,), (1,)), ((), ()))


def _make_attn_kernel(n_tiles, with_x):
    n_xb = SEQ // KB

    def kernel(*refs):
        if with_x:
            (q_ref, g_ref, ckv_ref, xkv_ref, selk_ref, selvt_ref, eye_ref, o_ref,
             kz_ref, vtc_ref, vtx_ref, qt_ref, m_ref, al_ref, acc_ref, s_ref) = refs
            key_srcs = ((ckv_ref, 0, CTX_LEN), (xkv_ref, CTX_LEN, SEQ))
        else:
            (q_ref, g_ref, ckv_ref, selk_ref, selvt_ref, eye_ref, _, o_ref,
             kz_ref, vtc_ref, qt_ref, m_ref, al_ref, acc_ref, s_ref) = refs
            key_srcs = ((ckv_ref, 0, CTX_LEN),)
        row_v = lax.broadcasted_iota(jnp.int32, (VROWS, 1), 0)

        def ext_values(svt, blk):
            vt = lax.dot_general(svt, blk, NT_DIMS, preferred_element_type=F32)
            return jnp.where(row_v == ATT_HD, 1.0, vt).astype(BF16)

        def prepare():
            for h in range(ATT_KV_HEADS):
                for e in range(2):
                    sk = selk_ref[h, e]
                    for (src, r0, n) in key_srcs:
                        for r in range(0, n, 1024):
                            nr = min(1024, n - r)
                            kz_ref[h, e, r0 + r:r0 + r + nr, :] = jnp.dot(
                                src[r:r + nr, :], sk, preferred_element_type=F32).astype(BF16)
                svt = selvt_ref[h]
                vtc_ref[h] = ext_values(svt, ckv_ref[...])
                if with_x:
                    for t in range(n_xb):
                        vtx_ref[h, t] = ext_values(svt, xkv_ref[t * KB:(t + 1) * KB, :])

        if with_x:
            pl.when(pl.program_id(1) == 0)(prepare)
        else:
            prepare()

        eye = eye_ref[...]
        for tile in range(n_tiles):
            for pr in range(ATT_HEADS // 2):
                qt_ref[tile, pr] = lax.dot_general(
                    eye, q_ref[tile * TC:(tile + 1) * TC, pr * LANES:(pr + 1) * LANES], NT_DIMS,
                    preferred_element_type=F32).astype(BF16)
        m_ref[...] = jnp.full_like(m_ref, -jnp.inf)
        acc_ref[...] = jnp.zeros_like(acc_ref)

        def scores(nxt, hd):
            tile, slot, k0, nk, _ = nxt
            h, pr, e = hd // ATT_GROUP, hd // 2, hd % 2
            s_ref[slot, hd, 0:nk, :] = jnp.dot(kz_ref[h, e, pl.ds(k0, nk), :], qt_ref[tile, pr],
                                               preferred_element_type=F32)

        def stage(cur, nxt):
            if cur is not None:
                tile, slot, _, nk, vt_of = cur
                for hd in range(ATT_HEADS):
                    s3 = s_ref[slot, hd, 0:nk, :].reshape(nk // SUBLANES, SUBLANES, TC)
                    m_col = jnp.max(jnp.max(s3, axis=0), axis=0, keepdims=True)
                    m_prev = m_ref[tile, hd]
                    m_next = jnp.maximum(m_prev, m_col)
                    al_ref[tile, hd] = jnp.exp2(m_prev - m_next)
                    m_ref[tile, hd] = m_next
            for hd in range(ATT_HEADS):
                if nxt is not None:
                    scores(nxt, hd)
                if cur is not None:
                    s3 = s_ref[slot, hd, 0:nk, :].reshape(nk // SUBLANES, SUBLANES, TC)
                    p = jnp.exp2(s3 - m_ref[tile, hd][None]).reshape(nk, TC).astype(BF16)
                    pv = jnp.dot(vt_of(hd // ATT_GROUP), p, preferred_element_type=F32)
                    acc3 = (acc_ref[tile, hd].reshape(VROWS // SUBLANES, SUBLANES, TC)
                            * al_ref[tile, hd][None])
                    acc_ref[tile, hd] = acc3.reshape(VROWS, TC) + pv

        def finish(tile):
            outs = []
            for hd in range(ATT_HEADS):
                a = acc_ref[tile, hd]
                outs.append(a[0:ATT_HD, :] / a[ATT_HD:ATT_HD + 1, :])
            o_nat = jnp.concatenate(outs, axis=0).T
            rows = slice(tile * TC, (tile + 1) * TC)
            o_ref[rows, :] = (o_nat * g_ref[rows, :].astype(F32)).astype(BF16)

        def ctx_stage(tile, slot):
            return (tile, slot, 0, CTX_LEN, lambda h: vtc_ref[h])

        def x_stage(tile, t, slot):
            k0 = CTX_LEN + t * KB
            if not isinstance(t, int):
                k0 = pl.multiple_of(k0, CTX_LEN)
            return (tile, slot, k0, KB, lambda h: vtx_ref[h, t])

        if not with_x:
            stage(None, ctx_stage(0, 0))
            stage(ctx_stage(0, 0), None)
            finish(0)
            return
        stage(None, ctx_stage(0, 1))
        for tile in range(n_tiles):
            s0 = tile % 2
            stage(ctx_stage(tile, 1 - s0), x_stage(tile, 0, s0))

            def body(j, carry, tile=tile, s0=s0):
                t = 2 * j
                stage(x_stage(tile, t, s0), x_stage(tile, t + 1, 1 - s0))
                stage(x_stage(tile, t + 1, 1 - s0), x_stage(tile, t + 2, s0))
                return carry
            lax.fori_loop(0, (n_xb - 2) // 2, body, 0)
            stage(x_stage(tile, n_xb - 2, s0), x_stage(tile, n_xb - 1, 1 - s0))
            nxt = ctx_stage(tile + 1, s0) if tile + 1 < n_tiles else None
            stage(x_stage(tile, n_xb - 1, 1 - s0), nxt)
            finish(tile)

    return kernel


ATT_XT = 2


def _attn(p, selk, selvt, eye):
    cq = P_AQ // ATT_Q
    cg = P_AG // ATT_Q
    ckv = P_AKV // (2 * ATT_KV)
    tq = ATT_XT * TC
    per_seq = SEQ // tq
    consts = [pl.BlockSpec((ATT_KV_HEADS, 2, 2 * ATT_KV, LANES), lambda *_: (0, 0, 0, 0)),
              pl.BlockSpec((ATT_KV_HEADS, VROWS, 2 * ATT_KV), lambda *_: (0, 0, 0)),
              pl.BlockSpec((LANES, LANES), lambda *_: (0, 0))]

    def scratch(n_tiles, n_keys):
        return [pltpu.VMEM((ATT_KV_HEADS, 2, n_keys, LANES), BF16),
                pltpu.VMEM((ATT_KV_HEADS, VROWS, CTX_LEN), BF16)
                ] + ([pltpu.VMEM((ATT_KV_HEADS, SEQ // KB, VROWS, KB), BF16)] if n_keys > CTX_LEN else []
                ) + [pltpu.VMEM((n_tiles, ATT_HEADS // 2, LANES, TC), BF16),
                     pltpu.VMEM((n_tiles, ATT_HEADS, SUBLANES, TC), F32),
                     pltpu.VMEM((n_tiles, ATT_HEADS, SUBLANES, TC), F32),
                     pltpu.VMEM((n_tiles, ATT_HEADS, VROWS, TC), F32),
                     pltpu.VMEM((2 if n_keys > CTX_LEN else 1, ATT_HEADS,
                                 KB if n_keys > CTX_LEN else CTX_LEN, TC), F32)]

    oc = pl.pallas_call(
        _make_attn_kernel(ATT_XT, True),
        grid=(BATCH, per_seq),
        in_specs=[pl.BlockSpec((tq, ATT_Q), lambda b, qi: (b * per_seq + qi, cq)),
                  pl.BlockSpec((tq, ATT_Q), lambda b, qi: (b * per_seq + qi, cg)),
                  pl.BlockSpec((CTX_LEN, 2 * ATT_KV), lambda b, qi: (XB + b, ckv)),
                  pl.BlockSpec((SEQ, 2 * ATT_KV), lambda b, qi: (b, ckv))] + consts,
        out_specs=pl.BlockSpec((tq, ATT_Q), lambda b, qi: (b * per_seq + qi, 0)),
        out_shape=jax.ShapeDtypeStruct((M_ALL, ATT_Q), BF16),
        scratch_shapes=scratch(ATT_XT, NKEYS),
        compiler_params=_cparams(2),
        name="attention",
    )(p, p, p, p, selk, selvt, eye)
    return pl.pallas_call(
        _make_attn_kernel(1, False),
        grid=(BATCH,),
        in_specs=[pl.BlockSpec((TC, ATT_Q), lambda b: (XB + b, cq)),
                  pl.BlockSpec((TC, ATT_Q), lambda b: (XB + b, cg)),
                  pl.BlockSpec((CTX_LEN, 2 * ATT_KV), lambda b: (XB + b, ckv))] + consts
                 + [pl.BlockSpec(memory_space=pl.ANY)],
        out_specs=pl.BlockSpec((TC, ATT_Q), lambda b: (XB + b, 0)),
        out_shape=jax.ShapeDtypeStruct((M_ALL, ATT_Q), BF16),
        input_output_aliases={6: 0},
        scratch_shapes=scratch(1, CTX_LEN),
        compiler_params=_cparams(1),
        name="attention_ctx",
    )(p, p, p, selk, selvt, eye, oc)


def _final_kernel(sx_ref, sc_ref, sig_ref, gg_ref, ua_ref, of_ref, ob_ref, oc_ref, mod_ref,
                  wc_ref, wg_ref, wa_ref, wo_ref, gn_ref, o_ref):
    i = pl.program_id(0)
    row = jnp.where(i >= N_XT, BATCH, i // TILES_PER_SEQ)
    ya = jnp.dot(ua_ref[...], wc_ref[...], preferred_element_type=F32)
    og = of_ref[...].astype(F32) + ob_ref[...].astype(F32)
    gn = gn_ref[...]
    parts = []
    for h in range(GLA_HEADS):
        oh = og[:, h * GLA_DV:(h + 1) * GLA_DV]
        ms = jnp.mean(oh * oh, axis=-1, keepdims=True)
        parts.append(oh * lax.rsqrt(ms + EPS) * gn)
    on = jnp.concatenate(parts, axis=1) * gg_ref[...].astype(F32)
    yb = jnp.dot(on.astype(BF16), wg_ref[...], preferred_element_type=F32)
    yc = jnp.dot(oc_ref[...], wa_ref[...], preferred_element_type=F32)
    merged = (sig_ref[:, 0:D_MODEL].astype(F32) * ya
              + sig_ref[:, D_MODEL:2 * D_MODEL].astype(F32) * yb
              + sig_ref[:, 2 * D_MODEL:3 * D_MODEL].astype(F32) * yc)
    out = jnp.dot(merged.astype(BF16), wo_ref[...], preferred_element_type=F32)
    gate = mod_ref[pl.ds(row, 1), 2 * D_MODEL:3 * D_MODEL]
    o_ref[...] = jnp.where(i >= N_XT, sc_ref[...], sx_ref[...]) + gate * out


def _final(l, n_tiles, xs, cs, ctx_off, p, ua, o_gla, oc, mod, wc, wg, wa, wo, gn):
    rowblk = lambda i: (i, 0)
    wspec = lambda k: pl.BlockSpec((None, k, D_MODEL), lambda i: (l, 0, 0))
    return pl.pallas_call(
        _final_kernel,
        grid=(n_tiles,),
        in_specs=_stream_specs(ctx_off) + [
                  pl.BlockSpec((TM, 3 * D_MODEL), lambda i: (i, P_SIG // (3 * D_MODEL))),
                  pl.BlockSpec((TM, GLA_V), lambda i: (i, P_GG // GLA_V)),
                  pl.BlockSpec((TM, D_CONV), rowblk),
                  pl.BlockSpec((TM, GLA_V), rowblk),
                  pl.BlockSpec((TM, GLA_V), rowblk),
                  pl.BlockSpec((TM, ATT_Q), rowblk),
                  pl.BlockSpec((None, 16, 3 * D_MODEL), lambda i: (l, 0, 0)),
                  wspec(D_CONV), wspec(GLA_V), wspec(ATT_Q), wspec(D_MODEL),
                  pl.BlockSpec((None, 1, GLA_DV), lambda i: (l, 0, 0))],
        out_specs=pl.BlockSpec((TM, D_MODEL), rowblk),
        out_shape=jax.ShapeDtypeStruct((n_tiles * TM, D_MODEL), F32),
        compiler_params=_cparams(1),
        name="merge_out",
    )(xs, cs, p, p, ua, o_gla[0], o_gla[1], oc, mod, wc, wg, wa, wo, gn)


def _rope_tables():
    t = np.arange(SEQ)
    row = (t // GRID_W).astype(np.float32)
    col = (t % GRID_W).astype(np.float32)
    n_freq = ROPE_AXIS_DIM // 2
    freqs = (np.float32(ROPE_THETA) ** (-np.arange(n_freq, dtype=np.float32) / n_freq)).astype(np.float32)
    ar = row[:, None] * freqs
    ac = col[:, None] * freqs
    cos64 = np.concatenate([np.cos(ar), np.cos(ar), np.cos(ac), np.cos(ac)], axis=1)
    sin64 = np.concatenate([-np.sin(ar), np.sin(ar), -np.sin(ac), np.sin(ac)], axis=1)
    return (np.tile(cos64, (1, ATT_HEADS)).astype(np.float32),
            np.tile(sin64, (1, ATT_HEADS)).astype(np.float32))


def _static_tables():
    bd = np.kron(np.eye(ATT_HEADS, dtype=np.float32), np.ones((ATT_HD, ATT_HD), np.float32))
    idx = np.arange(GLA_CHUNK)
    tri = np.stack([(idx[None, :] <= idx[:, None]), (idx[None, :] >= idx[:, None])]).astype(np.float32)
    selk = np.zeros((ATT_KV_HEADS, 2, 2 * ATT_KV, LANES), np.float32)
    selvt = np.zeros((ATT_KV_HEADS, VROWS, 2 * ATT_KV), np.float32)
    for h in range(ATT_KV_HEADS):
        for dd in range(ATT_HD):
            selvt[h, dd, ATT_KV + h * ATT_HD + dd] = 1.0
            for e in range(2):
                selk[h, e, h * ATT_HD + dd, e * ATT_HD + dd] = 1.0
    shifts = np.stack([np.eye(CONV_SH, k=r, dtype=np.float32) for r in range(SUBLANES)])
    return bd, tri, selk, selvt, shifts


def _split_cols(a):
    pad = jnp.zeros(a.shape[:-1] + (LANES - 2 * GLA_RANK,), a.dtype)
    return (a[..., :O_LR], a[..., O_AQ:], jnp.concatenate([a[..., O_LR:O_AQ], pad], axis=-1))


def kernel(x, c, ctx, c_ctx, norm_g, w_mod, b_mod, w_in, b_in, conv_dw_w, conv_dw_b, conv_ln_g,
           conv_ln_b, w_conv_out, gla_w_gate, gla_b_gate, gla_norm_g, w_gla_out, q_norm_g,
           k_norm_g, w_attn_out, w_out):
    cos_np, sin_np = _rope_tables()
    bd_np, tri_np, selk_np, selvt_np, shifts_np = _static_tables()
    shifts = jnp.asarray(shifts_np, BF16)
    cos_t, sin_t = jnp.asarray(cos_np), jnp.asarray(sin_np)
    bd = jnp.asarray(bd_np, BF16)
    tri = jnp.asarray(tri_np)
    selk = jnp.asarray(selk_np, BF16)
    selvt = jnp.asarray(selvt_np, BF16)
    eye = jnp.eye(LANES, dtype=BF16)

    w_bf = w_in.astype(BF16)
    w_abc = (w_bf,) + _split_cols(w_bf)[1:]
    b_abc = tuple(b.reshape(DEPTH, 1, b.shape[-1]) for b in _split_cols(b_in))
    cc = jnp.concatenate([c, c_ctx[None, :], jnp.zeros((16 - BATCH - 1, D_MODEL), F32)], axis=0)
    dw_w = jnp.broadcast_to(conv_dw_w[:, :, None, :], (DEPTH, CONV_WIDTH, SUBLANES, D_CONV))
    r3 = lambda a: a.reshape(DEPTH, 1, a.shape[-1])
    wup = jnp.zeros((DEPTH, LANES, 2 * GLA_QK), F32)
    wup = wup.at[:, 0:GLA_RANK, 0:GLA_QK].set(gla_w_gate[:, 0])
    wup = wup.at[:, GLA_RANK:2 * GLA_RANK, GLA_QK:].set(gla_w_gate[:, 1])
    wuh = wup.astype(BF16)
    wul = (wup - wuh.astype(F32)).astype(BF16)
    bup = gla_b_gate.reshape(DEPTH, 1, 2 * GLA_QK)
    qg = jnp.tile(q_norm_g, (1, ATT_HEADS)).reshape(DEPTH, 1, ATT_Q)
    kg = jnp.tile(k_norm_g, (1, ATT_KV_HEADS)).reshape(DEPTH, 1, ATT_KV)
    wc, wg, wa, wo = (w.astype(BF16) for w in (w_conv_out, w_gla_out, w_attn_out, w_out))

    mod = _modulation(cc, w_mod, b_mod)
    xs, cs, ctx_off = x.reshape(MX, D_MODEL), ctx.reshape(MC, D_MODEL), 0
    for l in range(DEPTH):
        p, dec = _inproj(l, xs, cs, ctx_off, mod, r3(norm_g), w_abc, b_abc, cos_t, sin_t, qg, kg, bd, wuh, wul, bup)
        ua = _conv(l, p, dw_w, r3(conv_dw_b), r3(conv_ln_g), r3(conv_ln_b), shifts)
        o_gla = _gla(p, dec, tri)
        oc = _attn(p, selk, selvt, eye)
        n_tiles = M_ALL // TM if l < DEPTH - 1 else N_XT
        xs = _final(l, n_tiles, xs, cs, ctx_off, p, ua, o_gla, oc, mod, wc, wg, wa, wo, r3(gla_norm_g))
        cs, ctx_off = xs, N_XT
    return xs.reshape(BATCH, SEQ, D_MODEL)
```

```python
import math

import numpy as np
import jax
import jax.numpy as jnp
from jax import lax
from jax.experimental import pallas as pl
from jax.experimental.pallas import tpu as pltpu

F32 = jnp.float32
BF16 = jnp.bfloat16
HIGHEST = lax.Precision.HIGHEST

D_MODEL = 1024
BATCH = 8
SEQ = 4096
DEPTH = 4
CTX_LEN = 256
GRID_W = 64
EPS = 1e-6
D_CONV = 512
CONV_WIDTH = 31
CONV_PAD = CONV_WIDTH // 2
GLA_HEADS = 4
GLA_DK = 64
GLA_DV = 128
GLA_QK = GLA_HEADS * GLA_DK
GLA_V = GLA_HEADS * GLA_DV
GLA_RANK = 16
GLA_GATE_NORM = 16.0
GLA_CHUNK = 64
ATT_HEADS = 8
ATT_KV_HEADS = 2
ATT_GROUP = ATT_HEADS // ATT_KV_HEADS
ATT_HD = 64
ATT_Q = ATT_HEADS * ATT_HD
ATT_KV = ATT_KV_HEADS * ATT_HD
ROPE_AXIS_DIM = ATT_HD // 2
ROPE_THETA = 10000.0

LANES = 128
SUBLANES = 8
MX = BATCH * SEQ
MC = BATCH * CTX_LEN
M_ALL = MX + MC

O_VAL, O_GLU, O_CGATE = 0, 512, 1024
O_GQ, O_GK, O_GV, O_GG = 1536, 1792, 2048, 2560
O_LR = 3072
O_AQ, O_AK, O_AV, O_AG = 3104, 3616, 3744, 3872
O_MA = 4384
N_IN = 7456

A_VAL, A_GLU, A_CG, A_GQK, A_GV, A_GG = 0, 512, 1024, 1536, 2048, 2560
NA = O_LR
B_AQ, B_AKV, B_AG, B_M = 0, 512, 768, 1280
NB = N_IN - O_AQ

P_SIG = 0
P_U = 3072
P_CG = 3584
P_GV = 4096
P_GD = 4608
GD_W = 3 * GLA_QK
P_GG = 6144
P_AQ = 6656
P_AG = 7168
P_AKV = 7680
NP = 7936

TM = 512
N_XT = MX // TM
N_CT = MC // TM
TILES_PER_SEQ = SEQ // TM
TC = 256
XB = MX // TC
BLK_PER_SEQ = SEQ // TC
HALO = 16
CONV_RC = 32
CONV_SH = TC // 2 + 2 * HALO
KB = 512
Q_PRESCALE = (ATT_HD ** -0.5) * math.log2(math.e)

VMEM_LIMIT = 56 * 1024 * 1024


def _cparams(n_axes, vmem=VMEM_LIMIT):
    return pltpu.CompilerParams(dimension_semantics=("arbitrary",) * n_axes,
                                vmem_limit_bytes=vmem)


def _silu(x):
    return x * jax.nn.sigmoid(x)


def _mod_kernel(c_ref, w_ref, b_ref, o_ref):
    s = _silu(c_ref[...])
    o_ref[...] = jnp.dot(s, w_ref[...], preferred_element_type=F32, precision=HIGHEST) + b_ref[...]


def _modulation(cc, w_mod, b_mod):
    nt = 3 * D_MODEL // 1024
    return pl.pallas_call(
        _mod_kernel,
        grid=(DEPTH, nt),
        in_specs=[pl.BlockSpec((16, D_MODEL), lambda l, n: (0, 0)),
                  pl.BlockSpec((None, D_MODEL, 1024), lambda l, n: (l, 0, n)),
                  pl.BlockSpec((None, 1, 1024), lambda l, n: (l, 0, n))],
        out_specs=pl.BlockSpec((None, 16, 1024), lambda l, n: (l, 0, n)),
        out_shape=jax.ShapeDtypeStruct((DEPTH, 16, 3 * D_MODEL), F32),
        compiler_params=_cparams(2),
        name="modulation",
    )(cc, w_mod, b_mod.reshape(DEPTH, 1, 3 * D_MODEL))


def _head_norm(xv, gain, bd):
    ss = jnp.dot((xv * xv).astype(BF16), bd, preferred_element_type=F32)
    return xv * lax.rsqrt(ss * (1.0 / ATT_HD) + EPS) * gain


def _rope(xv, cosv, sinv):
    parts = []
    for s in range(xv.shape[1] // LANES):
        sl = slice(s * LANES, (s + 1) * LANES)
        xs = xv[:, sl]
        up = pltpu.roll(xs, LANES - 16, axis=1)
        dn = pltpu.roll(xs, 16, axis=1)
        lane = lax.broadcasted_iota(jnp.int32, xs.shape, 1)
        partner = jnp.where((lane & 16) == 0, up, dn)
        parts.append(xs * cosv[:, sl] + partner * sinv[:, sl])
    return jnp.concatenate(parts, axis=1) if len(parts) > 1 else parts[0]


def _inproj_kernel(x_ref, c_ref, mod_ref, g_ref, wa_ref, wb_ref, wc_ref, ba_ref, bb_ref, bc_ref,
                   cos_ref, sin_ref, qg_ref, kg_ref,
                   bd_ref, wuh_ref, wul_ref, bup_ref, o_ref, dec_ref):
    i = pl.program_id(0)
    is_ctx = i >= N_XT
    row = jnp.where(is_ctx, BATCH, i // TILES_PER_SEQ)
    x = jnp.where(is_ctx, c_ref[...], x_ref[...])
    ms = jnp.mean(x * x, axis=-1, keepdims=True)
    y = x * lax.rsqrt(ms + EPS) * g_ref[...]
    m = mod_ref[pl.ds(row, 1), :]
    shift = m[:, 0:D_MODEL]
    scale = m[:, D_MODEL:2 * D_MODEL]
    h = (y * (1.0 + scale) + shift).astype(BF16)

    def proj_from(w_ref, b_ref):
        def proj(a, n):
            return jnp.dot(h, w_ref[:, a:a + n], preferred_element_type=F32) + b_ref[:, a:a + n]
        return proj

    proj_a = proj_from(wa_ref, ba_ref)
    proj_b = proj_from(wb_ref, bb_ref)
    proj_c = proj_from(wc_ref, bc_ref)

    def put(a, val):
        o_ref[:, a:a + val.shape[1]] = val.astype(BF16)

    lr = proj_c(0, LANES)
    aq_raw = proj_b(B_AQ, ATT_Q)
    akv = proj_b(B_AKV, 2 * ATT_KV)
    put(P_SIG, jax.nn.sigmoid(proj_b(B_M, D_MODEL)))

    lr_hi = lr.astype(BF16)
    lr_lo = (lr - lr_hi.astype(F32)).astype(BF16)
    wuh = wuh_ref[...]
    z = (jnp.dot(lr_hi, wuh, preferred_element_type=F32)
         + jnp.dot(lr_lo, wuh, preferred_element_type=F32)
         + jnp.dot(lr_hi, wul_ref[...], preferred_element_type=F32)) + bup_ref[...]
    put(P_SIG + D_MODEL, jax.nn.sigmoid(proj_b(B_M + D_MODEL, D_MODEL)))

    cosv = jnp.where(is_ctx, 1.0, cos_ref[...])
    sinv = jnp.where(is_ctx, 0.0, sin_ref[...])
    bd = bd_ref[...]
    aq = _head_norm(aq_raw, qg_ref[...], bd)
    put(P_AQ, _rope(aq, cosv, sinv) * Q_PRESCALE)
    ak = _head_norm(akv[:, :ATT_KV], kg_ref[...], bd[:ATT_KV, :ATT_KV])
    put(P_AKV, _rope(ak, cosv[:, :ATT_KV], sinv[:, :ATT_KV]))
    put(P_AKV + ATT_KV, akv[:, ATT_KV:])
    put(P_SIG + 2 * D_MODEL, jax.nn.sigmoid(proj_b(B_M + 2 * D_MODEL, D_MODEL)))

    la = (jnp.minimum(z, 0.0) - jnp.log(1.0 + jnp.exp(-jnp.abs(z)))) * (1.0 / GLA_GATE_NORM)
    rowc = lax.broadcasted_iota(jnp.int32, (TM, GLA_QK), 0) & (GLA_CHUNK - 1)
    cf = la[:, :GLA_QK]
    cb = la[:, GLA_QK:]
    sh = 1
    while sh < GLA_CHUNK:
        cf = cf + jnp.where(rowc >= sh, pltpu.roll(cf, sh, axis=0), 0.0)
        cb = cb + jnp.where(rowc < GLA_CHUNK - sh, pltpu.roll(cb, TM - sh, axis=0), 0.0)
        sh *= 2
    n_ch = TM // GLA_CHUNK
    last_f = [cf[c * GLA_CHUNK + GLA_CHUNK - 1:(c + 1) * GLA_CHUNK, :] for c in range(n_ch)]
    last_b = [cb[c * GLA_CHUNK:c * GLA_CHUNK + 1, :] for c in range(n_ch)]
    dec_ref[0] = jnp.concatenate(last_f, axis=0)
    dec_ref[1] = jnp.concatenate(last_b, axis=0)
    qk = proj_a(A_GQK, 2 * GLA_QK)
    gq = qk[:, :GLA_QK] * (GLA_DK ** -0.5)
    gk = qk[:, GLA_QK:]
    for dd, (cum, last) in enumerate(((cf, last_f), (cb, last_b))):
        tot = jnp.concatenate([jnp.broadcast_to(t, (GLA_CHUNK, GLA_QK)) for t in last], axis=0)
        base = P_GD + dd * GD_W
        put(base, gq * jnp.exp(cum))
        put(base + GLA_QK, gk * jnp.exp(-cum))
        put(base + 2 * GLA_QK, gk * jnp.exp(tot - cum))

    put(P_U, proj_a(A_VAL, D_CONV) * jax.nn.sigmoid(proj_a(A_GLU, D_CONV)))
    put(P_CG, _silu(proj_a(A_CG, D_CONV)))
    put(P_GG, _silu(proj_a(A_GG, GLA_V)))
    put(P_AG, _silu(proj_b(B_AG, ATT_Q)))
    put(P_GV, proj_a(A_GV, GLA_V))


def _stream_specs(ctx_off):
    return [pl.BlockSpec((TM, D_MODEL), lambda i: (jnp.minimum(i, N_XT - 1), 0)),
            pl.BlockSpec((TM, D_MODEL), lambda i: (jnp.maximum(i - N_XT, 0) + ctx_off, 0))]


def _inproj(l, xs, cs, ctx_off, mod, norm_g, w_abc, b_abc, cos_t, sin_t, qg, kg, bd, wuh, wul, bup):
    const = lambda i: (0, 0)
    pos = lambda i: (jnp.where(i >= N_XT, 0, i % TILES_PER_SEQ), 0)
    return pl.pallas_call(
        _inproj_kernel,
        grid=(M_ALL // TM,),
        in_specs=_stream_specs(ctx_off) + [
                  pl.BlockSpec((None, 16, 3 * D_MODEL), lambda i: (l, 0, 0)),
                  pl.BlockSpec((None, 1, D_MODEL), lambda i: (l, 0, 0)),
                  ] + [pl.BlockSpec((None, D_MODEL, n), lambda i: (l, 0, 0),
                                    pipeline_mode=pl.Buffered(1)) for n in (NA, NB, LANES)
                  ] + [pl.BlockSpec((None, 1, b.shape[-1]), lambda i: (l, 0, 0)) for b in b_abc
                  ] + [
                  pl.BlockSpec((TM, ATT_Q), pos),
                  pl.BlockSpec((TM, ATT_Q), pos),
                  pl.BlockSpec((None, 1, ATT_Q), lambda i: (l, 0, 0)),
                  pl.BlockSpec((None, 1, ATT_KV), lambda i: (l, 0, 0)),
                  pl.BlockSpec((ATT_Q, ATT_Q), const),
                  pl.BlockSpec((None, LANES, 2 * GLA_QK), lambda i: (l, 0, 0)),
                  pl.BlockSpec((None, LANES, 2 * GLA_QK), lambda i: (l, 0, 0)),
                  pl.BlockSpec((None, 1, 2 * GLA_QK), lambda i: (l, 0, 0))],
        out_specs=[pl.BlockSpec((TM, NP), lambda i: (i, 0)),
                   pl.BlockSpec((2, TM // GLA_CHUNK, GLA_QK), lambda i: (0, i, 0))],
        out_shape=[jax.ShapeDtypeStruct((M_ALL, NP), BF16),
                   jax.ShapeDtypeStruct((2, M_ALL // GLA_CHUNK, GLA_QK), F32)],
        compiler_params=_cparams(1),
        name="inproj",
    )(xs, cs, mod, norm_g, *w_abc, *b_abc, cos_t, sin_t, qg, kg, bd, wuh, wul, bup)


def _conv_kernel(u_ref, ul_ref, ur_ref, sg_ref, dw_ref, dwb_ref, lng_ref, lnb_ref, sh_ref,
                 o_ref, win_ref, ext_ref):
    i = pl.program_id(0)
    j = i % BLK_PER_SEQ
    is_x = i < XB
    left_ok = jnp.logical_and(is_x, j != 0)
    right_ok = jnp.logical_and(is_x, j != BLK_PER_SEQ - 1)
    zero_h = jnp.zeros((HALO, D_CONV), BF16)
    win_ref[0:HALO, :] = jnp.where(left_ok, ul_ref[...], zero_h)
    win_ref[HALO:HALO + TC, :] = u_ref[...]
    win_ref[HALO + TC:, :] = jnp.where(right_ok, ur_ref[...], zero_h)
    ext_ref[0] = win_ref[...].astype(F32)
    half = TC // 2
    for r in range(1, SUBLANES):
        for a in range(2):
            ext_ref[r, a * half:a * half + CONV_SH, :] = jnp.dot(
                sh_ref[r], win_ref[a * half:a * half + CONV_SH, :], preferred_element_type=F32)
    bias = dwb_ref[...]
    lng = lng_ref[...]
    lnb = lnb_ref[...]
    for c in range(TC // CONV_RC):
        r0 = c * CONV_RC
        acc = jnp.broadcast_to(bias, (CONV_RC, D_CONV))
        for t in range(CONV_WIDTH):
            off = t + HALO - CONV_PAD
            a0 = r0 + (off // SUBLANES) * SUBLANES
            w_t = jnp.concatenate([dw_ref[t]] * (CONV_RC // SUBLANES), axis=0)
            acc = acc + ext_ref[off % SUBLANES, a0:a0 + CONV_RC, :] * w_t
        mu = jnp.mean(acc, axis=-1, keepdims=True)
        d = acc - mu
        var = jnp.mean(d * d, axis=-1, keepdims=True)
        yn = d * lax.rsqrt(var + EPS) * lng + lnb
        o_ref[r0:r0 + CONV_RC, :] = (_silu(yn) * sg_ref[r0:r0 + CONV_RC, :].astype(F32)).astype(BF16)


def _conv(l, p, dw_w, dw_b, ln_g, ln_b, shifts):
    nhb = M_ALL // HALO
    per = TC // HALO
    cu = P_U // D_CONV
    cg = P_CG // D_CONV
    vec = lambda i: (l, 0, 0)
    return pl.pallas_call(
        _conv_kernel,
        grid=(M_ALL // TC,),
        in_specs=[pl.BlockSpec((TC, D_CONV), lambda i: (i, cu)),
                  pl.BlockSpec((HALO, D_CONV), lambda i: (jnp.maximum(i * per - 1, 0), cu)),
                  pl.BlockSpec((HALO, D_CONV), lambda i: (jnp.minimum((i + 1) * per, nhb - 1), cu)),
                  pl.BlockSpec((TC, D_CONV), lambda i: (i, cg)),
                  pl.BlockSpec((None, CONV_WIDTH, SUBLANES, D_CONV), lambda i: (l, 0, 0, 0)),
                  pl.BlockSpec((None, 1, D_CONV), vec),
                  pl.BlockSpec((None, 1, D_CONV), vec),
                  pl.BlockSpec((None, 1, D_CONV), vec),
                  pl.BlockSpec((SUBLANES, CONV_SH, CONV_SH), lambda i: (0, 0, 0))],
        out_specs=pl.BlockSpec((TC, D_CONV), lambda i: (i, 0)),
        out_shape=jax.ShapeDtypeStruct((M_ALL, D_CONV), BF16),
        scratch_shapes=[pltpu.VMEM((TC + 2 * HALO, D_CONV), BF16),
                        pltpu.VMEM((SUBLANES, TC + 2 * HALO, D_CONV), F32)],
        compiler_params=_cparams(1),
        name="conv",
    )(p, p, p, p, dw_w, dw_b, ln_g, ln_b, shifts)


def _gla_kernel(qf_ref, qb_ref, vf_ref, vb_ref, df_ref, db_ref, tri_ref, of_ref, ob_ref, st_ref):
    step = pl.program_id(1)

    @pl.when(step == 0)
    def _():
        st_ref[...] = jnp.zeros_like(st_ref)

    lane_h = lax.broadcasted_iota(jnp.int32, (GLA_CHUNK, GLA_QK), 1) // GLA_DK
    srow_h = lax.broadcasted_iota(jnp.int32, (GLA_V, GLA_QK), 0) // GLA_DV
    scol_h = lax.broadcasted_iota(jnp.int32, (GLA_V, GLA_QK), 1) // GLA_DK
    smask = srow_h == scol_h
    nt = (((1,), (1,)), ((), ()))
    n_chunks = TC // GLA_CHUNK
    dirs = ((qf_ref, vf_ref, df_ref, of_ref), (qb_ref, vb_ref, db_ref, ob_ref))
    work = []
    for n in range(n_chunks):
        for d, (qkk_ref, v_ref, dec_ref, o_ref) in enumerate(dirs):
            c = n if d == 0 else n_chunks - 1 - n
            rows = slice(c * GLA_CHUNK, (c + 1) * GLA_CHUNK)
            q_in = qkk_ref[rows, 0:GLA_QK]
            k_in = qkk_ref[rows, GLA_QK:2 * GLA_QK]
            k_st = qkk_ref[rows, 2 * GLA_QK:3 * GLA_QK]
            v = v_ref[rows, :]
            decay = jnp.exp(dec_ref[c:c + 1, :])
            tri4 = jnp.concatenate([tri_ref[d]] * GLA_HEADS, axis=0) > 0.5
            q_stack = jnp.concatenate(
                [jnp.where(lane_h == h, q_in, jnp.zeros_like(q_in)) for h in range(GLA_HEADS)],
                axis=0)
            att = lax.dot_general(q_stack, k_in, nt, preferred_element_type=F32)
            att = jnp.where(tri4, att, 0.0).astype(BF16)
            kvt = lax.dot_general(v, k_st, (((0,), (0,)), ((), ())),
                                  preferred_element_type=F32)
            work.append((d, o_ref, rows, q_in, v, decay, att, jnp.where(smask, kvt, 0.0)))
    for d, o_ref, rows, q_in, v, decay, att, kvt in work:
        st_old = st_ref[d]
        o_inter = lax.dot_general(q_in, st_old.astype(BF16), nt, preferred_element_type=F32)
        o_intra = jnp.concatenate(
            [jnp.dot(att[h * GLA_CHUNK:(h + 1) * GLA_CHUNK, :],
                     v[:, h * GLA_DV:(h + 1) * GLA_DV], preferred_element_type=F32)
             for h in range(GLA_HEADS)], axis=1)
        o_ref[rows, :] = (o_intra + o_inter).astype(BF16)
        st_ref[d] = st_old * decay + kvt


def _gla(p, dec, tri):
    def rbf(b, s):
        return jnp.where(s == 0, XB + b, b * BLK_PER_SEQ + s - 1)

    def rbb(b, s):
        return jnp.where(s == 0, XB + b, b * BLK_PER_SEQ + BLK_PER_SEQ - s)

    cgd = P_GD // GD_W
    cv = P_GV // GLA_V
    n_ch = TC // GLA_CHUNK
    dec4 = dec.reshape(2, M_ALL // TC, n_ch, GLA_QK)
    out = jax.ShapeDtypeStruct((M_ALL, GLA_V), BF16)
    return pl.pallas_call(
        _gla_kernel,
        grid=(BATCH, BLK_PER_SEQ + 1),
        in_specs=[pl.BlockSpec((TC, GD_W), lambda b, s: (rbf(b, s), cgd)),
                  pl.BlockSpec((TC, GD_W), lambda b, s: (rbb(b, s), cgd + 1)),
                  pl.BlockSpec((TC, GLA_V), lambda b, s: (rbf(b, s), cv)),
                  pl.BlockSpec((TC, GLA_V), lambda b, s: (rbb(b, s), cv)),
                  pl.BlockSpec((None, None, n_ch, GLA_QK), lambda b, s: (0, rbf(b, s), 0, 0)),
                  pl.BlockSpec((None, None, n_ch, GLA_QK), lambda b, s: (1, rbb(b, s), 0, 0)),
                  pl.BlockSpec((2, GLA_CHUNK, GLA_CHUNK), lambda b, s: (0, 0, 0))],
        out_specs=[pl.BlockSpec((TC, GLA_V), lambda b, s: (rbf(b, s), 0)),
                   pl.BlockSpec((TC, GLA_V), lambda b, s: (rbb(b, s), 0))],
        out_shape=[out, out],
        scratch_shapes=[pltpu.VMEM((2, GLA_V, GLA_QK), F32)],
        compiler_params=_cparams(2),
        name="gla",
    )(p, p, p, p, dec4, dec4, tri)


NKEYS = CTX_LEN + SEQ
VROWS = 80
NT_DIMS = (((1,), (1,)), ((), ()))


def _make_attn_kernel(n_tiles, with_x):
    n_xb = SEQ // KB

    def kernel(*refs):
        if with_x:
            (q_ref, g_ref, ckv_ref, xkv_ref, selk_ref, selvt_ref, eye_ref, o_ref,
             kz_ref, vtc_ref, vtx_ref, qt_ref, m_ref, al_ref, acc_ref, s_ref) = refs
            key_srcs = ((ckv_ref, 0, CTX_LEN), (xkv_ref, CTX_LEN, SEQ))
        else:
            (q_ref, g_ref, ckv_ref, selk_ref, selvt_ref, eye_ref, _, o_ref,
             kz_ref, vtc_ref, qt_ref, m_ref, al_ref, acc_ref, s_ref) = refs
            key_srcs = ((ckv_ref, 0, CTX_LEN),)
        row_v = lax.broadcasted_iota(jnp.int32, (VROWS, 1), 0)

        def ext_values(svt, blk):
            vt = lax.dot_general(svt, blk, NT_DIMS, preferred_element_type=F32)
            return jnp.where(row_v == ATT_HD, 1.0, vt).astype(BF16)

        def prepare():
            for h in range(ATT_KV_HEADS):
                for e in range(2):
                    sk = selk_ref[h, e]
                    for (src, r0, n) in key_srcs:
                        for r in range(0, n, 1024):
                            nr = min(1024, n - r)
                            kz_ref[h, e, r0 + r:r0 + r + nr, :] = jnp.dot(
                                src[r:r + nr, :], sk, preferred_element_type=F32).astype(BF16)
                svt = selvt_ref[h]
                vtc_ref[h] = ext_values(svt, ckv_ref[...])
                if with_x:
                    for t in range(n_xb):
                        vtx_ref[h, t] = ext_values(svt, xkv_ref[t * KB:(t + 1) * KB, :])

        if with_x:
            pl.when(pl.program_id(1) == 0)(prepare)
        else:
            prepare()

        eye = eye_ref[...]
        for tile in range(n_tiles):
            for pr in range(ATT_HEADS // 2):
                qt_ref[tile, pr] = lax.dot_general(
                    eye, q_ref[tile * TC:(tile + 1) * TC, pr * LANES:(pr + 1) * LANES], NT_DIMS,
                    preferred_element_type=F32).astype(BF16)
        m_ref[...] = jnp.full_like(m_ref, -jnp.inf)
        acc_ref[...] = jnp.zeros_like(acc_ref)

        def scores(nxt, hd):
            tile, slot, k0, nk, _ = nxt
            h, pr, e = hd // ATT_GROUP, hd // 2, hd % 2
            s_ref[slot, hd, 0:nk, :] = jnp.dot(kz_ref[h, e, pl.ds(k0, nk), :], qt_ref[tile, pr],
                                               preferred_element_type=F32)

        def stage(cur, nxt):
            if cur is not None:
                tile, slot, _, nk, vt_of = cur
                for hd in range(ATT_HEADS):
                    s3 = s_ref[slot, hd, 0:nk, :].reshape(nk // SUBLANES, SUBLANES, TC)
                    m_col = jnp.max(jnp.max(s3, axis=0), axis=0, keepdims=True)
                    m_prev = m_ref[tile, hd]
                    m_next = jnp.maximum(m_prev, m_col)
                    al_ref[tile, hd] = jnp.exp2(m_prev - m_next)
                    m_ref[tile, hd] = m_next
            for hd in range(ATT_HEADS):
                if nxt is not None:
                    scores(nxt, hd)
                if cur is not None:
                    s3 = s_ref[slot, hd, 0:nk, :].reshape(nk // SUBLANES, SUBLANES, TC)
                    p = jnp.exp2(s3 - m_ref[tile, hd][None]).reshape(nk, TC).astype(BF16)
                    pv = jnp.dot(vt_of(hd // ATT_GROUP), p, preferred_element_type=F32)
                    acc3 = (acc_ref[tile, hd].reshape(VROWS // SUBLANES, SUBLANES, TC)
                            * al_ref[tile, hd][None])
                    acc_ref[tile, hd] = acc3.reshape(VROWS, TC) + pv

        def finish(tile):
            outs = []
            for hd in range(ATT_HEADS):
                a = acc_ref[tile, hd]
                outs.append(a[0:ATT_HD, :] / a[ATT_HD:ATT_HD + 1, :])
            o_nat = jnp.concatenate(outs, axis=0).T
            rows = slice(tile * TC, (tile + 1) * TC)
            o_ref[rows, :] = (o_nat * g_ref[rows, :].astype(F32)).astype(BF16)

        def ctx_stage(tile, slot):
            return (tile, slot, 0, CTX_LEN, lambda h: vtc_ref[h])

        def x_stage(tile, t, slot):
            k0 = CTX_LEN + t * KB
            if not isinstance(t, int):
                k0 = pl.multiple_of(k0, CTX_LEN)
            return (tile, slot, k0, KB, lambda h: vtx_ref[h, t])

        if not with_x:
            stage(None, ctx_stage(0, 0))
            stage(ctx_stage(0, 0), None)
            finish(0)
            return
        stage(None, ctx_stage(0, 1))
        for tile in range(n_tiles):
            s0 = tile % 2
            stage(ctx_stage(tile, 1 - s0), x_stage(tile, 0, s0))

            def body(j, carry, tile=tile, s0=s0):
                t = 2 * j
                stage(x_stage(tile, t, s0), x_stage(tile, t + 1, 1 - s0))
                stage(x_stage(tile, t + 1, 1 - s0), x_stage(tile, t + 2, s0))
                return carry
            lax.fori_loop(0, (n_xb - 2) // 2, body, 0)
            stage(x_stage(tile, n_xb - 2, s0), x_stage(tile, n_xb - 1, 1 - s0))
            nxt = ctx_stage(tile + 1, s0) if tile + 1 < n_tiles else None
            stage(x_stage(tile, n_xb - 1, 1 - s0), nxt)
            finish(tile)

    return kernel


ATT_XT = 2


def _attn(p, selk, selvt, eye):
    cq = P_AQ // ATT_Q
    cg = P_AG // ATT_Q
    ckv = P_AKV // (2 * ATT_KV)
    tq = ATT_XT * TC
    per_seq = SEQ // tq
    consts = [pl.BlockSpec((ATT_KV_HEADS, 2, 2 * ATT_KV, LANES), lambda *_: (0, 0, 0, 0)),
              pl.BlockSpec((ATT_KV_HEADS, VROWS, 2 * ATT_KV), lambda *_: (0, 0, 0)),
              pl.BlockSpec((LANES, LANES), lambda *_: (0, 0))]

    def scratch(n_tiles, n_keys):
        return [pltpu.VMEM((ATT_KV_HEADS, 2, n_keys, LANES), BF16),
                pltpu.VMEM((ATT_KV_HEADS, VROWS, CTX_LEN), BF16)
                ] + ([pltpu.VMEM((ATT_KV_HEADS, SEQ // KB, VROWS, KB), BF16)] if n_keys > CTX_LEN else []
                ) + [pltpu.VMEM((n_tiles, ATT_HEADS // 2, LANES, TC), BF16),
                     pltpu.VMEM((n_tiles, ATT_HEADS, SUBLANES, TC), F32),
                     pltpu.VMEM((n_tiles, ATT_HEADS, SUBLANES, TC), F32),
                     pltpu.VMEM((n_tiles, ATT_HEADS, VROWS, TC), F32),
                     pltpu.VMEM((2 if n_keys > CTX_LEN else 1, ATT_HEADS,
                                 KB if n_keys > CTX_LEN else CTX_LEN, TC), F32)]

    oc = pl.pallas_call(
        _make_attn_kernel(ATT_XT, True),
        grid=(BATCH, per_seq),
        in_specs=[pl.BlockSpec((tq, ATT_Q), lambda b, qi: (b * per_seq + qi, cq)),
                  pl.BlockSpec((tq, ATT_Q), lambda b, qi: (b * per_seq + qi, cg)),
                  pl.BlockSpec((CTX_LEN, 2 * ATT_KV), lambda b, qi: (XB + b, ckv)),
                  pl.BlockSpec((SEQ, 2 * ATT_KV), lambda b, qi: (b, ckv))] + consts,
        out_specs=pl.BlockSpec((tq, ATT_Q), lambda b, qi: (b * per_seq + qi, 0)),
        out_shape=jax.ShapeDtypeStruct((M_ALL, ATT_Q), BF16),
        scratch_shapes=scratch(ATT_XT, NKEYS),
        compiler_params=_cparams(2),
        name="attention",
    )(p, p, p, p, selk, selvt, eye)
    return pl.pallas_call(
        _make_attn_kernel(1, False),
        grid=(BATCH,),
        in_specs=[pl.BlockSpec((TC, ATT_Q), lambda b: (XB + b, cq)),
                  pl.BlockSpec((TC, ATT_Q), lambda b: (XB + b, cg)),
                  pl.BlockSpec((CTX_LEN, 2 * ATT_KV), lambda b: (XB + b, ckv))] + consts
                 + [pl.BlockSpec(memory_space=pl.ANY)],
        out_specs=pl.BlockSpec((TC, ATT_Q), lambda b: (XB + b, 0)),
        out_shape=jax.ShapeDtypeStruct((M_ALL, ATT_Q), BF16),
        input_output_aliases={6: 0},
        scratch_shapes=scratch(1, CTX_LEN),
        compiler_params=_cparams(1),
        name="attention_ctx",
    )(p, p, p, selk, selvt, eye, oc)


def _final_kernel(sx_ref, sc_ref, sig_ref, gg_ref, ua_ref, of_ref, ob_ref, oc_ref, mod_ref,
                  wc_ref, wg_ref, wa_ref, wo_ref, gn_ref, o_ref):
    i = pl.program_id(0)
    row = jnp.where(i >= N_XT, BATCH, i // TILES_PER_SEQ)
    ya = jnp.dot(ua_ref[...], wc_ref[...], preferred_element_type=F32)
    og = of_ref[...].astype(F32) + ob_ref[...].astype(F32)
    gn = gn_ref[...]
    parts = []
    for h in range(GLA_HEADS):
        oh = og[:, h * GLA_DV:(h + 1) * GLA_DV]
        ms = jnp.mean(oh * oh, axis=-1, keepdims=True)
        parts.append(oh * lax.rsqrt(ms + EPS) * gn)
    on = jnp.concatenate(parts, axis=1) * gg_ref[...].astype(F32)
    yb = jnp.dot(on.astype(BF16), wg_ref[...], preferred_element_type=F32)
    yc = jnp.dot(oc_ref[...], wa_ref[...], preferred_element_type=F32)
    merged = (sig_ref[:, 0:D_MODEL].astype(F32) * ya
              + sig_ref[:, D_MODEL:2 * D_MODEL].astype(F32) * yb
              + sig_ref[:, 2 * D_MODEL:3 * D_MODEL].astype(F32) * yc)
    out = jnp.dot(merged.astype(BF16), wo_ref[...], preferred_element_type=F32)
    gate = mod_ref[pl.ds(row, 1), 2 * D_MODEL:3 * D_MODEL]
    o_ref[...] = jnp.where(i >= N_XT, sc_ref[...], sx_ref[...]) + gate * out


def _final(l, n_tiles, xs, cs, ctx_off, p, ua, o_gla, oc, mod, wc, wg, wa, wo, gn):
    rowblk = lambda i: (i, 0)
    wspec = lambda k: pl.BlockSpec((None, k, D_MODEL), lambda i: (l, 0, 0))
    return pl.pallas_call(
        _final_kernel,
        grid=(n_tiles,),
        in_specs=_stream_specs(ctx_off) + [
                  pl.BlockSpec((TM, 3 * D_MODEL), lambda i: (i, P_SIG // (3 * D_MODEL))),
                  pl.BlockSpec((TM, GLA_V), lambda i: (i, P_GG // GLA_V)),
                  pl.BlockSpec((TM, D_CONV), rowblk),
                  pl.BlockSpec((TM, GLA_V), rowblk),
                  pl.BlockSpec((TM, GLA_V), rowblk),
                  pl.BlockSpec((TM, ATT_Q), rowblk),
                  pl.BlockSpec((None, 16, 3 * D_MODEL), lambda i: (l, 0, 0)),
                  wspec(D_CONV), wspec(GLA_V), wspec(ATT_Q), wspec(D_MODEL),
                  pl.BlockSpec((None, 1, GLA_DV), lambda i: (l, 0, 0))],
        out_specs=pl.BlockSpec((TM, D_MODEL), rowblk),
        out_shape=jax.ShapeDtypeStruct((n_tiles * TM, D_MODEL), F32),
        compiler_params=_cparams(1),
        name="merge_out",
    )(xs, cs, p, p, ua, o_gla[0], o_gla[1], oc, mod, wc, wg, wa, wo, gn)


def _rope_tables():
    t = np.arange(SEQ)
    row = (t // GRID_W).astype(np.float32)
    col = (t % GRID_W).astype(np.float32)
    n_freq = ROPE_AXIS_DIM // 2
    freqs = (np.float32(ROPE_THETA) ** (-np.arange(n_freq, dtype=np.float32) / n_freq)).astype(np.float32)
    ar = row[:, None] * freqs
    ac = col[:, None] * freqs
    cos64 = np.concatenate([np.cos(ar), np.cos(ar), np.cos(ac), np.cos(ac)], axis=1)
    sin64 = np.concatenate([-np.sin(ar), np.sin(ar), -np.sin(ac), np.sin(ac)], axis=1)
    return (np.tile(cos64, (1, ATT_HEADS)).astype(np.float32),
            np.tile(sin64, (1, ATT_HEADS)).astype(np.float32))


def _static_tables():
    bd = np.kron(np.eye(ATT_HEADS, dtype=np.float32), np.ones((ATT_HD, ATT_HD), np.float32))
    idx = np.arange(GLA_CHUNK)
    tri = np.stack([(idx[None, :] <= idx[:, None]), (idx[None, :] >= idx[:, None])]).astype(np.float32)
    selk = np.zeros((ATT_KV_HEADS, 2, 2 * ATT_KV, LANES), np.float32)
    selvt = np.zeros((ATT_KV_HEADS, VROWS, 2 * ATT_KV), np.float32)
    for h in range(ATT_KV_HEADS):
        for dd in range(ATT_HD):
            selvt[h, dd, ATT_KV + h * ATT_HD + dd] = 1.0
            for e in range(2):
                selk[h, e, h * ATT_HD + dd, e * ATT_HD + dd] = 1.0
    shifts = np.stack([np.eye(CONV_SH, k=r, dtype=np.float32) for r in range(SUBLANES)])
    return bd, tri, selk, selvt, shifts


def _split_cols(a):
    pad = jnp.zeros(a.shape[:-1] + (LANES - 2 * GLA_RANK,), a.dtype)
    return (a[..., :O_LR], a[..., O_AQ:], jnp.concatenate([a[..., O_LR:O_AQ], pad], axis=-1))


def kernel(x, c, ctx, c_ctx, norm_g, w_mod, b_mod, w_in, b_in, conv_dw_w, conv_dw_b, conv_ln_g,
           conv_ln_b, w_conv_out, gla_w_gate, gla_b_gate, gla_norm_g, w_gla_out, q_norm_g,
           k_norm_g, w_attn_out, w_out):
    cos_np, sin_np = _rope_tables()
    bd_np, tri_np, selk_np, selvt_np, shifts_np = _static_tables()
    shifts = jnp.asarray(shifts_np, BF16)
    cos_t, sin_t = jnp.asarray(cos_np), jnp.asarray(sin_np)
    bd = jnp.asarray(bd_np, BF16)
    tri = jnp.asarray(tri_np)
    selk = jnp.asarray(selk_np, BF16)
    selvt = jnp.asarray(selvt_np, BF16)
    eye = jnp.eye(LANES, dtype=BF16)

    w_bf = w_in.astype(BF16)
    w_abc = (w_bf,) + _split_cols(w_bf)[1:]
    b_abc = tuple(b.reshape(DEPTH, 1, b.shape[-1]) for b in _split_cols(b_in))
    cc = jnp.concatenate([c, c_ctx[None, :], jnp.zeros((16 - BATCH - 1, D_MODEL), F32)], axis=0)
    dw_w = jnp.broadcast_to(conv_dw_w[:, :, None, :], (DEPTH, CONV_WIDTH, SUBLANES, D_CONV))
    r3 = lambda a: a.reshape(DEPTH, 1, a.shape[-1])
    wup = jnp.zeros((DEPTH, LANES, 2 * GLA_QK), F32)
    wup = wup.at[:, 0:GLA_RANK, 0:GLA_QK].set(gla_w_gate[:, 0])
    wup = wup.at[:, GLA_RANK:2 * GLA_RANK, GLA_QK:].set(gla_w_gate[:, 1])
    wuh = wup.astype(BF16)
    wul = (wup - wuh.astype(F32)).astype(BF16)
    bup = gla_b_gate.reshape(DEPTH, 1, 2 * GLA_QK)
    qg = jnp.tile(q_norm_g, (1, ATT_HEADS)).reshape(DEPTH, 1, ATT_Q)
    kg = jnp.tile(k_norm_g, (1, ATT_KV_HEADS)).reshape(DEPTH, 1, ATT_KV)
    wc, wg, wa, wo = (w.astype(BF16) for w in (w_conv_out, w_gla_out, w_attn_out, w_out))

    mod = _modulation(cc, w_mod, b_mod)
    xs, cs, ctx_off = x.reshape(MX, D_MODEL), ctx.reshape(MC, D_MODEL), 0
    for l in range(DEPTH):
        p, dec = _inproj(l, xs, cs, ctx_off, mod, r3(norm_g), w_abc, b_abc, cos_t, sin_t, qg, kg, bd, wuh, wul, bup)
        ua = _conv(l, p, dw_w, r3(conv_dw_b), r3(conv_ln_g), r3(conv_ln_b), shifts)
        o_gla = _gla(p, dec, tri)
        oc = _attn(p, selk, selvt, eye)
        n_tiles = M_ALL // TM if l < DEPTH - 1 else N_XT
        xs = _final(l, n_tiles, xs, cs, ctx_off, p, ua, o_gla, oc, mod, wc, wg, wa, wo, r3(gla_norm_g))
        cs, ctx_off = xs, N_XT
    return xs.reshape(BATCH, SEQ, D_MODEL)
```

```python
import math

import numpy as np
import jax
import jax.numpy as jnp
from jax import lax
from jax.experimental import pallas as pl
from jax.experimental.pallas import tpu as pltpu

F32 = jnp.float32
BF16 = jnp.bfloat16
HIGHEST = lax.Precision.HIGHEST

D_MODEL = 1024
BATCH = 8
SEQ = 4096
DEPTH = 4
CTX_LEN = 256
GRID_W = 64
EPS = 1e-6
D_CONV = 512
CONV_WIDTH = 31
CONV_PAD = CONV_WIDTH // 2
GLA_HEADS = 4
GLA_DK = 64
GLA_DV = 128
GLA_QK = GLA_HEADS * GLA_DK
GLA_V = GLA_HEADS * GLA_DV
GLA_RANK = 16
GLA_GATE_NORM = 16.0
GLA_CHUNK = 64
ATT_HEADS = 8
ATT_KV_HEADS = 2
ATT_GROUP = ATT_HEADS // ATT_KV_HEADS
ATT_HD = 64
ATT_Q = ATT_HEADS * ATT_HD
ATT_KV = ATT_KV_HEADS * ATT_HD
ROPE_AXIS_DIM = ATT_HD // 2
ROPE_THETA = 10000.0

LANES = 128
SUBLANES = 8
MX = BATCH * SEQ
MC = BATCH * CTX_LEN
M_ALL = MX + MC

O_VAL, O_GLU, O_CGATE = 0, 512, 1024
O_GQ, O_GK, O_GV, O_GG = 1536, 1792, 2048, 2560
O_LR = 3072
O_AQ, O_AK, O_AV, O_AG = 3104, 3616, 3744, 3872
O_MA = 4384
N_IN = 7456

A_VAL, A_GLU, A_CG, A_GQK, A_GV, A_GG = 0, 512, 1024, 1536, 2048, 2560
NA = O_LR
B_AQ, B_AKV, B_AG, B_M = 0, 512, 768, 1280
NB = N_IN - O_AQ

P_SIG = 0
P_U = 3072
P_CG = 3584
P_GV = 4096
P_GD = 4608
GD_W = 3 * GLA_QK
P_GG = 6144
P_AQ = 6656
P_AG = 7168
P_AKV = 7680
NP = 7936

TM = 512
N_XT = MX // TM
N_CT = MC // TM
TILES_PER_SEQ = SEQ // TM
TC = 256
XB = MX // TC
BLK_PER_SEQ = SEQ // TC
HALO = 16
CONV_RC = 32
CONV_SH = TC // 2 + 2 * HALO
KB = 512
Q_PRESCALE = (ATT_HD ** -0.5) * math.log2(math.e)

VMEM_LIMIT = 56 * 1024 * 1024


def _cparams(n_axes, vmem=VMEM_LIMIT):
    return pltpu.CompilerParams(dimension_semantics=("arbitrary",) * n_axes,
                                vmem_limit_bytes=vmem)


def _silu(x):
    return x * jax.nn.sigmoid(x)


def _mod_kernel(c_ref, w_ref, b_ref, o_ref):
    s = _silu(c_ref[...])
    o_ref[...] = jnp.dot(s, w_ref[...], preferred_element_type=F32, precision=HIGHEST) + b_ref[...]


def _modulation(cc, w_mod, b_mod):
    nt = 3 * D_MODEL // 1024
    return pl.pallas_call(
        _mod_kernel,
        grid=(DEPTH, nt),
        in_specs=[pl.BlockSpec((16, D_MODEL), lambda l, n: (0, 0)),
                  pl.BlockSpec((None, D_MODEL, 1024), lambda l, n: (l, 0, n)),
                  pl.BlockSpec((None, 1, 1024), lambda l, n: (l, 0, n))],
        out_specs=pl.BlockSpec((None, 16, 1024), lambda l, n: (l, 0, n)),
        out_shape=jax.ShapeDtypeStruct((DEPTH, 16, 3 * D_MODEL), F32),
        compiler_params=_cparams(2),
        name="modulation",
    )(cc, w_mod, b_mod.reshape(DEPTH, 1, 3 * D_MODEL))


def _head_norm(xv, gain, bd):
    ss = jnp.dot((xv * xv).astype(BF16), bd, preferred_element_type=F32)
    return xv * lax.rsqrt(ss * (1.0 / ATT_HD) + EPS) * gain


def _rope(xv, cosv, sinv):
    parts = []
    for s in range(xv.shape[1] // LANES):
        sl = slice(s * LANES, (s + 1) * LANES)
        xs = xv[:, sl]
        up = pltpu.roll(xs, LANES - 16, axis=1)
        dn = pltpu.roll(xs, 16, axis=1)
        lane = lax.broadcasted_iota(jnp.int32, xs.shape, 1)
        partner = jnp.where((lane & 16) == 0, up, dn)
        parts.append(xs * cosv[:, sl] + partner * sinv[:, sl])
    return jnp.concatenate(parts, axis=1) if len(parts) > 1 else parts[0]


def _inproj_kernel(x_ref, c_ref, mod_ref, g_ref, wa_ref, wb_ref, wc_ref, ba_ref, bb_ref, bc_ref,
                   cos_ref, sin_ref, qg_ref, kg_ref,
                   bd_ref, wuh_ref, wul_ref, bup_ref, o_ref, dec_ref):
    i = pl.program_id(0)
    is_ctx = i >= N_XT
    row = jnp.where(is_ctx, BATCH, i // TILES_PER_SEQ)
    x = jnp.where(is_ctx, c_ref[...], x_ref[...])
    ms = jnp.mean(x * x, axis=-1, keepdims=True)
    y = x * lax.rsqrt(ms + EPS) * g_ref[...]
    m = mod_ref[pl.ds(row, 1), :]
    shift = m[:, 0:D_MODEL]
    scale = m[:, D_MODEL:2 * D_MODEL]
    h = (y * (1.0 + scale) + shift).astype(BF16)

    def proj_from(w_ref, b_ref):
        def proj(a, n):
            return jnp.dot(h, w_ref[:, a:a + n], preferred_element_type=F32) + b_ref[:, a:a + n]
        return proj

    proj_a = proj_from(wa_ref, ba_ref)
    proj_b = proj_from(wb_ref, bb_ref)
    proj_c = proj_from(wc_ref, bc_ref)

    def put(a, val):
        o_ref[:, a:a + val.shape[1]] = val.astype(BF16)

    lr = proj_c(0, LANES)
    aq_raw = proj_b(B_AQ, ATT_Q)
    akv = proj_b(B_AKV, 2 * ATT_KV)
    put(P_SIG, jax.nn.sigmoid(proj_b(B_M, D_MODEL)))

    lr_hi = lr.astype(BF16)
    lr_lo = (lr - lr_hi.astype(F32)).astype(BF16)
    wuh = wuh_ref[...]
    z = (jnp.dot(lr_hi, wuh, preferred_element_type=F32)
         + jnp.dot(lr_lo, wuh, preferred_element_type=F32)
         + jnp.dot(lr_hi, wul_ref[...], preferred_element_type=F32)) + bup_ref[...]
    put(P_SIG + D_MODEL, jax.nn.sigmoid(proj_b(B_M + D_MODEL, D_MODEL)))

    cosv = jnp.where(is_ctx, 1.0, cos_ref[...])
    sinv = jnp.where(is_ctx, 0.0, sin_ref[...])
    bd = bd_ref[...]
    aq = _head_norm(aq_raw, qg_ref[...], bd)
    put(P_AQ, _rope(aq, cosv, sinv) * Q_PRESCALE)
    ak = _head_norm(akv[:, :ATT_KV], kg_ref[...], bd[:ATT_KV, :ATT_KV])
    put(P_AKV, _rope(ak, cosv[:, :ATT_KV], sinv[:, :ATT_KV]))
    put(P_AKV + ATT_KV, akv[:, ATT_KV:])
    put(P_SIG + 2 * D_MODEL, jax.nn.sigmoid(proj_b(B_M + 2 * D_MODEL, D_MODEL)))

    la = (jnp.minimum(z, 0.0) - jnp.log(1.0 + jnp.exp(-jnp.abs(z)))) * (1.0 / GLA_GATE_NORM)
    rowc = lax.broadcasted_iota(jnp.int32, (TM, GLA_QK), 0) & (GLA_CHUNK - 1)
    cf = la[:, :GLA_QK]
    cb = la[:, GLA_QK:]
    sh = 1
    while sh < GLA_CHUNK:
        cf = cf + jnp.where(rowc >= sh, pltpu.roll(cf, sh, axis=0), 0.0)
        cb = cb + jnp.where(rowc < GLA_CHUNK - sh, pltpu.roll(cb, TM - sh, axis=0), 0.0)
        sh *= 2
    n_ch = TM // GLA_CHUNK
    last_f = [cf[c * GLA_CHUNK + GLA_CHUNK - 1:(c + 1) * GLA_CHUNK, :] for c in range(n_ch)]
    last_b = [cb[c * GLA_CHUNK:c * GLA_CHUNK + 1, :] for c in range(n_ch)]
    dec_ref[0] = jnp.concatenate(last_f, axis=0)
    dec_ref[1] = jnp.concatenate(last_b, axis=0)
    qk = proj_a(A_GQK, 2 * GLA_QK)
    gq = qk[:, :GLA_QK] * (GLA_DK ** -0.5)
    gk = qk[:, GLA_QK:]
    for dd, (cum, last) in enumerate(((cf, last_f), (cb, last_b))):
        tot = jnp.concatenate([jnp.broadcast_to(t, (GLA_CHUNK, GLA_QK)) for t in last], axis=0)
        base = P_GD + dd * GD_W
        put(base, gq * jnp.exp(cum))
        put(base + GLA_QK, gk * jnp.exp(-cum))
        put(base + 2 * GLA_QK, gk * jnp.exp(tot - cum))

    put(P_U, proj_a(A_VAL, D_CONV) * jax.nn.sigmoid(proj_a(A_GLU, D_CONV)))
    put(P_CG, _silu(proj_a(A_CG, D_CONV)))
    put(P_GG, _silu(proj_a(A_GG, GLA_V)))
    put(P_AG, _silu(proj_b(B_AG, ATT_Q)))
    put(P_GV, proj_a(A_GV, GLA_V))


def _stream_specs(ctx_off):
    return [pl.BlockSpec((TM, D_MODEL), lambda i: (jnp.minimum(i, N_XT - 1), 0)),
            pl.BlockSpec((TM, D_MODEL), lambda i: (jnp.maximum(i - N_XT, 0) + ctx_off, 0))]


def _inproj(l, xs, cs, ctx_off, mod, norm_g, w_abc, b_abc, cos_t, sin_t, qg, kg, bd, wuh, wul, bup):
    const = lambda i: (0, 0)
    pos = lambda i: (jnp.where(i >= N_XT, 0, i % TILES_PER_SEQ), 0)
    return pl.pallas_call(
        _inproj_kernel,
        grid=(M_ALL // TM,),
        in_specs=_stream_specs(ctx_off) + [
                  pl.BlockSpec((None, 16, 3 * D_MODEL), lambda i: (l, 0, 0)),
                  pl.BlockSpec((None, 1, D_MODEL), lambda i: (l, 0, 0)),
                  ] + [pl.BlockSpec((None, D_MODEL, n), lambda i: (l, 0, 0),
                                    pipeline_mode=pl.Buffered(1)) for n in (NA, NB, LANES)
                  ] + [pl.BlockSpec((None, 1, b.shape[-1]), lambda i: (l, 0, 0)) for b in b_abc
                  ] + [
                  pl.BlockSpec((TM, ATT_Q), pos),
                  pl.BlockSpec((TM, ATT_Q), pos),
                  pl.BlockSpec((None, 1, ATT_Q), lambda i: (l, 0, 0)),
                  pl.BlockSpec((None, 1, ATT_KV), lambda i: (l, 0, 0)),
                  pl.BlockSpec((ATT_Q, ATT_Q), const),
                  pl.BlockSpec((None, LANES, 2 * GLA_QK), lambda i: (l, 0, 0)),
                  pl.BlockSpec((None, LANES, 2 * GLA_QK), lambda i: (l, 0, 0)),
                  pl.BlockSpec((None, 1, 2 * GLA_QK), lambda i: (l, 0, 0))],
        out_specs=[pl.BlockSpec((TM, NP), lambda i: (i, 0)),
                   pl.BlockSpec((2, TM // GLA_CHUNK, GLA_QK), lambda i: (0, i, 0))],
        out_shape=[jax.ShapeDtypeStruct((M_ALL, NP), BF16),
                   jax.ShapeDtypeStruct((2, M_ALL // GLA_CHUNK, GLA_QK), F32)],
        compiler_params=_cparams(1),
        name="inproj",
    )(xs, cs, mod, norm_g, *w_abc, *b_abc, cos_t, sin_t, qg, kg, bd, wuh, wul, bup)


def _conv_kernel(u_ref, ul_ref, ur_ref, sg_ref, dw_ref, dwb_ref, lng_ref, lnb_ref, sh_ref,
                 o_ref, win_ref, ext_ref):
    i = pl.program_id(0)
    j = i % BLK_PER_SEQ
    is_x = i < XB
    left_ok = jnp.logical_and(is_x, j != 0)
    right_ok = jnp.logical_and(is_x, j != BLK_PER_SEQ - 1)
    zero_h = jnp.zeros((HALO, D_CONV), BF16)
    win_ref[0:HALO, :] = jnp.where(left_ok, ul_ref[...], zero_h)
    win_ref[HALO:HALO + TC, :] = u_ref[...]
    win_ref[HALO + TC:, :] = jnp.where(right_ok, ur_ref[...], zero_h)
    ext_ref[0] = win_ref[...].astype(F32)
    half = TC // 2
    for r in range(1, SUBLANES):
        for a in range(2):
            ext_ref[r, a * half:a * half + CONV_SH, :] = jnp.dot(
                sh_ref[r], win_ref[a * half:a * half + CONV_SH, :], preferred_element_type=F32)
    bias = dwb_ref[...]
    lng = lng_ref[...]
    lnb = lnb_ref[...]
    for c in range(TC // CONV_RC):
        r0 = c * CONV_RC
        acc = jnp.broadcast_to(bias, (CONV_RC, D_CONV))
        for t in range(CONV_WIDTH):
            off = t + HALO - CONV_PAD
            a0 = r0 + (off // SUBLANES) * SUBLANES
            w_t = jnp.concatenate([dw_ref[t]] * (CONV_RC // SUBLANES), axis=0)
            acc = acc + ext_ref[off % SUBLANES, a0:a0 + CONV_RC, :] * w_t
        mu = jnp.mean(acc, axis=-1, keepdims=True)
        d = acc - mu
        var = jnp.mean(d * d, axis=-1, keepdims=True)
        yn = d * lax.rsqrt(var + EPS) * lng + lnb
        o_ref[r0:r0 + CONV_RC, :] = (_silu(yn) * sg_ref[r0:r0 + CONV_RC, :].astype(F32)).astype(BF16)


def _conv(l, p, dw_w, dw_b, ln_g, ln_b, shifts):
    nhb = M_ALL // HALO
    per = TC // HALO
    cu = P_U // D_CONV
    cg = P_CG // D_CONV
    vec = lambda i: (l, 0, 0)
    return pl.pallas_call(
        _conv_kernel,
        grid=(M_ALL // TC,),
        in_specs=[pl.BlockSpec((TC, D_CONV), lambda i: (i, cu)),
                  pl.BlockSpec((HALO, D_CONV), lambda i: (jnp.maximum(i * per - 1, 0), cu)),
                  pl.BlockSpec((HALO, D_CONV), lambda i: (jnp.minimum((i + 1) * per, nhb - 1), cu)),
                  pl.BlockSpec((TC, D_CONV), lambda i: (i, cg)),
                  pl.BlockSpec((None, CONV_WIDTH, SUBLANES, D_CONV), lambda i: (l, 0, 0, 0)),
                  pl.BlockSpec((None, 1, D_CONV), vec),
                  pl.BlockSpec((None, 1, D_CONV), vec),
                  pl.BlockSpec((None, 1, D_CONV), vec),
                  pl.BlockSpec((SUBLANES, CONV_SH, CONV_SH), lambda i: (0, 0, 0))],
        out_specs=pl.BlockSpec((TC, D_CONV), lambda i: (i, 0)),
        out_shape=jax.ShapeDtypeStruct((M_ALL, D_CONV), BF16),
        scratch_shapes=[pltpu.VMEM((TC + 2 * HALO, D_CONV), BF16),
                        pltpu.VMEM((SUBLANES, TC + 2 * HALO, D_CONV), F32)],
        compiler_params=_cparams(1),
        name="conv",
    )(p, p, p, p, dw_w, dw_b, ln_g, ln_b, shifts)


def _gla_kernel(qf_ref, qb_ref, vf_ref, vb_ref, df_ref, db_ref, tri_ref, of_ref, ob_ref, st_ref):
    step = pl.program_id(1)

    @pl.when(step == 0)
    def _():
        st_ref[...] = jnp.zeros_like(st_ref)

    lane_h = lax.broadcasted_iota(jnp.int32, (GLA_CHUNK, GLA_QK), 1) // GLA_DK
    srow_h = lax.broadcasted_iota(jnp.int32, (GLA_V, GLA_QK), 0) // GLA_DV
    scol_h = lax.broadcasted_iota(jnp.int32, (GLA_V, GLA_QK), 1) // GLA_DK
    smask = srow_h == scol_h
    nt = (((1,), (1,)), ((), ()))
    n_chunks = TC // GLA_CHUNK
    dirs = ((qf_ref, vf_ref, df_ref, of_ref), (qb_ref, vb_ref, db_ref, ob_ref))
    work = []
    for n in range(n_chunks):
        for d, (qkk_ref, v_ref, dec_ref, o_ref) in enumerate(dirs):
            c = n if d == 0 else n_chunks - 1 - n
            rows = slice(c * GLA_CHUNK, (c + 1) * GLA_CHUNK)
            q_in = qkk_ref[rows, 0:GLA_QK]
            k_in = qkk_ref[rows, GLA_QK:2 * GLA_QK]
            k_st = qkk_ref[rows, 2 * GLA_QK:3 * GLA_QK]
            v = v_ref[rows, :]
            decay = jnp.exp(dec_ref[c:c + 1, :])
            tri4 = jnp.concatenate([tri_ref[d]] * GLA_HEADS, axis=0) > 0.5
            q_stack = jnp.concatenate(
                [jnp.where(lane_h == h, q_in, jnp.zeros_like(q_in)) for h in range(GLA_HEADS)],
                axis=0)
            att = lax.dot_general(q_stack, k_in, nt, preferred_element_type=F32)
            att = jnp.where(tri4, att, 0.0).astype(BF16)
            kvt = lax.dot_general(v, k_st, (((0,), (0,)), ((), ())),
                                  preferred_element_type=F32)
            work.append((d, o_ref, rows, q_in, v, decay, att, jnp.where(smask, kvt, 0.0)))
    for d, o_ref, rows, q_in, v, decay, att, kvt in work:
        st_old = st_ref[d]
        o_inter = lax.dot_general(q_in, st_old.astype(BF16), nt, preferred_element_type=F32)
        o_intra = jnp.concatenate(
            [jnp.dot(att[h * GLA_CHUNK:(h + 1) * GLA_CHUNK, :],
                     v[:, h * GLA_DV:(h + 1) * GLA_DV], preferred_element_type=F32)
             for h in range(GLA_HEADS)], axis=1)
        o_ref[rows, :] = (o_intra + o_inter).astype(BF16)
        st_ref[d] = st_old * decay + kvt


def _gla(p, dec, tri):
    def rbf(b, s):
        return jnp.where(s == 0, XB + b, b * BLK_PER_SEQ + s - 1)

    def rbb(b, s):
        return jnp.where(s == 0, XB + b, b * BLK_PER_SEQ + BLK_PER_SEQ - s)

    cgd = P_GD // GD_W
    cv = P_GV // GLA_V
    n_ch = TC // GLA_CHUNK
    dec4 = dec.reshape(2, M_ALL // TC, n_ch, GLA_QK)
    out = jax.ShapeDtypeStruct((M_ALL, GLA_V), BF16)
    return pl.pallas_call(
        _gla_kernel,
        grid=(BATCH, BLK_PER_SEQ + 1),
        in_specs=[pl.BlockSpec((TC, GD_W), lambda b, s: (rbf(b, s), cgd)),
                  pl.BlockSpec((TC, GD_W), lambda b, s: (rbb(b, s), cgd + 1)),
                  pl.BlockSpec((TC, GLA_V), lambda b, s: (rbf(b, s), cv)),
                  pl.BlockSpec((TC, GLA_V), lambda b, s: (rbb(b, s), cv)),
                  pl.BlockSpec((None, None, n_ch, GLA_QK), lambda b, s: (0, rbf(b, s), 0, 0)),
                  pl.BlockSpec((None, None, n_ch, GLA_QK), lambda b, s: (1, rbb(b, s), 0, 0)),
                  pl.BlockSpec((2, GLA_CHUNK, GLA_CHUNK), lambda b, s: (0, 0, 0))],
        out_specs=[pl.BlockSpec((TC, GLA_V), lambda b, s: (rbf(b, s), 0)),
                   pl.BlockSpec((TC, GLA_V), lambda b, s: (rbb(b, s), 0))],
        out_shape=[out, out],
        scratch_shapes=[pltpu.VMEM((2, GLA_V, GLA_QK), F32)],
        compiler_params=_cparams(2),
        name="gla",
    )(p, p, p, p, dec4, dec4, tri)


NKEYS = CTX_LEN + SEQ
VROWS = 80
NT_DIMS = (((1,), (1,)), ((), ()))


def _make_attn_kernel(n_tiles, with_x):
    n_xb = SEQ // KB

    def kernel(*refs):
        if with_x:
            (q_ref, g_ref, ckv_ref, xkv_ref, selk_ref, selvt_ref, eye_ref, o_ref,
             kz_ref, vtc_ref, vtx_ref, qt_ref, m_ref, al_ref, acc_ref, s_ref) = refs
            key_srcs = ((ckv_ref, 0, CTX_LEN), (xkv_ref, CTX_LEN, SEQ))
        else:
            (q_ref, g_ref, ckv_ref, selk_ref, selvt_ref, eye_ref, _, o_ref,
             kz_ref, vtc_ref, qt_ref, m_ref, al_ref, acc_ref, s_ref) = refs
            key_srcs = ((ckv_ref, 0, CTX_LEN),)
        row_v = lax.broadcasted_iota(jnp.int32, (VROWS, 1), 0)

        def ext_values(svt, blk):
            vt = lax.dot_general(svt, blk, NT_DIMS, preferred_element_type=F32)
            return jnp.where(row_v == ATT_HD, 1.0, vt).astype(BF16)

        def prepare():
            for h in range(ATT_KV_HEADS):
                for e in range(2):
                    sk = selk_ref[h, e]
                    for (src, r0, n) in key_srcs:
                        for r in range(0, n, 1024):
                            nr = min(1024, n - r)
                            kz_ref[h, e, r0 + r:r0 + r + nr, :] = jnp.dot(
                                src[r:r + nr, :], sk, preferred_element_type=F32).astype(BF16)
                svt = selvt_ref[h]
                vtc_ref[h] = ext_values(svt, ckv_ref[...])
                if with_x:
                    for t in range(n_xb):
                        vtx_ref[h, t] = ext_values(svt, xkv_ref[t * KB:(t + 1) * KB, :])

        if with_x:
            pl.when(pl.program_id(1) == 0)(prepare)
        else:
            prepare()

        eye = eye_ref[...]
        for tile in range(n_tiles):
            for pr in range(ATT_HEADS // 2):
                qt_ref[tile, pr] = lax.dot_general(
                    eye, q_ref[tile * TC:(tile + 1) * TC, pr * LANES:(pr + 1) * LANES], NT_DIMS,
                    preferred_element_type=F32).astype(BF16)
        m_ref[...] = jnp.full_like(m_ref, -jnp.inf)
        acc_ref[...] = jnp.zeros_like(acc_ref)

        def scores(nxt, hd):
            tile, slot, k0, nk, _ = nxt
            h, pr, e = hd // ATT_GROUP, hd // 2, hd % 2
            s_ref[slot, hd, 0:nk, :] = jnp.dot(kz_ref[h, e, pl.ds(k0, nk), :], qt_ref[tile, pr],
                                               preferred_element_type=F32)

        def stage(cur, nxt):
            for hd in range(ATT_HEADS):
                if nxt is not None:
                    scores(nxt, hd)
                if cur is not None:
                    tile, slot, _, nk, vt_of = cur
                    s3 = s_ref[slot, hd, 0:nk, :].reshape(nk // SUBLANES, SUBLANES, TC)
                    p = jnp.exp2(s3 - m_ref[tile, hd][None]).reshape(nk, TC).astype(BF16)
                    pv = jnp.dot(vt_of(hd // ATT_GROUP), p, preferred_element_type=F32)
                    acc3 = (acc_ref[tile, hd].reshape(VROWS // SUBLANES, SUBLANES, TC)
                            * al_ref[tile, hd][None])
                    acc_ref[tile, hd] = acc3.reshape(VROWS, TC) + pv
                if nxt is not None:
                    tile, slot, _, nk, _ = nxt
                    s3 = s_ref[slot, hd, 0:nk, :].reshape(nk // SUBLANES, SUBLANES, TC)
                    m_col = jnp.max(jnp.max(s3, axis=0), axis=0, keepdims=True)
                    m_prev = m_ref[tile, hd]
                    m_next = jnp.maximum(m_prev, m_col)
                    al_ref[tile, hd] = jnp.exp2(m_prev - m_next)
                    m_ref[tile, hd] = m_next

        def finish(tile):
            outs = []
            for hd in range(ATT_HEADS):
                a = acc_ref[tile, hd]
                outs.append(a[0:ATT_HD, :] / a[ATT_HD:ATT_HD + 1, :])
            o_nat = jnp.concatenate(outs, axis=0).T
            rows = slice(tile * TC, (tile + 1) * TC)
            o_ref[rows, :] = (o_nat * g_ref[rows, :].astype(F32)).astype(BF16)

        def ctx_stage(tile, slot):
            return (tile, slot, 0, CTX_LEN, lambda h: vtc_ref[h])

        def x_stage(tile, t, slot):
            k0 = CTX_LEN + t * KB
            if not isinstance(t, int):
                k0 = pl.multiple_of(k0, CTX_LEN)
            return (tile, slot, k0, KB, lambda h: vtx_ref[h, t])

        if not with_x:
            stage(None, ctx_stage(0, 0))
            stage(ctx_stage(0, 0), None)
            finish(0)
            return
        stage(None, ctx_stage(0, 1))
        for tile in range(n_tiles):
            s0 = tile % 2
            stage(ctx_stage(tile, 1 - s0), x_stage(tile, 0, s0))

            def body(j, carry, tile=tile, s0=s0):
                t = 2 * j
                stage(x_stage(tile, t, s0), x_stage(tile, t + 1, 1 - s0))
                stage(x_stage(tile, t + 1, 1 - s0), x_stage(tile, t + 2, s0))
                return carry
            lax.fori_loop(0, (n_xb - 2) // 2, body, 0)
            stage(x_stage(tile, n_xb - 2, s0), x_stage(tile, n_xb - 1, 1 - s0))
            nxt = ctx_stage(tile + 1, s0) if tile + 1 < n_tiles else None
            stage(x_stage(tile, n_xb - 1, 1 - s0), nxt)
            finish(tile)

    return kernel


ATT_XT = 2


def _attn(p, selk, selvt, eye):
    cq = P_AQ // ATT_Q
    cg = P_AG // ATT_Q
    ckv = P_AKV // (2 * ATT_KV)
    tq = ATT_XT * TC
    per_seq = SEQ // tq
    consts = [pl.BlockSpec((ATT_KV_HEADS, 2, 2 * ATT_KV, LANES), lambda *_: (0, 0, 0, 0)),
              pl.BlockSpec((ATT_KV_HEADS, VROWS, 2 * ATT_KV), lambda *_: (0, 0, 0)),
              pl.BlockSpec((LANES, LANES), lambda *_: (0, 0))]

    def scratch(n_tiles, n_keys):
        return [pltpu.VMEM((ATT_KV_HEADS, 2, n_keys, LANES), BF16),
                pltpu.VMEM((ATT_KV_HEADS, VROWS, CTX_LEN), BF16)
                ] + ([pltpu.VMEM((ATT_KV_HEADS, SEQ // KB, VROWS, KB), BF16)] if n_keys > CTX_LEN else []
                ) + [pltpu.VMEM((n_tiles, ATT_HEADS // 2, LANES, TC), BF16),
                     pltpu.VMEM((n_tiles, ATT_HEADS, SUBLANES, TC), F32),
                     pltpu.VMEM((n_tiles, ATT_HEADS, SUBLANES, TC), F32),
                     pltpu.VMEM((n_tiles, ATT_HEADS, VROWS, TC), F32),
                     pltpu.VMEM((2 if n_keys > CTX_LEN else 1, ATT_HEADS,
                                 KB if n_keys > CTX_LEN else CTX_LEN, TC), F32)]

    oc = pl.pallas_call(
        _make_attn_kernel(ATT_XT, True),
        grid=(BATCH, per_seq),
        in_specs=[pl.BlockSpec((tq, ATT_Q), lambda b, qi: (b * per_seq + qi, cq)),
                  pl.BlockSpec((tq, ATT_Q), lambda b, qi: (b * per_seq + qi, cg)),
                  pl.BlockSpec((CTX_LEN, 2 * ATT_KV), lambda b, qi: (XB + b, ckv)),
                  pl.BlockSpec((SEQ, 2 * ATT_KV), lambda b, qi: (b, ckv))] + consts,
        out_specs=pl.BlockSpec((tq, ATT_Q), lambda b, qi: (b * per_seq + qi, 0)),
        out_shape=jax.ShapeDtypeStruct((M_ALL, ATT_Q), BF16),
        scratch_shapes=scratch(ATT_XT, NKEYS),
        compiler_params=_cparams(2),
        name="attention",
    )(p, p, p, p, selk, selvt, eye)
    return pl.pallas_call(
        _make_attn_kernel(1, False),
        grid=(BATCH,),
        in_specs=[pl.BlockSpec((TC, ATT_Q), lambda b: (XB + b, cq)),
                  pl.BlockSpec((TC, ATT_Q), lambda b: (XB + b, cg)),
                  pl.BlockSpec((CTX_LEN, 2 * ATT_KV), lambda b: (XB + b, ckv))] + consts
                 + [pl.BlockSpec(memory_space=pl.ANY)],
        out_specs=pl.BlockSpec((TC, ATT_Q), lambda b: (XB + b, 0)),
        out_shape=jax.ShapeDtypeStruct((M_ALL, ATT_Q), BF16),
        input_output_aliases={6: 0},
        scratch_shapes=scratch(1, CTX_LEN),
        compiler_params=_cparams(1),
        name="attention_ctx",
    )(p, p, p, selk, selvt, eye, oc)


def _final_kernel(sx_ref, sc_ref, sig_ref, gg_ref, ua_ref, of_ref, ob_ref, oc_ref, mod_ref,
                  wc_ref, wg_ref, wa_ref, wo_ref, gn_ref, o_ref):
    i = pl.program_id(0)
    row = jnp.where(i >= N_XT, BATCH, i // TILES_PER_SEQ)
    ya = jnp.dot(ua_ref[...], wc_ref[...], preferred_element_type=F32)
    og = of_ref[...].astype(F32) + ob_ref[...].astype(F32)
    gn = gn_ref[...]
    parts = []
    for h in range(GLA_HEADS):
        oh = og[:, h * GLA_DV:(h + 1) * GLA_DV]
        ms = jnp.mean(oh * oh, axis=-1, keepdims=True)
        parts.append(oh * lax.rsqrt(ms + EPS) * gn)
    on = jnp.concatenate(parts, axis=1) * gg_ref[...].astype(F32)
    yb = jnp.dot(on.astype(BF16), wg_ref[...], preferred_element_type=F32)
    yc = jnp.dot(oc_ref[...], wa_ref[...], preferred_element_type=F32)
    merged = (sig_ref[:, 0:D_MODEL].astype(F32) * ya
              + sig_ref[:, D_MODEL:2 * D_MODEL].astype(F32) * yb
              + sig_ref[:, 2 * D_MODEL:3 * D_MODEL].astype(F32) * yc)
    out = jnp.dot(merged.astype(BF16), wo_ref[...], preferred_element_type=F32)
    gate = mod_ref[pl.ds(row, 1), 2 * D_MODEL:3 * D_MODEL]
    o_ref[...] = jnp.where(i >= N_XT, sc_ref[...], sx_ref[...]) + gate * out


def _final(l, n_tiles, xs, cs, ctx_off, p, ua, o_gla, oc, mod, wc, wg, wa, wo, gn):
    rowblk = lambda i: (i, 0)
    wspec = lambda k: pl.BlockSpec((None, k, D_MODEL), lambda i: (l, 0, 0))
    return pl.pallas_call(
        _final_kernel,
        grid=(n_tiles,),
        in_specs=_stream_specs(ctx_off) + [
                  pl.BlockSpec((TM, 3 * D_MODEL), lambda i: (i, P_SIG // (3 * D_MODEL))),
                  pl.BlockSpec((TM, GLA_V), lambda i: (i, P_GG // GLA_V)),
                  pl.BlockSpec((TM, D_CONV), rowblk),
                  pl.BlockSpec((TM, GLA_V), rowblk),
                  pl.BlockSpec((TM, GLA_V), rowblk),
                  pl.BlockSpec((TM, ATT_Q), rowblk),
                  pl.BlockSpec((None, 16, 3 * D_MODEL), lambda i: (l, 0, 0)),
                  wspec(D_CONV), wspec(GLA_V), wspec(ATT_Q), wspec(D_MODEL),
                  pl.BlockSpec((None, 1, GLA_DV), lambda i: (l, 0, 0))],
        out_specs=pl.BlockSpec((TM, D_MODEL), rowblk),
        out_shape=jax.ShapeDtypeStruct((n_tiles * TM, D_MODEL), F32),
        compiler_params=_cparams(1),
        name="merge_out",
    )(xs, cs, p, p, ua, o_gla[0], o_gla[1], oc, mod, wc, wg, wa, wo, gn)


def _rope_tables():
    t = np.arange(SEQ)
    row = (t // GRID_W).astype(np.float32)
    col = (t % GRID_W).astype(np.float32)
    n_freq = ROPE_AXIS_DIM // 2
    freqs = (np.float32(ROPE_THETA) ** (-np.arange(n_freq, dtype=np.float32) / n_freq)).astype(np.float32)
    ar = row[:, None] * freqs
    ac = col[:, None] * freqs
    cos64 = np.concatenate([np.cos(ar), np.cos(ar), np.cos(ac), np.cos(ac)], axis=1)
    sin64 = np.concatenate([-np.sin(ar), np.sin(ar), -np.sin(ac), np.sin(ac)], axis=1)
    return (np.tile(cos64, (1, ATT_HEADS)).astype(np.float32),
            np.tile(sin64, (1, ATT_HEADS)).astype(np.float32))


def _static_tables():
    bd = np.kron(np.eye(ATT_HEADS, dtype=np.float32), np.ones((ATT_HD, ATT_HD), np.float32))
    idx = np.arange(GLA_CHUNK)
    tri = np.stack([(idx[None, :] <= idx[:, None]), (idx[None, :] >= idx[:, None])]).astype(np.float32)
    selk = np.zeros((ATT_KV_HEADS, 2, 2 * ATT_KV, LANES), np.float32)
    selvt = np.zeros((ATT_KV_HEADS, VROWS, 2 * ATT_KV), np.float32)
    for h in range(ATT_KV_HEADS):
        for dd in range(ATT_HD):
            selvt[h, dd, ATT_KV + h * ATT_HD + dd] = 1.0
            for e in range(2):
                selk[h, e, h * ATT_HD + dd, e * ATT_HD + dd] = 1.0
    shifts = np.stack([np.eye(CONV_SH, k=r, dtype=np.float32) for r in range(SUBLANES)])
    return bd, tri, selk, selvt, shifts


def _split_cols(a):
    pad = jnp.zeros(a.shape[:-1] + (LANES - 2 * GLA_RANK,), a.dtype)
    return (a[..., :O_LR], a[..., O_AQ:], jnp.concatenate([a[..., O_LR:O_AQ], pad], axis=-1))


def kernel(x, c, ctx, c_ctx, norm_g, w_mod, b_mod, w_in, b_in, conv_dw_w, conv_dw_b, conv_ln_g,
           conv_ln_b, w_conv_out, gla_w_gate, gla_b_gate, gla_norm_g, w_gla_out, q_norm_g,
           k_norm_g, w_attn_out, w_out):
    cos_np, sin_np = _rope_tables()
    bd_np, tri_np, selk_np, selvt_np, shifts_np = _static_tables()
    shifts = jnp.asarray(shifts_np, BF16)
    cos_t, sin_t = jnp.asarray(cos_np), jnp.asarray(sin_np)
    bd = jnp.asarray(bd_np, BF16)
    tri = jnp.asarray(tri_np)
    selk = jnp.asarray(selk_np, BF16)
    selvt = jnp.asarray(selvt_np, BF16)
    eye = jnp.eye(LANES, dtype=BF16)

    w_bf = w_in.astype(BF16)
    w_abc = (w_bf,) + _split_cols(w_bf)[1:]
    b_abc = tuple(b.reshape(DEPTH, 1, b.shape[-1]) for b in _split_cols(b_in))
    cc = jnp.concatenate([c, c_ctx[None, :], jnp.zeros((16 - BATCH - 1, D_MODEL), F32)], axis=0)
    dw_w = jnp.broadcast_to(conv_dw_w[:, :, None, :], (DEPTH, CONV_WIDTH, SUBLANES, D_CONV))
    r3 = lambda a: a.reshape(DEPTH, 1, a.shape[-1])
    wup = jnp.zeros((DEPTH, LANES, 2 * GLA_QK), F32)
    wup = wup.at[:, 0:GLA_RANK, 0:GLA_QK].set(gla_w_gate[:, 0])
    wup = wup.at[:, GLA_RANK:2 * GLA_RANK, GLA_QK:].set(gla_w_gate[:, 1])
    wuh = wup.astype(BF16)
    wul = (wup - wuh.astype(F32)).astype(BF16)
    bup = gla_b_gate.reshape(DEPTH, 1, 2 * GLA_QK)
    qg = jnp.tile(q_norm_g, (1, ATT_HEADS)).reshape(DEPTH, 1, ATT_Q)
    kg = jnp.tile(k_norm_g, (1, ATT_KV_HEADS)).reshape(DEPTH, 1, ATT_KV)
    wc, wg, wa, wo = (w.astype(BF16) for w in (w_conv_out, w_gla_out, w_attn_out, w_out))

    mod = _modulation(cc, w_mod, b_mod)
    xs, cs, ctx_off = x.reshape(MX, D_MODEL), ctx.reshape(MC, D_MODEL), 0
    for l in range(DEPTH):
        p, dec = _inproj(l, xs, cs, ctx_off, mod, r3(norm_g), w_abc, b_abc, cos_t, sin_t, qg, kg, bd, wuh, wul, bup)
        ua = _conv(l, p, dw_w, r3(conv_dw_b), r3(conv_ln_g), r3(conv_ln_b), shifts)
        o_gla = _gla(p, dec, tri)
        oc = _attn(p, selk, selvt, eye)
        n_tiles = M_ALL // TM if l < DEPTH - 1 else N_XT
        xs = _final(l, n_tiles, xs, cs, ctx_off, p, ua, o_gla, oc, mod, wc, wg, wa, wo, r3(gla_norm_g))
        cs, ctx_off = xs, N_XT
    return xs.reshape(BATCH, SEQ, D_MODEL)
```

```python
import math

import numpy as np
import jax
import jax.numpy as jnp
from jax import lax
from jax.experimental import pallas as pl
from jax.experimental.pallas import tpu as pltpu

F32 = jnp.float32
BF16 = jnp.bfloat16
HIGHEST = lax.Precision.HIGHEST

D_MODEL = 1024
BATCH = 8
SEQ = 4096
DEPTH = 4
CTX_LEN = 256
GRID_W = 64
EPS = 1e-6
D_CONV = 512
CONV_WIDTH = 31
CONV_PAD = CONV_WIDTH // 2
GLA_HEADS = 4
GLA_DK = 64
GLA_DV = 128
GLA_QK = GLA_HEADS * GLA_DK
GLA_V = GLA_HEADS * GLA_DV
GLA_RANK = 16
GLA_GATE_NORM = 16.0
GLA_CHUNK = 64
ATT_HEADS = 8
ATT_KV_HEADS = 2
ATT_GROUP = ATT_HEADS // ATT_KV_HEADS
ATT_HD = 64
ATT_Q = ATT_HEADS * ATT_HD
ATT_KV = ATT_KV_HEADS * ATT_HD
ROPE_AXIS_DIM = ATT_HD // 2
ROPE_THETA = 10000.0

LANES = 128
SUBLANES = 8
MX = BATCH * SEQ
MC = BATCH * CTX_LEN
M_ALL = MX + MC

O_VAL, O_GLU, O_CGATE = 0, 512, 1024
O_GQ, O_GK, O_GV, O_GG = 1536, 1792, 2048, 2560
O_LR = 3072
O_AQ, O_AK, O_AV, O_AG = 3104, 3616, 3744, 3872
O_MA = 4384
N_IN = 7456

A_VAL, A_GLU, A_CG, A_GQK, A_GV, A_GG = 0, 512, 1024, 1536, 2048, 2560
NA = O_LR
B_AQ, B_AKV, B_AG, B_M = 0, 512, 768, 1280
NB = N_IN - O_AQ

P_SIG = 0
P_U = 3072
P_CG = 3584
P_GV = 4096
P_GD = 4608
GD_W = 3 * GLA_QK
P_GG = 6144
P_AQ = 6656
P_AG = 7168
P_AKV = 7680
NP = 7936

TM = 512
N_XT = MX // TM
N_CT = MC // TM
TILES_PER_SEQ = SEQ // TM
TC = 256
XB = MX // TC
BLK_PER_SEQ = SEQ // TC
HALO = 16
CONV_RC = 32
CONV_SH = TC // 2 + 2 * HALO
KB = 512
Q_PRESCALE = (ATT_HD ** -0.5) * math.log2(math.e)

VMEM_LIMIT = 56 * 1024 * 1024


def _cparams(n_axes, vmem=VMEM_LIMIT):
    return pltpu.CompilerParams(dimension_semantics=("arbitrary",) * n_axes,
                                vmem_limit_bytes=vmem)


def _silu(x):
    return x * jax.nn.sigmoid(x)


def _mod_kernel(c_ref, w_ref, b_ref, o_ref):
    s = _silu(c_ref[...])
    o_ref[...] = jnp.dot(s, w_ref[...], preferred_element_type=F32, precision=HIGHEST) + b_ref[...]


def _modulation(cc, w_mod, b_mod):
    nt = 3 * D_MODEL // 1024
    return pl.pallas_call(
        _mod_kernel,
        grid=(DEPTH, nt),
        in_specs=[pl.BlockSpec((16, D_MODEL), lambda l, n: (0, 0)),
                  pl.BlockSpec((None, D_MODEL, 1024), lambda l, n: (l, 0, n)),
                  pl.BlockSpec((None, 1, 1024), lambda l, n: (l, 0, n))],
        out_specs=pl.BlockSpec((None, 16, 1024), lambda l, n: (l, 0, n)),
        out_shape=jax.ShapeDtypeStruct((DEPTH, 16, 3 * D_MODEL), F32),
        compiler_params=_cparams(2),
        name="modulation",
    )(cc, w_mod, b_mod.reshape(DEPTH, 1, 3 * D_MODEL))


def _head_norm(xv, gain, bd):
    ss = jnp.dot((xv * xv).astype(BF16), bd, preferred_element_type=F32)
    return xv * lax.rsqrt(ss * (1.0 / ATT_HD) + EPS) * gain


def _rope(xv, cosv, sinv):
    parts = []
    for s in range(xv.shape[1] // LANES):
        sl = slice(s * LANES, (s + 1) * LANES)
        xs = xv[:, sl]
        up = pltpu.roll(xs, LANES - 16, axis=1)
        dn = pltpu.roll(xs, 16, axis=1)
        lane = lax.broadcasted_iota(jnp.int32, xs.shape, 1)
        partner = jnp.where((lane & 16) == 0, up, dn)
        parts.append(xs * cosv[:, sl] + partner * sinv[:, sl])
    return jnp.concatenate(parts, axis=1) if len(parts) > 1 else parts[0]


def _inproj_kernel(x_ref, c_ref, mod_ref, g_ref, wa_ref, wb_ref, wc_ref, ba_ref, bb_ref, bc_ref,
                   cos_ref, sin_ref, qg_ref, kg_ref,
                   bd_ref, wuh_ref, wul_ref, bup_ref, o_ref, dec_ref):
    i = pl.program_id(0)
    is_ctx = i >= N_XT
    row = jnp.where(is_ctx, BATCH, i // TILES_PER_SEQ)
    x = jnp.where(is_ctx, c_ref[...], x_ref[...])
    ms = jnp.mean(x * x, axis=-1, keepdims=True)
    y = x * lax.rsqrt(ms + EPS) * g_ref[...]
    m = mod_ref[pl.ds(row, 1), :]
    shift = m[:, 0:D_MODEL]
    scale = m[:, D_MODEL:2 * D_MODEL]
    h = (y * (1.0 + scale) + shift).astype(BF16)

    def proj_from(w_ref, b_ref):
        def proj(a, n):
            return jnp.dot(h, w_ref[:, a:a + n], preferred_element_type=F32) + b_ref[:, a:a + n]
        return proj

    proj_a = proj_from(wa_ref, ba_ref)
    proj_b = proj_from(wb_ref, bb_ref)
    proj_c = proj_from(wc_ref, bc_ref)

    def put(a, val):
        o_ref[:, a:a + val.shape[1]] = val.astype(BF16)

    lr = proj_c(0, LANES)
    aq_raw = proj_b(B_AQ, ATT_Q)
    akv = proj_b(B_AKV, 2 * ATT_KV)
    def put_gate(k):
        for a in range(0, D_MODEL, D_MODEL // 2):
            put(P_SIG + k * D_MODEL + a,
                jax.nn.sigmoid(proj_b(B_M + k * D_MODEL + a, D_MODEL // 2)))

    put_gate(0)

    lr_hi = lr.astype(BF16)
    lr_lo = (lr - lr_hi.astype(F32)).astype(BF16)
    wuh = wuh_ref[...]
    z = (jnp.dot(lr_hi, wuh, preferred_element_type=F32)
         + jnp.dot(lr_lo, wuh, preferred_element_type=F32)
         + jnp.dot(lr_hi, wul_ref[...], preferred_element_type=F32)) + bup_ref[...]
    put_gate(1)

    cosv = jnp.where(is_ctx, 1.0, cos_ref[...])
    sinv = jnp.where(is_ctx, 0.0, sin_ref[...])
    bd = bd_ref[...]
    aq = _head_norm(aq_raw, qg_ref[...], bd)
    put(P_AQ, _rope(aq, cosv, sinv) * Q_PRESCALE)
    ak = _head_norm(akv[:, :ATT_KV], kg_ref[...], bd[:ATT_KV, :ATT_KV])
    put(P_AKV, _rope(ak, cosv[:, :ATT_KV], sinv[:, :ATT_KV]))
    put(P_AKV + ATT_KV, akv[:, ATT_KV:])
    put_gate(2)

    la = (jnp.minimum(z, 0.0) - jnp.log(1.0 + jnp.exp(-jnp.abs(z)))) * (1.0 / GLA_GATE_NORM)
    rowc = lax.broadcasted_iota(jnp.int32, (TM, GLA_QK), 0) & (GLA_CHUNK - 1)
    cf = la[:, :GLA_QK]
    cb = la[:, GLA_QK:]
    sh = 1
    while sh < GLA_CHUNK:
        cf = cf + jnp.where(rowc >= sh, pltpu.roll(cf, sh, axis=0), 0.0)
        cb = cb + jnp.where(rowc < GLA_CHUNK - sh, pltpu.roll(cb, TM - sh, axis=0), 0.0)
        sh *= 2
    n_ch = TM // GLA_CHUNK
    last_f = [cf[c * GLA_CHUNK + GLA_CHUNK - 1:(c + 1) * GLA_CHUNK, :] for c in range(n_ch)]
    last_b = [cb[c * GLA_CHUNK:c * GLA_CHUNK + 1, :] for c in range(n_ch)]
    dec_ref[0] = jnp.concatenate(last_f, axis=0)
    dec_ref[1] = jnp.concatenate(last_b, axis=0)
    qk = proj_a(A_GQK, 2 * GLA_QK)
    gq = qk[:, :GLA_QK] * (GLA_DK ** -0.5)
    gk = qk[:, GLA_QK:]
    for dd, (cum, last) in enumerate(((cf, last_f), (cb, last_b))):
        tot = jnp.concatenate([jnp.broadcast_to(t, (GLA_CHUNK, GLA_QK)) for t in last], axis=0)
        base = P_GD + dd * GD_W
        put(base, gq * jnp.exp(cum))
        put(base + GLA_QK, gk * jnp.exp(-cum))
        put(base + 2 * GLA_QK, gk * jnp.exp(tot - cum))

    put(P_U, proj_a(A_VAL, D_CONV) * jax.nn.sigmoid(proj_a(A_GLU, D_CONV)))
    put(P_CG, _silu(proj_a(A_CG, D_CONV)))
    put(P_GG, _silu(proj_a(A_GG, GLA_V)))
    put(P_AG, _silu(proj_b(B_AG, ATT_Q)))
    put(P_GV, proj_a(A_GV, GLA_V))


def _stream_specs(ctx_off):
    return [pl.BlockSpec((TM, D_MODEL), lambda i: (jnp.minimum(i, N_XT - 1), 0)),
            pl.BlockSpec((TM, D_MODEL), lambda i: (jnp.maximum(i - N_XT, 0) + ctx_off, 0))]


def _inproj(l, xs, cs, ctx_off, mod, norm_g, w_abc, b_abc, cos_t, sin_t, qg, kg, bd, wuh, wul, bup):
    const = lambda i: (0, 0)
    pos = lambda i: (jnp.where(i >= N_XT, 0, i % TILES_PER_SEQ), 0)
    return pl.pallas_call(
        _inproj_kernel,
        grid=(M_ALL // TM,),
        in_specs=_stream_specs(ctx_off) + [
                  pl.BlockSpec((None, 16, 3 * D_MODEL), lambda i: (l, 0, 0)),
                  pl.BlockSpec((None, 1, D_MODEL), lambda i: (l, 0, 0)),
                  ] + [pl.BlockSpec((None, D_MODEL, n), lambda i: (l, 0, 0),
                                    pipeline_mode=pl.Buffered(1)) for n in (NA, NB, LANES)
                  ] + [pl.BlockSpec((None, 1, b.shape[-1]), lambda i: (l, 0, 0)) for b in b_abc
                  ] + [
                  pl.BlockSpec((TM, ATT_Q), pos),
                  pl.BlockSpec((TM, ATT_Q), pos),
                  pl.BlockSpec((None, 1, ATT_Q), lambda i: (l, 0, 0)),
                  pl.BlockSpec((None, 1, ATT_KV), lambda i: (l, 0, 0)),
                  pl.BlockSpec((ATT_Q, ATT_Q), const),
                  pl.BlockSpec((None, LANES, 2 * GLA_QK), lambda i: (l, 0, 0)),
                  pl.BlockSpec((None, LANES, 2 * GLA_QK), lambda i: (l, 0, 0)),
                  pl.BlockSpec((None, 1, 2 * GLA_QK), lambda i: (l, 0, 0))],
        out_specs=[pl.BlockSpec((TM, NP), lambda i: (i, 0)),
                   pl.BlockSpec((2, TM // GLA_CHUNK, GLA_QK), lambda i: (0, i, 0))],
        out_shape=[jax.ShapeDtypeStruct((M_ALL, NP), BF16),
                   jax.ShapeDtypeStruct((2, M_ALL // GLA_CHUNK, GLA_QK), F32)],
        compiler_params=_cparams(1),
        name="inproj",
    )(xs, cs, mod, norm_g, *w_abc, *b_abc, cos_t, sin_t, qg, kg, bd, wuh, wul, bup)


def _conv_kernel(u_ref, ul_ref, ur_ref, sg_ref, dw_ref, dwb_ref, lng_ref, lnb_ref, sh_ref,
                 o_ref, win_ref, ext_ref):
    i = pl.program_id(0)
    j = i % BLK_PER_SEQ
    is_x = i < XB
    left_ok = jnp.logical_and(is_x, j != 0)
    right_ok = jnp.logical_and(is_x, j != BLK_PER_SEQ - 1)
    zero_h = jnp.zeros((HALO, D_CONV), BF16)
    win_ref[0:HALO, :] = jnp.where(left_ok, ul_ref[...], zero_h)
    win_ref[HALO:HALO + TC, :] = u_ref[...]
    win_ref[HALO + TC:, :] = jnp.where(right_ok, ur_ref[...], zero_h)
    ext_ref[0] = win_ref[...].astype(F32)
    half = TC // 2
    for r in range(1, SUBLANES):
        for a in range(2):
            ext_ref[r, a * half:a * half + CONV_SH, :] = jnp.dot(
                sh_ref[r], win_ref[a * half:a * half + CONV_SH, :], preferred_element_type=F32)
    bias = dwb_ref[...]
    lng = lng_ref[...]
    lnb = lnb_ref[...]
    for c in range(TC // CONV_RC):
        r0 = c * CONV_RC
        acc = jnp.broadcast_to(bias, (CONV_RC, D_CONV))
        for t in range(CONV_WIDTH):
            off = t + HALO - CONV_PAD
            a0 = r0 + (off // SUBLANES) * SUBLANES
            w_t = jnp.concatenate([dw_ref[t]] * (CONV_RC // SUBLANES), axis=0)
            acc = acc + ext_ref[off % SUBLANES, a0:a0 + CONV_RC, :] * w_t
        mu = jnp.mean(acc, axis=-1, keepdims=True)
        d = acc - mu
        var = jnp.mean(d * d, axis=-1, keepdims=True)
        yn = d * lax.rsqrt(var + EPS) * lng + lnb
        o_ref[r0:r0 + CONV_RC, :] = (_silu(yn) * sg_ref[r0:r0 + CONV_RC, :].astype(F32)).astype(BF16)


def _conv(l, p, dw_w, dw_b, ln_g, ln_b, shifts):
    nhb = M_ALL // HALO
    per = TC // HALO
    cu = P_U // D_CONV
    cg = P_CG // D_CONV
    vec = lambda i: (l, 0, 0)
    return pl.pallas_call(
        _conv_kernel,
        grid=(M_ALL // TC,),
        in_specs=[pl.BlockSpec((TC, D_CONV), lambda i: (i, cu)),
                  pl.BlockSpec((HALO, D_CONV), lambda i: (jnp.maximum(i * per - 1, 0), cu)),
                  pl.BlockSpec((HALO, D_CONV), lambda i: (jnp.minimum((i + 1) * per, nhb - 1), cu)),
                  pl.BlockSpec((TC, D_CONV), lambda i: (i, cg)),
                  pl.BlockSpec((None, CONV_WIDTH, SUBLANES, D_CONV), lambda i: (l, 0, 0, 0)),
                  pl.BlockSpec((None, 1, D_CONV), vec),
                  pl.BlockSpec((None, 1, D_CONV), vec),
                  pl.BlockSpec((None, 1, D_CONV), vec),
                  pl.BlockSpec((SUBLANES, CONV_SH, CONV_SH), lambda i: (0, 0, 0))],
        out_specs=pl.BlockSpec((TC, D_CONV), lambda i: (i, 0)),
        out_shape=jax.ShapeDtypeStruct((M_ALL, D_CONV), BF16),
        scratch_shapes=[pltpu.VMEM((TC + 2 * HALO, D_CONV), BF16),
                        pltpu.VMEM((SUBLANES, TC + 2 * HALO, D_CONV), F32)],
        compiler_params=_cparams(1),
        name="conv",
    )(p, p, p, p, dw_w, dw_b, ln_g, ln_b, shifts)


def _gla_kernel(qf_ref, qb_ref, vf_ref, vb_ref, df_ref, db_ref, tri_ref, of_ref, ob_ref, st_ref):
    step = pl.program_id(1)

    @pl.when(step == 0)
    def _():
        st_ref[...] = jnp.zeros_like(st_ref)

    lane_h = lax.broadcasted_iota(jnp.int32, (GLA_CHUNK, GLA_QK), 1) // GLA_DK
    srow_h = lax.broadcasted_iota(jnp.int32, (GLA_V, GLA_QK), 0) // GLA_DV
    scol_h = lax.broadcasted_iota(jnp.int32, (GLA_V, GLA_QK), 1) // GLA_DK
    smask = srow_h == scol_h
    nt = (((1,), (1,)), ((), ()))
    n_chunks = TC // GLA_CHUNK
    dirs = ((qf_ref, vf_ref, df_ref, of_ref), (qb_ref, vb_ref, db_ref, ob_ref))
    work = []
    for n in range(n_chunks):
        for d, (qkk_ref, v_ref, dec_ref, o_ref) in enumerate(dirs):
            c = n if d == 0 else n_chunks - 1 - n
            rows = slice(c * GLA_CHUNK, (c + 1) * GLA_CHUNK)
            q_in = qkk_ref[rows, 0:GLA_QK]
            k_in = qkk_ref[rows, GLA_QK:2 * GLA_QK]
            k_st = qkk_ref[rows, 2 * GLA_QK:3 * GLA_QK]
            v = v_ref[rows, :]
            decay = jnp.exp(dec_ref[c:c + 1, :])
            tri4 = jnp.concatenate([tri_ref[d]] * GLA_HEADS, axis=0) > 0.5
            q_stack = jnp.concatenate(
                [jnp.where(lane_h == h, q_in, jnp.zeros_like(q_in)) for h in range(GLA_HEADS)],
                axis=0)
            att = lax.dot_general(q_stack, k_in, nt, preferred_element_type=F32)
            att = jnp.where(tri4, att, 0.0).astype(BF16)
            kvt = lax.dot_general(v, k_st, (((0,), (0,)), ((), ())),
                                  preferred_element_type=F32)
            work.append((d, o_ref, rows, q_in, v, decay, att, jnp.where(smask, kvt, 0.0)))
    for d, o_ref, rows, q_in, v, decay, att, kvt in work:
        st_old = st_ref[d]
        o_inter = lax.dot_general(q_in, st_old.astype(BF16), nt, preferred_element_type=F32)
        o_intra = jnp.concatenate(
            [jnp.dot(att[h * GLA_CHUNK:(h + 1) * GLA_CHUNK, :],
                     v[:, h * GLA_DV:(h + 1) * GLA_DV], preferred_element_type=F32)
             for h in range(GLA_HEADS)], axis=1)
        o_ref[rows, :] = (o_intra + o_inter).astype(BF16)
        st_ref[d] = st_old * decay + kvt


def _gla(p, dec, tri):
    def rbf(b, s):
        return jnp.where(s == 0, XB + b, b * BLK_PER_SEQ + s - 1)

    def rbb(b, s):
        return jnp.where(s == 0, XB + b, b * BLK_PER_SEQ + BLK_PER_SEQ - s)

    cgd = P_GD // GD_W
    cv = P_GV // GLA_V
    n_ch = TC // GLA_CHUNK
    dec4 = dec.reshape(2, M_ALL // TC, n_ch, GLA_QK)
    out = jax.ShapeDtypeStruct((M_ALL, GLA_V), BF16)
    return pl.pallas_call(
        _gla_kernel,
        grid=(BATCH, BLK_PER_SEQ + 1),
        in_specs=[pl.BlockSpec((TC, GD_W), lambda b, s: (rbf(b, s), cgd)),
                  pl.BlockSpec((TC, GD_W), lambda b, s: (rbb(b, s), cgd + 1)),
                  pl.BlockSpec((TC, GLA_V), lambda b, s: (rbf(b, s), cv)),
                  pl.BlockSpec((TC, GLA_V), lambda b, s: (rbb(b, s), cv)),
                  pl.BlockSpec((None, None, n_ch, GLA_QK), lambda b, s: (0, rbf(b, s), 0, 0)),
                  pl.BlockSpec((None, None, n_ch, GLA_QK), lambda b, s: (1, rbb(b, s), 0, 0)),
                  pl.BlockSpec((2, GLA_CHUNK, GLA_CHUNK), lambda b, s: (0, 0, 0))],
        out_specs=[pl.BlockSpec((TC, GLA_V), lambda b, s: (rbf(b, s), 0)),
                   pl.BlockSpec((TC, GLA_V), lambda b, s: (rbb(b, s), 0))],
        out_shape=[out, out],
        scratch_shapes=[pltpu.VMEM((2, GLA_V, GLA_QK), F32)],
        compiler_params=_cparams(2),
        name="gla",
    )(p, p, p, p, dec4, dec4, tri)


NKEYS = CTX_LEN + SEQ
VROWS = 80
NT_DIMS = (((1,), (1,)), ((), ()))


def _make_attn_kernel(n_tiles, with_x):
    n_xb = SEQ // KB

    def kernel(*refs):
        if with_x:
            (q_ref, g_ref, ckv_ref, xkv_ref, selk_ref, selvt_ref, eye_ref, o_ref,
             kz_ref, vtc_ref, vtx_ref, qt_ref, m_ref, al_ref, acc_ref, s_ref) = refs
            key_srcs = ((ckv_ref, 0, CTX_LEN), (xkv_ref, CTX_LEN, SEQ))
        else:
            (q_ref, g_ref, ckv_ref, selk_ref, selvt_ref, eye_ref, _, o_ref,
             kz_ref, vtc_ref, qt_ref, m_ref, al_ref, acc_ref, s_ref) = refs
            key_srcs = ((ckv_ref, 0, CTX_LEN),)
        row_v = lax.broadcasted_iota(jnp.int32, (VROWS, 1), 0)

        def ext_values(svt, blk):
            vt = lax.dot_general(svt, blk, NT_DIMS, preferred_element_type=F32)
            return jnp.where(row_v == ATT_HD, 1.0, vt).astype(BF16)

        def prepare():
            for h in range(ATT_KV_HEADS):
                for e in range(2):
                    sk = selk_ref[h, e]
                    for (src, r0, n) in key_srcs:
                        for r in range(0, n, 1024):
                            nr = min(1024, n - r)
                            kz_ref[h, e, r0 + r:r0 + r + nr, :] = jnp.dot(
                                src[r:r + nr, :], sk, preferred_element_type=F32).astype(BF16)
                svt = selvt_ref[h]
                vtc_ref[h] = ext_values(svt, ckv_ref[...])
                if with_x:
                    for t in range(n_xb):
                        vtx_ref[h, t] = ext_values(svt, xkv_ref[t * KB:(t + 1) * KB, :])

        if with_x:
            pl.when(pl.program_id(1) == 0)(prepare)
        else:
            prepare()

        eye = eye_ref[...]
        for tile in range(n_tiles):
            for pr in range(ATT_HEADS // 2):
                qt_ref[tile, pr] = lax.dot_general(
                    eye, q_ref[tile * TC:(tile + 1) * TC, pr * LANES:(pr + 1) * LANES], NT_DIMS,
                    preferred_element_type=F32).astype(BF16)
        m_ref[...] = jnp.full_like(m_ref, -jnp.inf)
        acc_ref[...] = jnp.zeros_like(acc_ref)

        def scores(nxt, hd):
            tile, slot, k0, nk, _ = nxt
            h, pr, e = hd // ATT_GROUP, hd // 2, hd % 2
            s_ref[slot, hd, 0:nk, :] = jnp.dot(kz_ref[h, e, pl.ds(k0, nk), :], qt_ref[tile, pr],
                                               preferred_element_type=F32)

        def stage(cur, nxt):
            for hd in range(ATT_HEADS):
                if nxt is not None:
                    scores(nxt, hd)
                if cur is not None:
                    tile, slot, _, nk, vt_of = cur
                    s3 = s_ref[slot, hd, 0:nk, :].reshape(nk // SUBLANES, SUBLANES, TC)
                    p = jnp.exp2(s3 - m_ref[tile, hd][None]).reshape(nk, TC).astype(BF16)
                    pv = jnp.dot(vt_of(hd // ATT_GROUP), p, preferred_element_type=F32)
                    acc3 = (acc_ref[tile, hd].reshape(VROWS // SUBLANES, SUBLANES, TC)
                            * al_ref[tile, hd][None])
                    acc_ref[tile, hd] = acc3.reshape(VROWS, TC) + pv
                if nxt is not None:
                    tile, slot, _, nk, _ = nxt
                    s3 = s_ref[slot, hd, 0:nk, :].reshape(nk // SUBLANES, SUBLANES, TC)
                    m_col = jnp.max(jnp.max(s3, axis=0), axis=0, keepdims=True)
                    m_prev = m_ref[tile, hd]
                    m_next = jnp.maximum(m_prev, m_col)
                    al_ref[tile, hd] = jnp.exp2(m_prev - m_next)
                    m_ref[tile, hd] = m_next

        def finish(tile):
            outs = []
            for hd in range(ATT_HEADS):
                a = acc_ref[tile, hd]
                outs.append(a[0:ATT_HD, :] / a[ATT_HD:ATT_HD + 1, :])
            o_nat = jnp.concatenate(outs, axis=0).T
            rows = slice(tile * TC, (tile + 1) * TC)
            o_ref[rows, :] = (o_nat * g_ref[rows, :].astype(F32)).astype(BF16)

        def ctx_stage(tile, slot):
            return (tile, slot, 0, CTX_LEN, lambda h: vtc_ref[h])

        def x_stage(tile, t, slot):
            k0 = CTX_LEN + t * KB
            if not isinstance(t, int):
                k0 = pl.multiple_of(k0, CTX_LEN)
            return (tile, slot, k0, KB, lambda h: vtx_ref[h, t])

        if not with_x:
            stage(None, ctx_stage(0, 0))
            stage(ctx_stage(0, 0), None)
            finish(0)
            return
        stage(None, ctx_stage(0, 1))
        for tile in range(n_tiles):
            s0 = tile % 2
            stage(ctx_stage(tile, 1 - s0), x_stage(tile, 0, s0))

            def body(j, carry, tile=tile, s0=s0):
                t = 2 * j
                stage(x_stage(tile, t, s0), x_stage(tile, t + 1, 1 - s0))
                stage(x_stage(tile, t + 1, 1 - s0), x_stage(tile, t + 2, s0))
                return carry
            lax.fori_loop(0, (n_xb - 2) // 2, body, 0)
            stage(x_stage(tile, n_xb - 2, s0), x_stage(tile, n_xb - 1, 1 - s0))
            nxt = ctx_stage(tile + 1, s0) if tile + 1 < n_tiles else None
            stage(x_stage(tile, n_xb - 1, 1 - s0), nxt)
            finish(tile)

    return kernel


ATT_XT = 2


def _attn(p, selk, selvt, eye):
    cq = P_AQ // ATT_Q
    cg = P_AG // ATT_Q
    ckv = P_AKV // (2 * ATT_KV)
    tq = ATT_XT * TC
    per_seq = SEQ // tq
    consts = [pl.BlockSpec((ATT_KV_HEADS, 2, 2 * ATT_KV, LANES), lambda *_: (0, 0, 0, 0)),
              pl.BlockSpec((ATT_KV_HEADS, VROWS, 2 * ATT_KV), lambda *_: (0, 0, 0)),
              pl.BlockSpec((LANES, LANES), lambda *_: (0, 0))]

    def scratch(n_tiles, n_keys):
        return [pltpu.VMEM((ATT_KV_HEADS, 2, n_keys, LANES), BF16),
                pltpu.VMEM((ATT_KV_HEADS, VROWS, CTX_LEN), BF16)
                ] + ([pltpu.VMEM((ATT_KV_HEADS, SEQ // KB, VROWS, KB), BF16)] if n_keys > CTX_LEN else []
                ) + [pltpu.VMEM((n_tiles, ATT_HEADS // 2, LANES, TC), BF16),
                     pltpu.VMEM((n_tiles, ATT_HEADS, SUBLANES, TC), F32),
                     pltpu.VMEM((n_tiles, ATT_HEADS, SUBLANES, TC), F32),
                     pltpu.VMEM((n_tiles, ATT_HEADS, VROWS, TC), F32),
                     pltpu.VMEM((2 if n_keys > CTX_LEN else 1, ATT_HEADS,
                                 KB if n_keys > CTX_LEN else CTX_LEN, TC), F32)]

    oc = pl.pallas_call(
        _make_attn_kernel(ATT_XT, True),
        grid=(BATCH, per_seq),
        in_specs=[pl.BlockSpec((tq, ATT_Q), lambda b, qi: (b * per_seq + qi, cq)),
                  pl.BlockSpec((tq, ATT_Q), lambda b, qi: (b * per_seq + qi, cg)),
                  pl.BlockSpec((CTX_LEN, 2 * ATT_KV), lambda b, qi: (XB + b, ckv)),
                  pl.BlockSpec((SEQ, 2 * ATT_KV), lambda b, qi: (b, ckv))] + consts,
        out_specs=pl.BlockSpec((tq, ATT_Q), lambda b, qi: (b * per_seq + qi, 0)),
        out_shape=jax.ShapeDtypeStruct((M_ALL, ATT_Q), BF16),
        scratch_shapes=scratch(ATT_XT, NKEYS),
        compiler_params=_cparams(2),
        name="attention",
    )(p, p, p, p, selk, selvt, eye)
    return pl.pallas_call(
        _make_attn_kernel(1, False),
        grid=(BATCH,),
        in_specs=[pl.BlockSpec((TC, ATT_Q), lambda b: (XB + b, cq)),
                  pl.BlockSpec((TC, ATT_Q), lambda b: (XB + b, cg)),
                  pl.BlockSpec((CTX_LEN, 2 * ATT_KV), lambda b: (XB + b, ckv))] + consts
                 + [pl.BlockSpec(memory_space=pl.ANY)],
        out_specs=pl.BlockSpec((TC, ATT_Q), lambda b: (XB + b, 0)),
        out_shape=jax.ShapeDtypeStruct((M_ALL, ATT_Q), BF16),
        input_output_aliases={6: 0},
        scratch_shapes=scratch(1, CTX_LEN),
        compiler_params=_cparams(1),
        name="attention_ctx",
    )(p, p, p, selk, selvt, eye, oc)


def _final_kernel(sx_ref, sc_ref, sig_ref, gg_ref, ua_ref, of_ref, ob_ref, oc_ref, mod_ref,
                  wc_ref, wg_ref, wa_ref, wo_ref, gn_ref, o_ref):
    i = pl.program_id(0)
    row = jnp.where(i >= N_XT, BATCH, i // TILES_PER_SEQ)
    ya = jnp.dot(ua_ref[...], wc_ref[...], preferred_element_type=F32)
    og = of_ref[...].astype(F32) + ob_ref[...].astype(F32)
    gn = gn_ref[...]
    parts = []
    for h in range(GLA_HEADS):
        oh = og[:, h * GLA_DV:(h + 1) * GLA_DV]
        ms = jnp.mean(oh * oh, axis=-1, keepdims=True)
        parts.append(oh * lax.rsqrt(ms + EPS) * gn)
    on = jnp.concatenate(parts, axis=1) * gg_ref[...].astype(F32)
    yb = jnp.dot(on.astype(BF16), wg_ref[...], preferred_element_type=F32)
    yc = jnp.dot(oc_ref[...], wa_ref[...], preferred_element_type=F32)
    merged = (sig_ref[:, 0:D_MODEL].astype(F32) * ya
              + sig_ref[:, D_MODEL:2 * D_MODEL].astype(F32) * yb
              + sig_ref[:, 2 * D_MODEL:3 * D_MODEL].astype(F32) * yc)
    out = jnp.dot(merged.astype(BF16), wo_ref[...], preferred_element_type=F32)
    gate = mod_ref[pl.ds(row, 1), 2 * D_MODEL:3 * D_MODEL]
    o_ref[...] = jnp.where(i >= N_XT, sc_ref[...], sx_ref[...]) + gate * out


def _final(l, n_tiles, xs, cs, ctx_off, p, ua, o_gla, oc, mod, wc, wg, wa, wo, gn):
    rowblk = lambda i: (i, 0)
    wspec = lambda k: pl.BlockSpec((None, k, D_MODEL), lambda i: (l, 0, 0))
    return pl.pallas_call(
        _final_kernel,
        grid=(n_tiles,),
        in_specs=_stream_specs(ctx_off) + [
                  pl.BlockSpec((TM, 3 * D_MODEL), lambda i: (i, P_SIG // (3 * D_MODEL))),
                  pl.BlockSpec((TM, GLA_V), lambda i: (i, P_GG // GLA_V)),
                  pl.BlockSpec((TM, D_CONV), rowblk),
                  pl.BlockSpec((TM, GLA_V), rowblk),
                  pl.BlockSpec((TM, GLA_V), rowblk),
                  pl.BlockSpec((TM, ATT_Q), rowblk),
                  pl.BlockSpec((None, 16, 3 * D_MODEL), lambda i: (l, 0, 0)),
                  wspec(D_CONV), wspec(GLA_V), wspec(ATT_Q), wspec(D_MODEL),
                  pl.BlockSpec((None, 1, GLA_DV), lambda i: (l, 0, 0))],
        out_specs=pl.BlockSpec((TM, D_MODEL), rowblk),
        out_shape=jax.ShapeDtypeStruct((n_tiles * TM, D_MODEL), F32),
        compiler_params=_cparams(1),
        name="merge_out",
    )(xs, cs, p, p, ua, o_gla[0], o_gla[1], oc, mod, wc, wg, wa, wo, gn)


def _rope_tables():
    t = np.arange(SEQ)
    row = (t // GRID_W).astype(np.float32)
    col = (t % GRID_W).astype(np.float32)
    n_freq = ROPE_AXIS_DIM // 2
    freqs = (np.float32(ROPE_THETA) ** (-np.arange(n_freq, dtype=np.float32) / n_freq)).astype(np.float32)
    ar = row[:, None] * freqs
    ac = col[:, None] * freqs
    cos64 = np.concatenate([np.cos(ar), np.cos(ar), np.cos(ac), np.cos(ac)], axis=1)
    sin64 = np.concatenate([-np.sin(ar), np.sin(ar), -np.sin(ac), np.sin(ac)], axis=1)
    return (np.tile(cos64, (1, ATT_HEADS)).astype(np.float32),
            np.tile(sin64, (1, ATT_HEADS)).astype(np.float32))


def _static_tables():
    bd = np.kron(np.eye(ATT_HEADS, dtype=np.float32), np.ones((ATT_HD, ATT_HD), np.float32))
    idx = np.arange(GLA_CHUNK)
    tri = np.stack([(idx[None, :] <= idx[:, None]), (idx[None, :] >= idx[:, None])]).astype(np.float32)
    selk = np.zeros((ATT_KV_HEADS, 2, 2 * ATT_KV, LANES), np.float32)
    selvt = np.zeros((ATT_KV_HEADS, VROWS, 2 * ATT_KV), np.float32)
    for h in range(ATT_KV_HEADS):
        for dd in range(ATT_HD):
            selvt[h, dd, ATT_KV + h * ATT_HD + dd] = 1.0
            for e in range(2):
                selk[h, e, h * ATT_HD + dd, e * ATT_HD + dd] = 1.0
    shifts = np.stack([np.eye(CONV_SH, k=r, dtype=np.float32) for r in range(SUBLANES)])
    return bd, tri, selk, selvt, shifts


def _split_cols(a):
    pad = jnp.zeros(a.shape[:-1] + (LANES - 2 * GLA_RANK,), a.dtype)
    return (a[..., :O_LR], a[..., O_AQ:], jnp.concatenate([a[..., O_LR:O_AQ], pad], axis=-1))


def kernel(x, c, ctx, c_ctx, norm_g, w_mod, b_mod, w_in, b_in, conv_dw_w, conv_dw_b, conv_ln_g,
           conv_ln_b, w_conv_out, gla_w_gate, gla_b_gate, gla_norm_g, w_gla_out, q_norm_g,
           k_norm_g, w_attn_out, w_out):
    cos_np, sin_np = _rope_tables()
    bd_np, tri_np, selk_np, selvt_np, shifts_np = _static_tables()
    shifts = jnp.asarray(shifts_np, BF16)
    cos_t, sin_t = jnp.asarray(cos_np), jnp.asarray(sin_np)
    bd = jnp.asarray(bd_np, BF16)
    tri = jnp.asarray(tri_np)
    selk = jnp.asarray(selk_np, BF16)
    selvt = jnp.asarray(selvt_np, BF16)
    eye = jnp.eye(LANES, dtype=BF16)

    w_bf = w_in.astype(BF16)
    w_abc = (w_bf,) + _split_cols(w_bf)[1:]
    b_abc = tuple(b.reshape(DEPTH, 1, b.shape[-1]) for b in _split_cols(b_in))
    cc = jnp.concatenate([c, c_ctx[None, :], jnp.zeros((16 - BATCH - 1, D_MODEL), F32)], axis=0)
    dw_w = jnp.broadcast_to(conv_dw_w[:, :, None, :], (DEPTH, CONV_WIDTH, SUBLANES, D_CONV))
    r3 = lambda a: a.reshape(DEPTH, 1, a.shape[-1])
    wup = jnp.zeros((DEPTH, LANES, 2 * GLA_QK), F32)
    wup = wup.at[:, 0:GLA_RANK, 0:GLA_QK].set(gla_w_gate[:, 0])
    wup = wup.at[:, GLA_RANK:2 * GLA_RANK, GLA_QK:].set(gla_w_gate[:, 1])
    wuh = wup.astype(BF16)
    wul = (wup - wuh.astype(F32)).astype(BF16)
    bup = gla_b_gate.reshape(DEPTH, 1, 2 * GLA_QK)
    qg = jnp.tile(q_norm_g, (1, ATT_HEADS)).reshape(DEPTH, 1, ATT_Q)
    kg = jnp.tile(k_norm_g, (1, ATT_KV_HEADS)).reshape(DEPTH, 1, ATT_KV)
    wc, wg, wa, wo = (w.astype(BF16) for w in (w_conv_out, w_gla_out, w_attn_out, w_out))

    mod = _modulation(cc, w_mod, b_mod)
    xs, cs, ctx_off = x.reshape(MX, D_MODEL), ctx.reshape(MC, D_MODEL), 0
    for l in range(DEPTH):
        p, dec = _inproj(l, xs, cs, ctx_off, mod, r3(norm_g), w_abc, b_abc, cos_t, sin_t, qg, kg, bd, wuh, wul, bup)
        ua = _conv(l, p, dw_w, r3(conv_dw_b), r3(conv_ln_g), r3(conv_ln_b), shifts)
        o_gla = _gla(p, dec, tri)
        oc = _attn(p, selk, selvt, eye)
        n_tiles = M_ALL // TM if l < DEPTH - 1 else N_XT
        xs = _final(l, n_tiles, xs, cs, ctx_off, p, ua, o_gla, oc, mod, wc, wg, wa, wo, r3(gla_norm_g))
        cs, ctx_off = xs, N_XT
    return xs.reshape(BATCH, SEQ, D_MODEL)
```
